```python
import jax
import jax.numpy as jnp
from jax import lax
import numpy as np

D_MODEL = 1024
BATCH = 32
SEQ = 2048
DEPTH = 1

CTX_LEN = 256
GRID_W = 64
EPS = 1e-6

ML_WIDTH = D_MODEL // 2
ML_HEADS = 4
ML_HEAD_DIM = ML_WIDTH // ML_HEADS
ML_CHUNK = 128
GLA_WIDTH = D_MODEL - ML_WIDTH
GLA_HEADS = 4
GLA_DV = GLA_WIDTH // GLA_HEADS
GLA_KEY_WIDTH = GLA_WIDTH // 2
GLA_DK = GLA_KEY_WIDTH // GLA_HEADS
GLA_RANK = 16
GLA_TAU = 16.0
GLA_CHUNK = 64
CONV_W = 3
MIX_WIDTH = ML_WIDTH + GLA_WIDTH
PROJ_WIDTHS = (ML_WIDTH, ML_WIDTH, ML_WIDTH, ML_WIDTH, 4 * ML_HEADS,
               GLA_KEY_WIDTH, GLA_KEY_WIDTH, GLA_WIDTH, GLA_WIDTH, 2 * GLA_RANK)
P_TOTAL = sum(PROJ_WIDTHS)

PEER_HEADS = 8
PEER_NKEYS = 128
PEER_EXPERTS = PEER_NKEYS * PEER_NKEYS
PEER_TOPK = 16
PEER_QDIM = 256
PEER_BLOCK = 128

kernel_name = 'hymba_mlstm_gla_peer_dit_block'


def rms_norm(x, g):
    xf = x.astype(jnp.float32)
    y = xf * lax.rsqrt(jnp.mean(xf * xf, axis=-1, keepdims=True) + EPS)
    return y.astype(x.dtype) * g


def head_rms(h, g):
    y = h * lax.rsqrt(jnp.mean(h * h, axis=-1, keepdims=True) + EPS)
    return y.reshape(h.shape[:2] + (-1,)) * g


def short_conv(u, w, on_grid):
    bz, n, ch = u.shape
    if on_grid:
        rows = n // GRID_W
        u = u.reshape(bz * rows, GRID_W, ch)
    y = lax.conv_general_dilated(u, w[:, None, :].astype(u.dtype), (1,),
                                 [(CONV_W // 2, CONV_W // 2)],
                                 dimension_numbers=('NWC', 'WIO', 'NWC'),
                                 feature_group_count=ch)
    return y.reshape(bz, n, ch)


def to_chunks(t, size):
    bz, n, hh = t.shape[:3]
    t = t.reshape((bz, n // size, size, hh) + t.shape[3:])
    return jnp.moveaxis(t, 3, 1)


def from_chunks(t):
    bz, hh, nc, ll, d = t.shape
    return jnp.moveaxis(t, 1, 3).reshape(bz, nc * ll, hh, d)


def orient(t, direction):
    return jnp.flip(t, axis=1) if direction == 1 else t


def prep_mixers(z, conv_w, gate_b, lr_w2, alpha_b, on_grid):
    bz, n, _ = z.shape
    f32 = jnp.float32
    mq, mk, mv, mo, mg, gq, gk, gv, gr, glr = jnp.split(
        z, np.cumsum(PROJ_WIDTHS)[:-1].tolist(), axis=-1)
    mq, mk = jnp.split(jax.nn.silu(short_conv(jnp.concatenate([mq, mk], axis=-1), conv_w, on_grid)),
                       2, axis=-1)
    gates = (mg + gate_b).astype(f32).reshape(bz, n, 2, 2, ML_HEADS)
    alpha = jnp.einsum('bndr,drk->bndk', glr.reshape(bz, n, 2, GLA_RANK), lr_w2) + alpha_b
    return {
        'ml_q': mq.astype(f32).reshape(bz, n, ML_HEADS, ML_HEAD_DIM),
        'ml_k': (mk.astype(f32) * ML_HEAD_DIM ** -0.5).reshape(bz, n, ML_HEADS, ML_HEAD_DIM),
        'ml_v': mv.astype(f32).reshape(bz, n, ML_HEADS, ML_HEAD_DIM),
        'ml_o': mo,
        'ml_ig': gates[:, :, :, 0],
        'ml_lf': jax.nn.log_sigmoid(gates[:, :, :, 1]),
        'gla_q': (gq.astype(f32) * GLA_DK ** -0.5).reshape(bz, n, GLA_HEADS, GLA_DK),
        'gla_k': gk.astype(f32).reshape(bz, n, GLA_HEADS, GLA_DK),
        'gla_v': gv.astype(f32).reshape(bz, n, GLA_HEADS, GLA_DV),
        'gla_r': gr,
        'gla_la': (jax.nn.log_sigmoid(alpha.astype(f32)) / GLA_TAU).reshape(
            bz, n, 2, GLA_HEADS, GLA_DK),
    }


def mlstm_states(k, v, logf, ig, init):
    b = jnp.cumsum(logf, axis=-1)
    b_end = b[..., -1]
    w_log = b_end[..., None] - b + ig
    m_loc = jnp.max(w_log, axis=-1)
    w = jnp.exp(w_log - m_loc[..., None])
    c_loc = jnp.einsum('bhcld,bhcle->bhcde', w[..., None] * v, k)
    n_loc = jnp.einsum('bhcl,bhcle->bhce', w, k)

    def step(carry, inp):
        cm, nv, m = carry
        be, ml, cl, nl = inp
        m_new = jnp.maximum(be + m, ml)
        a = jnp.exp(be + m - m_new)
        bb = jnp.exp(ml - m_new)
        c_new = a[..., None, None] * cm + bb[..., None, None] * cl
        n_new = a[..., None] * nv + bb[..., None] * nl
        return (c_new, n_new, m_new), (cm, nv, m)

    xs = tuple(jnp.moveaxis(t, 2, 0) for t in (b_end, m_loc, c_loc, n_loc))
    final, starts = lax.scan(step, init, xs)
    starts = tuple(jnp.moveaxis(t, 0, 2) for t in starts)
    return starts, final


def mlstm_outputs(q, k, v, logf, ig, starts):
    c_s, n_s, m_s = starts
    ll = q.shape[-2]
    tri = jnp.tril(jnp.ones((ll, ll), dtype=bool))
    b = jnp.cumsum(logf, axis=-1)
    d_log = jnp.where(tri, b[..., :, None] - b[..., None, :] + ig[..., None, :], -jnp.inf)
    inter_log = b + m_s[..., None]
    m_t = jnp.maximum(inter_log, jnp.max(d_log, axis=-1))
    scores = jnp.einsum('bhctd,bhcsd->bhcts', q, k) * jnp.exp(d_log - m_t[..., None])
    inter = jnp.exp(inter_log - m_t)
    num = (jnp.einsum('bhcts,bhcsd->bhctd', scores, v)
           + inter[..., None] * jnp.einsum('bhcde,bhcte->bhctd', c_s, q))
    den = scores.sum(-1) + inter * jnp.einsum('bhce,bhcte->bhct', n_s, q)
    return num / jnp.maximum(jnp.abs(den), jnp.exp(-m_t))[..., None]


def mlstm_direction(lat, ctx, with_ctx_out):
    q, k, v, lf, ig = lat
    qc, kc, vc, lfc, igc = ctx
    bz, hh, _, _, d = q.shape
    init = (jnp.zeros((bz, hh, d, d), jnp.float32), jnp.zeros((bz, hh, d), jnp.float32),
            jnp.zeros((bz, hh), jnp.float32))
    ctx_starts, ctx_final = mlstm_states(kc, vc, lfc, igc, init)
    lat_starts, _ = mlstm_states(k, v, lf, ig, ctx_final)
    h = mlstm_outputs(q, k, v, lf, ig, lat_starts)
    hc = mlstm_outputs(qc, kc, vc, lfc, igc, ctx_starts) if with_ctx_out else None
    return h, hc


def gla_states(k, v, loga, init):
    b = jnp.cumsum(loga, axis=-2)
    b_end = b[..., -1, :]
    s_loc = jnp.einsum('bhcld,bhcle->bhcde', k * jnp.exp(b_end[..., None, :] - b), v)

    def step(s, inp):
        be, sl = inp
        return jnp.exp(be)[..., None] * s + sl, s

    final, starts = lax.scan(step, init, (jnp.moveaxis(b_end, 2, 0), jnp.moveaxis(s_loc, 2, 0)))
    return jnp.moveaxis(starts, 0, 2), final


def gla_outputs(q, k, v, loga, s_start):
    ll = q.shape[-2]
    tri = jnp.tril(jnp.ones((ll, ll), dtype=bool))
    b = jnp.cumsum(loga, axis=-2)
    ref = b[..., ll // 2, :][..., None, :]
    att = jnp.einsum('bhctd,bhcsd->bhcts', q * jnp.exp(b - ref), k * jnp.exp(ref - b))
    att = jnp.where(tri, att, 0.0)
    return (jnp.einsum('bhcts,bhcse->bhcte', att, v)
            + jnp.einsum('bhctd,bhcde->bhcte', q * jnp.exp(b), s_start))


def gla_direction(lat, ctx, with_ctx_out):
    q, k, v, la = lat
    qc, kc, vc, lac = ctx
    bz, hh = q.shape[:2]
    init = jnp.zeros((bz, hh, GLA_DK, GLA_DV), jnp.float32)
    ctx_starts, ctx_final = gla_states(kc, vc, lac, init)
    lat_starts, _ = gla_states(k, v, la, ctx_final)
    h = gla_outputs(q, k, v, la, lat_starts)
    hc = gla_outputs(qc, kc, vc, lac, ctx_starts) if with_ctx_out else None
    return h, hc


def ml_inputs(p, d):
    return [to_chunks(orient(t, d), ML_CHUNK)
            for t in (p['ml_q'], p['ml_k'], p['ml_v'], p['ml_lf'][:, :, d], p['ml_ig'][:, :, d])]


def gla_inputs(p, d):
    return [to_chunks(orient(t, d), GLA_CHUNK)
            for t in (p['gla_q'], p['gla_k'], p['gla_v'], p['gla_la'][:, :, d])]


def mix_out(ml_h, ml_o, gla_h, gla_r, ml_g, gla_g, w_out, dtype):
    ml = jax.nn.sigmoid(ml_o) * head_rms(ml_h, ml_g)
    gl = jax.nn.silu(gla_r) * head_rms(gla_h, gla_g)
    return (jnp.concatenate([ml, gl], axis=-1) @ w_out).astype(dtype)


def token_mixer(h_lat, h_ctx, w_in, conv_w, gate_b, lr_w2, alpha_b, ml_g, gla_g, w_out,
                with_ctx_out):
    lat = prep_mixers(h_lat @ w_in, conv_w, gate_b, lr_w2, alpha_b, True)
    ctx = prep_mixers(h_ctx @ w_in, conv_w, gate_b, lr_w2, alpha_b, False)
    ml_lat, ml_ctx, gla_lat, gla_ctx = [], [], [], []
    for d in range(2):
        h, hc = mlstm_direction(ml_inputs(lat, d), ml_inputs(ctx, d), with_ctx_out)
        ml_lat.append(orient(from_chunks(h), d))
        g, gc = gla_direction(gla_inputs(lat, d), gla_inputs(ctx, d), with_ctx_out)
        gla_lat.append(orient(from_chunks(g), d))
        if with_ctx_out:
            ml_ctx.append(orient(from_chunks(hc), d))
            gla_ctx.append(orient(from_chunks(gc), d))
    y = mix_out(ml_lat[0] + ml_lat[1], lat['ml_o'], gla_lat[0] + gla_lat[1], lat['gla_r'],
                ml_g, gla_g, w_out, h_lat.dtype)
    yc = None
    if with_ctx_out:
        yc = mix_out(ml_ctx[0] + ml_ctx[1], ctx['ml_o'], gla_ctx[0] + gla_ctx[1], ctx['gla_r'],
                     ml_g, gla_g, w_out, h_ctx.dtype)
    return y, yc


def peer_ffn(h, wq, keys, u_tab, v_tab):
    bz, n, d = h.shape
    blocks = h.reshape(bz * n // PEER_BLOCK, PEER_BLOCK, d)

    def retrieve(xb):
        q = (xb @ wq).reshape(PEER_BLOCK, PEER_HEADS, 2, PEER_QDIM // 2)
        s = jnp.einsum('tphq,phkq->tphk', q, keys)
        top_s, top_i = lax.top_k(s, PEER_TOPK)
        cand = top_s[:, :, 0, :, None] + top_s[:, :, 1, None, :]
        best_s, best_c = lax.top_k(cand.reshape(PEER_BLOCK, PEER_HEADS, PEER_TOPK * PEER_TOPK),
                                   PEER_TOPK)
        i1 = jnp.take_along_axis(top_i[:, :, 0], best_c // PEER_TOPK, axis=-1)
        i2 = jnp.take_along_axis(top_i[:, :, 1], best_c % PEER_TOPK, axis=-1)
        idx = i1 * PEER_NKEYS + i2
        g = jax.nn.softmax(best_s.astype(jnp.float32), axis=-1).astype(xb.dtype)
        act = jax.nn.gelu(jnp.einsum('tpkd,td->tpk', u_tab[idx], xb), approximate=False)
        return jnp.einsum('tpk,tpkd->td', g * act, v_tab[idx])

    return lax.map(retrieve, blocks).reshape(bz, n, d)


def setup_inputs(seed: int = 0) -> dict:
    key = jax.random.key(seed)
    ks = jax.random.split(key, 24)

    def nrm(k, shape, scale):
        return jax.random.normal(k, shape, jnp.float32) * scale

    f_bias = jnp.linspace(3.0, 6.0, ML_HEADS, dtype=jnp.float32)
    i_part = nrm(ks[8], (DEPTH, 2, 1, ML_HEADS), 0.1)
    f_part = f_bias + nrm(ks[9], (DEPTH, 2, 1, ML_HEADS), 0.1)
    ml_gate_b = jnp.concatenate([i_part, f_part], axis=2).reshape(DEPTH, 4 * ML_HEADS)
    return {
        'x': nrm(ks[0], (BATCH, SEQ, D_MODEL), 1.0),
        'c': nrm(ks[1], (BATCH, D_MODEL), 1.0),
        'ctx': nrm(ks[2], (BATCH, CTX_LEN, D_MODEL), 1.0),
        'c_ctx': nrm(ks[3], (D_MODEL,), 1.0),
        'w_mod': nrm(ks[4], (DEPTH, D_MODEL, 6 * D_MODEL), 0.5 * D_MODEL ** -0.5),
        'b_mod': nrm(ks[5], (DEPTH, 6 * D_MODEL), 0.02),
        'norm1_g': 1.0 + nrm(ks[6], (DEPTH, D_MODEL), 0.02),
        'w_in': nrm(ks[7], (DEPTH, D_MODEL, P_TOTAL), D_MODEL ** -0.5),
        'ml_conv_w': nrm(ks[10], (DEPTH, CONV_W, 2 * ML_WIDTH), CONV_W ** -0.5),
        'ml_gate_b': ml_gate_b,
        'ml_norm_g': 1.0 + nrm(ks[11], (DEPTH, ML_WIDTH), 0.02),
        'gla_lr_w2': nrm(ks[12], (DEPTH, 2, GLA_RANK, GLA_KEY_WIDTH), GLA_RANK ** -0.5),
        'gla_alpha_b': nrm(ks[13], (DEPTH, 2, GLA_KEY_WIDTH), 0.1),
        'gla_norm_g': 1.0 + nrm(ks[14], (DEPTH, GLA_WIDTH), 0.02),
        'w_out': nrm(ks[15], (DEPTH, MIX_WIDTH, D_MODEL), MIX_WIDTH ** -0.5),
        'norm2_g': 1.0 + nrm(ks[16], (DEPTH, D_MODEL), 0.02),
        'peer_wq': nrm(ks[17], (DEPTH, D_MODEL, PEER_HEADS * PEER_QDIM), D_MODEL ** -0.5),
        'peer_keys': nrm(ks[18], (DEPTH, PEER_HEADS, 2, PEER_NKEYS, PEER_QDIM // 2),
                         (PEER_QDIM // 2) ** -0.5),
        'peer_u': nrm(ks[19], (DEPTH, PEER_EXPERTS, D_MODEL), D_MODEL ** -0.5),
        'peer_v': nrm(ks[20], (DEPTH, PEER_EXPERTS, D_MODEL), PEER_HEADS ** -0.5),
        'norm_f_g': 1.0 + nrm(ks[21], (D_MODEL,), 0.02),
    }


def reference(x, c, ctx, c_ctx, w_mod, b_mod, norm1_g, w_in, ml_conv_w, ml_gate_b, ml_norm_g,
              gla_lr_w2, gla_alpha_b, gla_norm_g, w_out, norm2_g, peer_wq, peer_keys, peer_u,
              peer_v, norm_f_g):
    cond = jax.nn.silu(c)
    cond_ctx = jax.nn.silu(c_ctx)
    for layer in range(DEPTH):
        update_ctx = layer + 1 < DEPTH
        mod = (cond @ w_mod[layer] + b_mod[layer])[:, None, :]
        mod_c = cond_ctx @ w_mod[layer] + b_mod[layer]
        sh1, sc1, g1, sh2, sc2, g2 = jnp.split(mod, 6, axis=-1)
        csh1, csc1, cg1, csh2, csc2, cg2 = jnp.split(mod_c, 6, axis=-1)

        h = rms_norm(x, norm1_g[layer]) * (1.0 + sc1) + sh1
        hc = rms_norm(ctx, norm1_g[layer]) * (1.0 + csc1) + csh1
        y, yc = token_mixer(h, hc, w_in[layer], ml_conv_w[layer], ml_gate_b[layer],
                            gla_lr_w2[layer], gla_alpha_b[layer], ml_norm_g[layer],
                            gla_norm_g[layer], w_out[layer], update_ctx)
        x = x + g1 * y
        h2 = rms_norm(x, norm2_g[layer]) * (1.0 + sc2) + sh2
        x = x + g2 * peer_ffn(h2, peer_wq[layer], peer_keys[layer], peer_u[layer], peer_v[layer])

        if update_ctx:
            ctx = ctx + cg1 * yc
            hc2 = rms_norm(ctx, norm2_g[layer]) * (1.0 + csc2) + csh2
            ctx = ctx + cg2 * peer_ffn(hc2, peer_wq[layer], peer_keys[layer], peer_u[layer],
                                       peer_v[layer])
    return rms_norm(x, norm_f_g)
```

```python
import functools

import jax
import jax.numpy as jnp
from jax import lax
from jax.experimental import pallas as pl
from jax.experimental.pallas import tpu as pltpu

F32 = jnp.float32
BF16 = jnp.bfloat16

EPS = 1e-6
D_MODEL = 1024
GRID_W = 64

ML_HEADS = 4
ML_HEAD_DIM = 128
ML_WIDTH = ML_HEADS * ML_HEAD_DIM
ML_CHUNK = 128
GLA_HEADS = 4
GLA_DK = 64
GLA_DV = 128
GLA_KEY_WIDTH = GLA_HEADS * GLA_DK
GLA_WIDTH = GLA_HEADS * GLA_DV
GLA_RANK = 16
GLA_TAU = 16.0
GLA_CHUNK = 64
N_GATES = 4 * ML_HEADS

PEER_HEADS = 8
PEER_NKEYS = 128
PEER_TOPK = 16
PEER_HALF = 128

LANES = 128
SUBLANES = 8
VMEM_LIMIT = 56 * 1024 * 1024

INPROJ_TM = 256
MIX_TM = 256
PEER_TM = 512
PEER_TE = 1024

NEG_INF = float("-inf")


def _params(*sem):
    return pltpu.CompilerParams(dimension_semantics=sem, vmem_limit_bytes=VMEM_LIMIT)


def _dot(a, b):
    return jnp.dot(a, b, preferred_element_type=F32)


def _dot_nt(a, b):
    return lax.dot_general(a, b, (((1,), (1,)), ((), ())), preferred_element_type=F32)


def _dot_tn(a, b):
    return lax.dot_general(a, b, (((0,), (0,)), ((), ())), preferred_element_type=F32)


def _split3(x):
    hi = x.astype(BF16)
    r1 = x - hi.astype(F32)
    mid = r1.astype(BF16)
    lo = (r1 - mid.astype(F32)).astype(BF16)
    return hi, mid, lo


def _dot_exact_rhs(a01, x):
    hi, mid, lo = _split3(x)
    return _dot(a01, hi) + _dot(a01, mid) + _dot(a01, lo)


def _dot_exact_lhs(x, a01):
    hi, mid, lo = _split3(x)
    return _dot(hi, a01) + _dot(mid, a01) + _dot(lo, a01)


def _sigmoid(x):
    return 1.0 / (1.0 + jnp.exp(-x))


def _log_sigmoid(x):
    return jnp.minimum(x, 0.0) - jnp.log(1.0 + jnp.exp(-jnp.abs(x)))


def _rms(x):
    return x * lax.rsqrt(jnp.mean(x * x, axis=-1, keepdims=True) + EPS)


def _mod_kernel(c_ref, w_ref, b_ref, o_ref):
    cond = c_ref[...]
    cond = cond * _sigmoid(cond)
    ch, cl = cond.astype(BF16), (cond - cond.astype(BF16).astype(F32)).astype(BF16)
    w = w_ref[...]
    wh = w.astype(BF16)
    wl = (w - wh.astype(F32)).astype(BF16)
    o_ref[...] = _dot(ch, wh) + _dot(ch, wl) + _dot(cl, wh) + b_ref[...]


def _modulation(c_all, w_mod, b_mod):
    rows, d = c_all.shape
    n_out = w_mod.shape[1]
    tn = 512
    return pl.pallas_call(
        _mod_kernel,
        grid=(n_out // tn,),
        in_specs=[pl.BlockSpec((rows, d), lambda j: (0, 0)),
                  pl.BlockSpec((d, tn), lambda j: (0, j)),
                  pl.BlockSpec((1, tn), lambda j: (0, j))],
        out_specs=pl.BlockSpec((rows, tn), lambda j: (0, j)),
        out_shape=jax.ShapeDtypeStruct((rows, n_out), F32),
        compiler_params=_params("arbitrary"),
        name="mod",
    )(c_all, w_mod, b_mod)


def _chunk_masks(tm, chunk):
    r = lax.broadcasted_iota(jnp.int32, (tm, tm), 0)
    c = lax.broadcasted_iota(jnp.int32, (tm, tm), 1)
    same = (r // chunk) == (c // chunk)
    prefix = jnp.where(same & (c <= r), 1.0, 0.0).astype(BF16)
    suffix = jnp.where(same & (c >= r), 1.0, 0.0).astype(BF16)
    return prefix, suffix


def _inproj_kernel(x_ref, sh_ref, sc_ref, ng_ref, wqk_ref, wv_ref, wo_ref, wg_ref, wgt_ref,
                   wgqk_ref, wgv_ref, wgr_ref, wlr_ref, conv_ref, gb_ref, gbc_ref, w2_ref, ab_ref,
                   mq_ref, mk_ref, mv_ref, mo_ref, gtok_ref, gmaj_ref,
                   gq_ref, gk_ref, gv_ref, gr_ref, bg_ref, *, row_len):
    tm = x_ref.shape[1]
    x = x_ref[0]
    h = _rms(x) * ng_ref[...] * (1.0 + sc_ref[0]) + sh_ref[0]
    hb = h.astype(BF16)

    u = _dot(hb, wqk_ref[...])
    pos = lax.broadcasted_iota(jnp.int32, (tm, 1), 0) % row_len
    up = jnp.where(pos == 0, 0.0, pltpu.roll(u, 1, 0))
    dn = jnp.where(pos == row_len - 1, 0.0, pltpu.roll(u, tm - 1, 0))
    cw = conv_ref[...]
    y = cw[0:1] * up + cw[1:2] * u + cw[2:3] * dn
    y = y * _sigmoid(y)
    mq_ref[0] = y[:, :ML_WIDTH].astype(BF16)
    mk_ref[0] = (y[:, ML_WIDTH:] * ML_HEAD_DIM ** -0.5).astype(BF16)
    mv_ref[0] = _dot(hb, wv_ref[...]).astype(BF16)
    mo_ref[0] = _dot(hb, wo_ref[...]).astype(BF16)

    pre128, suf128 = _chunk_masks(tm, ML_CHUNK)
    g = _dot(hb, wg_ref[...]) + gb_ref[...]
    col = lax.broadcasted_iota(jnp.int32, (1, LANES), 1)
    is_f = ((col % 8) >= 4) & (col < N_GATES)
    is_bwd = col >= 8
    lf = jnp.where(is_f, _log_sigmoid(g), 0.0)
    cum = jnp.where(is_bwd, _dot_exact_rhs(suf128, lf), _dot_exact_rhs(pre128, lf))
    gtok_ref[0] = jnp.where(is_f, cum, g)[:, :N_GATES]

    gt = _dot_nt(wgt_ref[...], hb) + gbc_ref[...]
    row = lax.broadcasted_iota(jnp.int32, (N_GATES, 1), 0)
    is_f_r = (row % 8) >= 4
    lft = jnp.where(is_f_r, _log_sigmoid(gt), 0.0)
    cum_t = jnp.where(row >= 8, _dot_exact_lhs(lft, pre128), _dot_exact_lhs(lft, suf128))
    gmaj_ref[0] = jnp.where(is_f_r, cum_t, gt)

    gqk = _dot(hb, wgqk_ref[...])
    gq_ref[0] = (gqk[:, :GLA_KEY_WIDTH] * GLA_DK ** -0.5).astype(BF16)
    gk_ref[0] = gqk[:, GLA_KEY_WIDTH:].astype(BF16)
    gv_ref[0] = _dot(hb, wgv_ref[...]).astype(BF16)
    gr_ref[0] = _dot(hb, wgr_ref[...]).astype(BF16)
    glr = _dot(hb, wlr_ref[...]).astype(BF16)
    alpha = _dot(glr, w2_ref[...]) + ab_ref[...]
    la = _log_sigmoid(alpha) * (1.0 / GLA_TAU)
    pre64, suf64 = _chunk_masks(tm, GLA_CHUNK)
    bg_ref[0, :, :GLA_KEY_WIDTH] = _dot_exact_rhs(pre64, la[:, :GLA_KEY_WIDTH])
    bg_ref[0, :, GLA_KEY_WIDTH:] = _dot_exact_rhs(suf64, la[:, GLA_KEY_WIDTH:])


def _inproj(x, sh, sc, ng, wts, row_len):
    bz, n, d = x.shape
    tm = min(INPROJ_TM, n)
    assert n % tm == 0 and tm % row_len == 0 and tm % ML_CHUNK == 0
    grid = (bz, n // tm)
    per_b = (lambda b, i: (b, 0, 0)) if sh.shape[0] == bz else (lambda b, i: (0, 0, 0))

    def full(a):
        return pl.BlockSpec(a.shape, lambda b, i, nd=a.ndim: (0,) * nd)

    def tok(width):
        return pl.BlockSpec((1, tm, width), lambda b, i: (b, i, 0))

    def out(width, dtype):
        return jax.ShapeDtypeStruct((bz, n, width), dtype)

    in_specs = [tok(d), pl.BlockSpec((1, 1, d), per_b), pl.BlockSpec((1, 1, d), per_b), full(ng)]
    in_specs += [full(w) for w in wts]
    out_specs = [tok(ML_WIDTH)] * 4 + [tok(N_GATES), pl.BlockSpec((1, N_GATES, tm), lambda b, i: (b, 0, i))]
    out_specs += [tok(GLA_KEY_WIDTH)] * 2 + [tok(GLA_WIDTH)] * 2 + [tok(2 * GLA_KEY_WIDTH)]
    out_shape = [out(ML_WIDTH, BF16)] * 4 + [out(N_GATES, F32),
                                              jax.ShapeDtypeStruct((bz, N_GATES, n), F32)]
    out_shape += [out(GLA_KEY_WIDTH, BF16)] * 2 + [out(GLA_WIDTH, BF16)] * 2 + [out(2 * GLA_KEY_WIDTH, F32)]
    return pl.pallas_call(
        functools.partial(_inproj_kernel, row_len=row_len),
        grid=grid, in_specs=in_specs, out_specs=out_specs, out_shape=out_shape,
        compiler_params=_params("parallel", "arbitrary"),
        name="inproj",
    )(x, sh, sc, ng, *wts)


def _ml_chunk(q, k, v, gt, gm, d, mask, state, want_out):
    ct, nv, m = state
    ll = q.shape[0]
    ig_col, b_col = gt[:, 2 * d:2 * d + 1], gt[:, 2 * d + 1:2 * d + 2]
    ig_row, b_row = gm[2 * d:2 * d + 1, :], gm[2 * d + 1:2 * d + 2, :]
    b_end = b_row[:, ll - 1:ll] if d == 0 else b_row[:, 0:1]
    h = None
    if want_out:
        d_log = jnp.where(mask, b_col - b_row + ig_row, NEG_INF)
        inter_log = b_col + m
        m_t = jnp.maximum(inter_log, jnp.max(d_log, axis=1, keepdims=True))
        scores = _dot_nt(q, k) * jnp.exp(d_log - m_t)
        inter = jnp.exp(inter_log - m_t)
        qf = q.astype(F32)
        num = _dot(scores.astype(BF16), v) + inter * _dot(q, ct.astype(BF16))
        den = jnp.sum(scores, axis=1, keepdims=True) + inter * jnp.sum(qf * nv, axis=1, keepdims=True)
        h = num / jnp.maximum(jnp.abs(den), jnp.exp(-m_t))
    m_loc = jnp.max(b_end - b_row + ig_row, axis=1, keepdims=True)
    w_col = jnp.exp(b_end - b_col + ig_col - m_loc)
    c_loc = _dot_tn(k, (w_col * v.astype(F32)).astype(BF16))
    n_loc = jnp.sum(w_col * k.astype(F32), axis=0, keepdims=True)
    m_new = jnp.maximum(b_end + m, m_loc)
    a = jnp.exp(b_end + m - m_new)
    bb = jnp.exp(m_loc - m_new)
    return h, (a * ct + bb * c_loc, a * nv + bb * n_loc, m_new)


def _mlstm_kernel(q_ref, k_ref, v_ref, gt_ref, gm_ref, qc_ref, kc_ref, vc_ref, gtc_ref, gmc_ref,
                  hf_ref, hb_ref):
    ll = ML_CHUNK
    nc = q_ref.shape[1] // ll
    ncc = qc_ref.shape[1] // ll
    r = lax.broadcasted_iota(jnp.int32, (ll, ll), 0)
    c = lax.broadcasted_iota(jnp.int32, (ll, ll), 1)
    masks = (c <= r, c >= r)

    def load(refs, ci):
        qr, kr, vr, gtr, gmr = refs
        rows = pl.ds(pl.multiple_of(ci * ll, ll), ll)
        return qr[0, rows, :], kr[0, rows, :], vr[0, rows, :], gtr[0, 0, rows, :], gmr[0, 0, ci]

    zero = (jnp.zeros((ML_HEAD_DIM, ML_HEAD_DIM), F32), jnp.zeros((1, ML_HEAD_DIM), F32),
            jnp.zeros((1, 1), F32))
    ctx_refs = (qc_ref, kc_ref, vc_ref, gtc_ref, gmc_ref)
    lat_refs = (q_ref, k_ref, v_ref, gt_ref, gm_ref)

    def ctx_body(i, carry):
        sf, sb = carry
        _, sf = _ml_chunk(*load(ctx_refs, i), 0, masks[0], sf, False)
        _, sb = _ml_chunk(*load(ctx_refs, ncc - 1 - i), 1, masks[1], sb, False)
        return sf, sb

    def lat_body(i, carry):
        sf, sb = carry
        cb = nc - 1 - i
        h, sf = _ml_chunk(*load(lat_refs, i), 0, masks[0], sf, True)
        hf_ref[0, pl.ds(pl.multiple_of(i * ll, ll), ll), :] = h
        h, sb = _ml_chunk(*load(lat_refs, cb), 1, masks[1], sb, True)
        hb_ref[0, pl.ds(pl.multiple_of(cb * ll, ll), ll), :] = h
        return sf, sb

    carry = lax.fori_loop(0, ncc, ctx_body, (zero, zero))
    lax.fori_loop(0, nc, lat_body, carry)


def _mlstm(q, k, v, gt, gm, qc, kc, vc, gtc, gmc):
    bz, n, _ = q.shape
    ncx = qc.shape[1]

    def seq(nn):
        return pl.BlockSpec((1, nn, ML_HEAD_DIM), lambda b, h: (b, 0, h))

    def gtok(nn):
        return pl.BlockSpec((1, 1, nn, 4), lambda b, h: (b, h, 0, 0))

    def gmaj(nn):
        return pl.BlockSpec((1, 1, nn // ML_CHUNK, 4, ML_CHUNK), lambda b, h: (b, h, 0, 0, 0))

    return pl.pallas_call(
        _mlstm_kernel,
        grid=(bz, ML_HEADS),
        in_specs=[seq(n)] * 3 + [gtok(n), gmaj(n)] + [seq(ncx)] * 3 + [gtok(ncx), gmaj(ncx)],
        out_specs=[seq(n)] * 2,
        out_shape=[jax.ShapeDtypeStruct((bz, n, ML_WIDTH), F32)] * 2,
        compiler_params=_params("parallel", "arbitrary"),
        name="mlstm",
    )(q, k, v, gt, gm, qc, kc, vc, gtc, gmc)


def _gla_chunk(q, k, v, b, d, mask, st, want_out):
    ll = q.shape[0]
    qf, kf = q.astype(F32), k.astype(F32)
    mid = ll // 2 if d == 0 else ll // 2 - 1
    ref = b[mid:mid + 1]
    b_end = b[ll - 1:ll] if d == 0 else b[0:1]
    o = None
    if want_out:
        att = _dot_nt((qf * jnp.exp(b - ref)).astype(BF16), (kf * jnp.exp(ref - b)).astype(BF16))
        att = jnp.where(mask, att, 0.0)
        o = _dot(att.astype(BF16), v) + _dot_nt((qf * jnp.exp(b)).astype(BF16), st.astype(BF16))
    s_loc = _dot_tn(v, (kf * jnp.exp(b_end - b)).astype(BF16))
    return o, jnp.exp(b_end) * st + s_loc


def _gla_kernel(q_ref, k_ref, v_ref, bf_ref, bb_ref, qc_ref, kc_ref, vc_ref, bfc_ref, bbc_ref,
                of_ref, ob_ref):
    ll = GLA_CHUNK
    nc = q_ref.shape[2] // ll
    ncc = qc_ref.shape[2] // ll
    r = lax.broadcasted_iota(jnp.int32, (ll, ll), 0)
    c = lax.broadcasted_iota(jnp.int32, (ll, ll), 1)
    masks = (c <= r, c >= r)

    def load(refs, ci, d):
        qr, kr, vr, bfr, bbr = refs
        rows = pl.ds(pl.multiple_of(ci * ll, ll), ll)
        br = bfr if d == 0 else bbr
        return qr[0, 0, rows, :], kr[0, 0, rows, :], vr[0, rows, :], br[0, 0, rows, :]

    zero = jnp.zeros((GLA_DV, GLA_DK), F32)
    ctx_refs = (qc_ref, kc_ref, vc_ref, bfc_ref, bbc_ref)
    lat_refs = (q_ref, k_ref, v_ref, bf_ref, bb_ref)

    def ctx_body(i, carry):
        sf, sb = carry
        _, sf = _gla_chunk(*load(ctx_refs, i, 0), 0, masks[0], sf, False)
        _, sb = _gla_chunk(*load(ctx_refs, ncc - 1 - i, 1), 1, masks[1], sb, False)
        return sf, sb

    def lat_body(i, carry):
        sf, sb = carry
        cb = nc - 1 - i
        o, sf = _gla_chunk(*load(lat_refs, i, 0), 0, masks[0], sf, True)
        of_ref[0, pl.ds(pl.multiple_of(i * ll, ll), ll), :] = o
        o, sb = _gla_chunk(*load(lat_refs, cb, 1), 1, masks[1], sb, True)
        ob_ref[0, pl.ds(pl.multiple_of(cb * ll, ll), ll), :] = o
        return sf, sb

    carry = lax.fori_loop(0, ncc, ctx_body, (zero, zero))
    lax.fori_loop(0, nc, lat_body, carry)


def _gla(q, k, v, bf, bb, qc, kc, vc, bfc, bbc):
    bz, _, n, _ = q.shape
    ncx = qc.shape[2]

    def key(nn):
        return pl.BlockSpec((1, 1, nn, GLA_DK), lambda b, h: (b, h, 0, 0))

    def val(nn):
        return pl.BlockSpec((1, nn, GLA_DV), lambda b, h: (b, 0, h))

    return pl.pallas_call(
        _gla_kernel,
        grid=(bz, GLA_HEADS),
        in_specs=[key(n), key(n), val(n), key(n), key(n), key(ncx), key(ncx), val(ncx), key(ncx), key(ncx)],
        out_specs=[val(n)] * 2,
        out_shape=[jax.ShapeDtypeStruct((bz, n, GLA_WIDTH), F32)] * 2,
        compiler_params=_params("parallel", "arbitrary"),
        name="gla",
    )(q, k, v, bf, bb, qc, kc, vc, bfc, bbc)


def _head_rms(h, g, width):
    parts = [_rms(h[:, i:i + width]) for i in range(0, h.shape[1], width)]
    return jnp.concatenate(parts, axis=1) * g


def _top16(s):
    nk, tm = s.shape
    key_id = lax.broadcasted_iota(jnp.int32, (nk, tm), 0).astype(F32)
    slot = lax.broadcasted_iota(jnp.int32, (PEER_TOPK, tm), 0)

    def body(r, carry):
        s, rank, vals = carry
        m = jnp.max(s, axis=0, keepdims=True)
        first = jnp.min(jnp.where(s == m, key_id, float(nk)), axis=0, keepdims=True)
        sel = key_id == first
        rank = jnp.where(sel, r.astype(F32), rank)
        s = jnp.where(sel, NEG_INF, s)
        vals = jnp.where(slot == r, m, vals)
        return s, rank, vals

    init = (s, jnp.full((nk, tm), float(PEER_TOPK), F32), jnp.zeros((PEER_TOPK, tm), F32))
    _, rank, vals = lax.fori_loop(0, PEER_TOPK, body, init)
    return vals, rank


def _pair_counts(v1, v2):
    tm = v1.shape[1]
    blocks = [v1[0:1] + v2[0:8], v1[0:1] + v2[8:16]]
    blocks += [v1[a:a + 1] + v2[0:8] for a in range(1, 8)]
    blocks += [v1[8:16] + v2[0:1]]
    cand0 = jnp.concatenate(blocks, axis=0)
    nrow = cand0.shape[0]
    i = lax.broadcasted_iota(jnp.int32, (nrow, tm), 0)
    blk, rr = i // 8, i % 8
    flat = jnp.where(blk == 0, rr, jnp.where(blk == 1, 8 + rr,
                     jnp.where(blk <= 8, (blk - 1) * 16 + rr, (8 + rr) * 16))).astype(F32)

    def body(_, carry):
        cand, picked = carry
        m = jnp.max(cand, axis=0, keepdims=True)
        first = jnp.min(jnp.where(cand == m, flat, 1e9), axis=0, keepdims=True)
        sel = flat == first
        return jnp.where(sel, NEG_INF, cand), jnp.where(sel, 1.0, picked)

    _, picked = lax.fori_loop(0, PEER_TOPK, body, (cand0, jnp.zeros((nrow, tm), F32)))
    z = jnp.sum(picked * jnp.exp(cand0 - cand0[0:1]), axis=0, keepdims=True)
    n_rows = [jnp.sum(picked[0:16], axis=0, keepdims=True)]
    n_rows += [jnp.sum(picked[8 * (a + 1):8 * (a + 2)], axis=0, keepdims=True) for a in range(1, 8)]
    n = jnp.concatenate(n_rows + [picked[72:80]], axis=0)
    return n, z


def _mix_kernel(x_ref, mhf_ref, mhb_ref, ghf_ref, ghb_ref, mo_ref, gr_ref, mlg_ref, glag_ref,
                wout_ref, g1_ref, sh2_ref, sc2_ref, n2g_ref, wq_ref, keys_ref,
                x1_ref, h2t_ref, e1_ref, l_ref, e2_ref, rk_ref, q_scr):
    ml = _head_rms(mhf_ref[0] + mhb_ref[0], mlg_ref[...], ML_HEAD_DIM)
    ml = _sigmoid(mo_ref[0].astype(F32)) * ml
    gl = _head_rms(ghf_ref[0] + ghb_ref[0], glag_ref[...], GLA_DV)
    gr = gr_ref[0].astype(F32)
    gl = gr * _sigmoid(gr) * gl
    mix = jnp.concatenate([ml, gl], axis=1).astype(BF16)
    x1 = x_ref[0] + g1_ref[0] * _dot(mix, wout_ref[...])
    x1_ref[0] = x1
    h2 = _rms(x1) * n2g_ref[...] * (1.0 + sc2_ref[0]) + sh2_ref[0]
    h2t_ref[...] = h2.T.astype(BF16)
    qall = _dot(h2.astype(BF16), wq_ref[...]).astype(BF16)
    for j in range(2 * PEER_HEADS):
        q_scr[j] = qall[:, j * PEER_HALF:(j + 1) * PEER_HALF]

    def head_body(p, carry):
        s1 = _dot_nt(keys_ref[2 * p], q_scr[2 * p])
        s2 = _dot_nt(keys_ref[2 * p + 1], q_scr[2 * p + 1])
        v1, rank1 = _top16(s1)
        v2, rank2 = _top16(s2)
        n, z = _pair_counts(v1, v2)
        lim = jnp.zeros_like(rank1)
        for a in range(PEER_TOPK):
            lim = jnp.where(rank1 == float(a), n[a:a + 1], lim)
        e1_ref[p] = jnp.exp(s1 - v1[0:1]) / z
        l_ref[p] = lim
        e2_ref[p] = jnp.exp(s2 - v2[0:1])
        rk_ref[p] = rank2
        return carry

    lax.fori_loop(0, PEER_HEADS, head_body, 0)


def _mix(x, mhf, mhb, ghf, ghb, mo, gr, mlg, glag, wout, g1, sh2, sc2, n2g, wq, keys):
    bz, n, d = x.shape
    tm = min(MIX_TM, n)
    assert n % tm == 0 and tm % LANES == 0
    nt = n // tm
    t_all = bz * n

    def tok(width):
        return pl.BlockSpec((1, tm, width), lambda b, i: (b, i, 0))

    def full(a):
        return pl.BlockSpec(a.shape, lambda b, i, nd=a.ndim: (0,) * nd)

    vec = pl.BlockSpec((1, 1, d), lambda b, i: (b, 0, 0))
    tab = pl.BlockSpec((PEER_HEADS, PEER_NKEYS, tm), lambda b, i: (0, 0, b * nt + i))
    tab_shape = jax.ShapeDtypeStruct((PEER_HEADS, PEER_NKEYS, t_all), F32)
    return pl.pallas_call(
        _mix_kernel,
        grid=(bz, nt),
        in_specs=[tok(d)] + [tok(ML_WIDTH)] * 6 + [full(mlg), full(glag), full(wout), vec, vec, vec,
                                                    full(n2g), full(wq), full(keys)],
        out_specs=[tok(d), pl.BlockSpec((d, tm), lambda b, i: (0, b * nt + i)), tab, tab, tab, tab],
        out_shape=[jax.ShapeDtypeStruct((bz, n, d), F32), jax.ShapeDtypeStruct((d, t_all), BF16),
                   tab_shape, tab_shape, tab_shape, tab_shape],
        scratch_shapes=[pltpu.VMEM((2 * PEER_HEADS, tm, PEER_HALF), BF16)],
        compiler_params=_params("parallel", "arbitrary"),
        name="mix",
    )(x, mhf, mhb, ghf, ghb, mo, gr, mlg, glag, wout, g1, sh2, sc2, n2g, wq, keys)


def _peer_kernel(h2t_ref, u_ref, vt_ref, e1_ref, l_ref, e2_ref, rk_ref, x1_ref, g2_ref, nfg_ref,
                 out_ref, acc_ref):
    j = pl.program_id(2)
    rows_per_step = u_ref.shape[0] // PEER_NKEYS

    @pl.when(j == 0)
    def _():
        acc_ref[...] = jnp.zeros_like(acc_ref)

    act = _dot(u_ref[...], h2t_ref[...])
    act = 0.5 * act * (1.0 + lax.erf(act * 2.0 ** -0.5))
    parts = []
    for i in range(rows_per_step):
        key1 = pl.ds(j * rows_per_step + i, 1)
        w = None
        for p in range(PEER_HEADS):
            term = jnp.where(rk_ref[p] < l_ref[p, key1, :], e1_ref[p, key1, :] * e2_ref[p], 0.0)
            w = term if w is None else w + term
        parts.append((w * act[i * PEER_NKEYS:(i + 1) * PEER_NKEYS]).astype(BF16))
    acc_ref[...] += _dot(vt_ref[...], jnp.concatenate(parts, axis=0))

    @pl.when(j == pl.num_programs(2) - 1)
    def _():
        xf = x1_ref[0] + g2_ref[0] * acc_ref[...].T
        out_ref[0] = _rms(xf) * nfg_ref[...]


def _peer(h2t, u, vt, e1, lim, e2, rk, x1, g2, nfg):
    bz, n, d = x1.shape
    n_exp = u.shape[0]
    tm = min(PEER_TM, n)
    te = PEER_TE
    assert n % tm == 0 and n_exp % te == 0 and te % PEER_NKEYS == 0
    nt = n // tm
    tab = pl.BlockSpec((PEER_HEADS, PEER_NKEYS, tm), lambda b, i, j: (0, 0, b * nt + i))
    tok = pl.BlockSpec((1, tm, d), lambda b, i, j: (b, i, 0))
    return pl.pallas_call(
        _peer_kernel,
        grid=(bz, nt, n_exp // te),
        in_specs=[pl.BlockSpec((d, tm), lambda b, i, j: (0, b * nt + i)),
                  pl.BlockSpec((te, d), lambda b, i, j: (j, 0)),
                  pl.BlockSpec((d, te), lambda b, i, j: (0, j)),
                  tab, tab, tab, tab, tok,
                  pl.BlockSpec((1, 1, d), lambda b, i, j: (b, 0, 0)),
                  pl.BlockSpec((1, d), lambda b, i, j: (0, 0))],
        out_specs=tok,
        out_shape=jax.ShapeDtypeStruct((bz, n, d), F32),
        scratch_shapes=[pltpu.VMEM((d, tm), F32)],
        compiler_params=_params("parallel", "parallel", "arbitrary"),
        name="peer",
    )(h2t, u, vt, e1, lim, e2, rk, x1, g2, nfg)


def _inproj_weights(w_in, conv_w, gate_b, lr_w2, alpha_b):
    widths = (ML_WIDTH, ML_WIDTH, ML_WIDTH, ML_WIDTH, N_GATES,
              GLA_KEY_WIDTH, GLA_KEY_WIDTH, GLA_WIDTH, GLA_WIDTH, 2 * GLA_RANK)
    offs = [0]
    for w in widths:
        offs.append(offs[-1] + w)
    wb = w_in.astype(BF16)
    col = lambda a, b: wb[:, offs[a]:offs[b]]
    pad_lanes = lambda a: jnp.pad(a, ((0, 0), (0, LANES - a.shape[1])))
    w2 = jnp.zeros((LANES, 2 * GLA_KEY_WIDTH), F32)
    w2 = w2.at[:GLA_RANK, :GLA_KEY_WIDTH].set(lr_w2[0])
    w2 = w2.at[GLA_RANK:2 * GLA_RANK, GLA_KEY_WIDTH:].set(lr_w2[1])
    return (col(0, 2), col(2, 3), col(3, 4), pad_lanes(col(4, 5)), col(4, 5).T,
            col(5, 7), col(7, 8), col(8, 9), pad_lanes(col(9, 10)),
            conv_w, pad_lanes(gate_b[None, :]), gate_b[:, None],
            w2.astype(BF16), alpha_b.reshape(1, 2 * GLA_KEY_WIDTH))


def _per_head_gates(gtok, gmaj):
    bz, n, _ = gtok.shape
    gt = gtok.reshape(bz, n, 2, 2, ML_HEADS).transpose(0, 4, 1, 2, 3).reshape(bz, ML_HEADS, n, 4)
    gm = gmaj.reshape(bz, 2, 2, ML_HEADS, n // ML_CHUNK, ML_CHUNK)
    gm = gm.transpose(0, 3, 4, 1, 2, 5).reshape(bz, ML_HEADS, n // ML_CHUNK, 4, ML_CHUNK)
    return gt, gm


def _per_head_keys(a):
    bz, n, _ = a.shape
    return a.reshape(bz, n, GLA_HEADS, GLA_DK).transpose(0, 2, 1, 3)


def _token_mix_inputs(x, sh, sc, ng, wts, row_len):
    mq, mk, mv, mo, gtok, gmaj, gq, gk, gv, gr, bg = _inproj(x, sh, sc, ng, wts, row_len)
    gt, gm = _per_head_gates(gtok, gmaj)
    ml = (mq, mk, mv, gt, gm)
    gla = (_per_head_keys(gq), _per_head_keys(gk), gv,
           _per_head_keys(bg[..., :GLA_KEY_WIDTH]), _per_head_keys(bg[..., GLA_KEY_WIDTH:]))
    return ml, gla, mo, gr


def kernel(x, c, ctx, c_ctx, w_mod, b_mod, norm1_g, w_in, ml_conv_w, ml_gate_b, ml_norm_g,
           gla_lr_w2, gla_alpha_b, gla_norm_g, w_out, norm2_g, peer_wq, peer_keys, peer_u,
           peer_v, norm_f_g):
    assert w_mod.shape[0] == 1, "single trunk layer"
    bz, n, d = x.shape
    c_all = jnp.concatenate([c, c_ctx[None, :]], axis=0)
    c_all = jnp.pad(c_all, ((0, (-c_all.shape[0]) % SUBLANES), (0, 0)))
    mod = _modulation(c_all, w_mod[0], b_mod[0][None, :])
    sh1, sc1, g1, sh2, sc2, g2 = [m[:, None, :] for m in jnp.split(mod[:bz], 6, axis=1)]
    mod_c = mod[bz]
    csh1, csc1 = mod_c[None, None, :d], mod_c[None, None, d:2 * d]

    ng1 = norm1_g[0][None, :]
    wts = _inproj_weights(w_in[0], ml_conv_w[0], ml_gate_b[0], gla_lr_w2[0], gla_alpha_b[0])
    ml_lat, gla_lat, mo, gr = _token_mix_inputs(x, sh1, sc1, ng1, wts, GRID_W)
    ml_ctx, gla_ctx, _, _ = _token_mix_inputs(ctx, csh1, csc1, ng1, wts, ctx.shape[1])
    mhf, mhb = _mlstm(*ml_lat, *ml_ctx)
    ghf, ghb = _gla(*gla_lat, *gla_ctx)

    keys = peer_keys[0].reshape(2 * PEER_HEADS, PEER_NKEYS, PEER_HALF).astype(BF16)
    x1, h2t, e1, lim, e2, rk = _mix(
        x, mhf, mhb, ghf, ghb, mo, gr, ml_norm_g[0][None, :], gla_norm_g[0][None, :],
        w_out[0].astype(BF16), g1, sh2, sc2, norm2_g[0][None, :], peer_wq[0].astype(BF16), keys)
    return _peer(h2t, peer_u[0].astype(BF16), peer_v[0].astype(BF16).T, e1, lim, e2, rk,
                 x1, g2, norm_f_g[None, :])
```

```python
import functools

import jax
import jax.numpy as jnp
from jax import lax
from jax.experimental import pallas as pl
from jax.experimental.pallas import tpu as pltpu

F32 = jnp.float32
BF16 = jnp.bfloat16

EPS = 1e-6
D_MODEL = 1024
GRID_W = 64

ML_HEADS = 4
ML_HEAD_DIM = 128
ML_WIDTH = ML_HEADS * ML_HEAD_DIM
ML_CHUNK = 128
GLA_HEADS = 4
GLA_DK = 64
GLA_DV = 128
GLA_KEY_WIDTH = GLA_HEADS * GLA_DK
GLA_WIDTH = GLA_HEADS * GLA_DV
GLA_RANK = 16
GLA_TAU = 16.0
GLA_CHUNK = 64
N_GATES = 4 * ML_HEADS

PEER_HEADS = 8
PEER_NKEYS = 128
PEER_TOPK = 16
PEER_HALF = 128

LANES = 128
SUBLANES = 8
VMEM_LIMIT = 56 * 1024 * 1024

INPROJ_TM = 256
MIX_TM = 256
PEER_TM = 512
PEER_TE = 1024

NEG_INF = float("-inf")


def _params(*sem):
    return pltpu.CompilerParams(dimension_semantics=sem, vmem_limit_bytes=VMEM_LIMIT)


def _dot(a, b):
    return jnp.dot(a, b, preferred_element_type=F32)


def _dot_nt(a, b):
    return lax.dot_general(a, b, (((1,), (1,)), ((), ())), preferred_element_type=F32)


def _dot_tn(a, b):
    return lax.dot_general(a, b, (((0,), (0,)), ((), ())), preferred_element_type=F32)


def _split3(x):
    hi = x.astype(BF16)
    r1 = x - hi.astype(F32)
    mid = r1.astype(BF16)
    lo = (r1 - mid.astype(F32)).astype(BF16)
    return hi, mid, lo


def _dot_exact_rhs(a01, x):
    hi, mid, lo = _split3(x)
    return _dot(a01, hi) + _dot(a01, mid) + _dot(a01, lo)


def _dot_exact_lhs(x, a01):
    hi, mid, lo = _split3(x)
    return _dot(hi, a01) + _dot(mid, a01) + _dot(lo, a01)


def _sigmoid(x):
    return 1.0 / (1.0 + jnp.exp(-x))


def _log_sigmoid(x):
    return jnp.minimum(x, 0.0) - jnp.log(1.0 + jnp.exp(-jnp.abs(x)))


def _rms(x):
    return x * lax.rsqrt(jnp.mean(x * x, axis=-1, keepdims=True) + EPS)


def _mod_kernel(c_ref, w_ref, b_ref, o_ref):
    cond = c_ref[...]
    cond = cond * _sigmoid(cond)
    ch, cl = cond.astype(BF16), (cond - cond.astype(BF16).astype(F32)).astype(BF16)
    w = w_ref[...]
    wh = w.astype(BF16)
    wl = (w - wh.astype(F32)).astype(BF16)
    o_ref[...] = _dot(ch, wh) + _dot(ch, wl) + _dot(cl, wh) + b_ref[...]


def _modulation(c_all, w_mod, b_mod):
    rows, d = c_all.shape
    n_out = w_mod.shape[1]
    tn = 512
    return pl.pallas_call(
        _mod_kernel,
        grid=(n_out // tn,),
        in_specs=[pl.BlockSpec((rows, d), lambda j: (0, 0)),
                  pl.BlockSpec((d, tn), lambda j: (0, j)),
                  pl.BlockSpec((1, tn), lambda j: (0, j))],
        out_specs=pl.BlockSpec((rows, tn), lambda j: (0, j)),
        out_shape=jax.ShapeDtypeStruct((rows, n_out), F32),
        compiler_params=_params("arbitrary"),
        name="mod",
    )(c_all, w_mod, b_mod)


def _chunk_masks(tm, chunk):
    r = lax.broadcasted_iota(jnp.int32, (tm, tm), 0)
    c = lax.broadcasted_iota(jnp.int32, (tm, tm), 1)
    same = (r // chunk) == (c // chunk)
    prefix = jnp.where(same & (c <= r), 1.0, 0.0).astype(BF16)
    suffix = jnp.where(same & (c >= r), 1.0, 0.0).astype(BF16)
    return prefix, suffix


def _inproj_kernel(x_ref, sh_ref, sc_ref, ng_ref, wqk_ref, wv_ref, wo_ref, wg_ref, wgt_ref,
                   wgqk_ref, wgv_ref, wgr_ref, wlr_ref, conv_ref, gb_ref, gbc_ref, w2_ref, ab_ref,
                   mq_ref, mk_ref, mv_ref, mo_ref, gtok_ref, gmaj_ref,
                   gq_ref, gk_ref, gv_ref, gr_ref, bg_ref, *, row_len):
    tm = x_ref.shape[1]
    x = x_ref[0]
    h = _rms(x) * ng_ref[...] * (1.0 + sc_ref[0]) + sh_ref[0]
    hb = h.astype(BF16)

    u = _dot(hb, wqk_ref[...])
    pos = lax.broadcasted_iota(jnp.int32, (tm, 1), 0) % row_len
    up = jnp.where(pos == 0, 0.0, pltpu.roll(u, 1, 0))
    dn = jnp.where(pos == row_len - 1, 0.0, pltpu.roll(u, tm - 1, 0))
    cw = conv_ref[...]
    y = cw[0:1] * up + cw[1:2] * u + cw[2:3] * dn
    y = y * _sigmoid(y)
    mq_ref[0] = y[:, :ML_WIDTH].astype(BF16)
    mk_ref[0] = (y[:, ML_WIDTH:] * ML_HEAD_DIM ** -0.5).astype(BF16)
    mv_ref[0] = _dot(hb, wv_ref[...]).astype(BF16)
    mo_ref[0] = _dot(hb, wo_ref[...]).astype(BF16)

    pre128, suf128 = _chunk_masks(tm, ML_CHUNK)
    g = _dot(hb, wg_ref[...]) + gb_ref[...]
    col = lax.broadcasted_iota(jnp.int32, (1, LANES), 1)
    is_f = ((col % 8) >= 4) & (col < N_GATES)
    is_bwd = col >= 8
    lf = jnp.where(is_f, _log_sigmoid(g), 0.0)
    cum = jnp.where(is_bwd, _dot_exact_rhs(suf128, lf), _dot_exact_rhs(pre128, lf))
    gtok_ref[0] = jnp.where(is_f, cum, g)[:, :N_GATES]

    gt = _dot_nt(wgt_ref[...], hb) + gbc_ref[...]
    row = lax.broadcasted_iota(jnp.int32, (N_GATES, 1), 0)
    is_f_r = (row % 8) >= 4
    lft = jnp.where(is_f_r, _log_sigmoid(gt), 0.0)
    cum_t = jnp.where(row >= 8, _dot_exact_lhs(lft, pre128), _dot_exact_lhs(lft, suf128))
    gmaj_ref[0] = jnp.where(is_f_r, cum_t, gt)

    gqk = _dot(hb, wgqk_ref[...])
    gq_ref[0] = (gqk[:, :GLA_KEY_WIDTH] * GLA_DK ** -0.5).astype(BF16)
    gk_ref[0] = gqk[:, GLA_KEY_WIDTH:].astype(BF16)
    gv_ref[0] = _dot(hb, wgv_ref[...]).astype(BF16)
    gr_ref[0] = _dot(hb, wgr_ref[...]).astype(BF16)
    glr = _dot(hb, wlr_ref[...]).astype(BF16)
    alpha = _dot(glr, w2_ref[...]) + ab_ref[...]
    la = _log_sigmoid(alpha) * (1.0 / GLA_TAU)
    pre64, suf64 = _chunk_masks(tm, GLA_CHUNK)
    bg_ref[0, :, :GLA_KEY_WIDTH] = _dot_exact_rhs(pre64, la[:, :GLA_KEY_WIDTH])
    bg_ref[0, :, GLA_KEY_WIDTH:] = _dot_exact_rhs(suf64, la[:, GLA_KEY_WIDTH:])


def _inproj(x, sh, sc, ng, wts, row_len):
    bz, n, d = x.shape
    tm = min(INPROJ_TM, n)
    assert n % tm == 0 and tm % row_len == 0 and tm % ML_CHUNK == 0
    grid = (bz, n // tm)
    per_b = (lambda b, i: (b, 0, 0)) if sh.shape[0] == bz else (lambda b, i: (0, 0, 0))

    def full(a):
        return pl.BlockSpec(a.shape, lambda b, i, nd=a.ndim: (0,) * nd)

    def tok(width):
        return pl.BlockSpec((1, tm, width), lambda b, i: (b, i, 0))

    def out(width, dtype):
        return jax.ShapeDtypeStruct((bz, n, width), dtype)

    in_specs = [tok(d), pl.BlockSpec((1, 1, d), per_b), pl.BlockSpec((1, 1, d), per_b), full(ng)]
    in_specs += [full(w) for w in wts]
    out_specs = [tok(ML_WIDTH)] * 4 + [tok(N_GATES), pl.BlockSpec((1, N_GATES, tm), lambda b, i: (b, 0, i))]
    out_specs += [tok(GLA_KEY_WIDTH)] * 2 + [tok(GLA_WIDTH)] * 2 + [tok(2 * GLA_KEY_WIDTH)]
    out_shape = [out(ML_WIDTH, BF16)] * 4 + [out(N_GATES, F32),
                                              jax.ShapeDtypeStruct((bz, N_GATES, n), F32)]
    out_shape += [out(GLA_KEY_WIDTH, BF16)] * 2 + [out(GLA_WIDTH, BF16)] * 2 + [out(2 * GLA_KEY_WIDTH, F32)]
    return pl.pallas_call(
        functools.partial(_inproj_kernel, row_len=row_len),
        grid=grid, in_specs=in_specs, out_specs=out_specs, out_shape=out_shape,
        compiler_params=_params("parallel", "arbitrary"),
        name="inproj",
    )(x, sh, sc, ng, *wts)


def _ml_chunk(q, k, v, gt, gm, d, mask, state, want_out):
    ct, nv, m = state
    ll = q.shape[0]
    ig_col, b_col = gt[:, 2 * d:2 * d + 1], gt[:, 2 * d + 1:2 * d + 2]
    ig_row, b_row = gm[2 * d:2 * d + 1, :], gm[2 * d + 1:2 * d + 2, :]
    b_end = b_row[:, ll - 1:ll] if d == 0 else b_row[:, 0:1]
    h = None
    if want_out:
        d_log = jnp.where(mask, b_col - b_row + ig_row, NEG_INF)
        inter_log = b_col + m
        m_t = jnp.maximum(inter_log, jnp.max(d_log, axis=1, keepdims=True))
        scores = _dot_nt(q, k) * jnp.exp(d_log - m_t)
        inter = jnp.exp(inter_log - m_t)
        qf = q.astype(F32)
        num = _dot(scores.astype(BF16), v) + inter * _dot(q, ct.astype(BF16))
        den = jnp.sum(scores, axis=1, keepdims=True) + inter * jnp.sum(qf * nv, axis=1, keepdims=True)
        h = num / jnp.maximum(jnp.abs(den), jnp.exp(-m_t))
    m_loc = jnp.max(b_end - b_row + ig_row, axis=1, keepdims=True)
    w_col = jnp.exp(b_end - b_col + ig_col - m_loc)
    c_loc = _dot_tn(k, (w_col * v.astype(F32)).astype(BF16))
    n_loc = jnp.sum(w_col * k.astype(F32), axis=0, keepdims=True)
    m_new = jnp.maximum(b_end + m, m_loc)
    a = jnp.exp(b_end + m - m_new)
    bb = jnp.exp(m_loc - m_new)
    return h, (a * ct + bb * c_loc, a * nv + bb * n_loc, m_new)


def _mlstm_kernel(q_ref, k_ref, v_ref, gt_ref, gm_ref, qc_ref, kc_ref, vc_ref, gtc_ref, gmc_ref,
                  hf_ref, hb_ref):
    ll = ML_CHUNK
    nc = q_ref.shape[1] // ll
    ncc = qc_ref.shape[1] // ll
    r = lax.broadcasted_iota(jnp.int32, (ll, ll), 0)
    c = lax.broadcasted_iota(jnp.int32, (ll, ll), 1)
    masks = (c <= r, c >= r)

    def load(refs, ci):
        qr, kr, vr, gtr, gmr = refs
        rows = pl.ds(pl.multiple_of(ci * ll, ll), ll)
        return qr[0, rows, :], kr[0, rows, :], vr[0, rows, :], gtr[0, 0, rows, :], gmr[0, 0, ci]

    zero = (jnp.zeros((ML_HEAD_DIM, ML_HEAD_DIM), F32), jnp.zeros((1, ML_HEAD_DIM), F32),
            jnp.zeros((1, 1), F32))
    ctx_refs = (qc_ref, kc_ref, vc_ref, gtc_ref, gmc_ref)
    lat_refs = (q_ref, k_ref, v_ref, gt_ref, gm_ref)

    def ctx_body(i, carry):
        sf, sb = carry
        _, sf = _ml_chunk(*load(ctx_refs, i), 0, masks[0], sf, False)
        _, sb = _ml_chunk(*load(ctx_refs, ncc - 1 - i), 1, masks[1], sb, False)
        return sf, sb

    def lat_body(i, carry):
        sf, sb = carry
        cb = nc - 1 - i
        h, sf = _ml_chunk(*load(lat_refs, i), 0, masks[0], sf, True)
        hf_ref[0, pl.ds(pl.multiple_of(i * ll, ll), ll), :] = h
        h, sb = _ml_chunk(*load(lat_refs, cb), 1, masks[1], sb, True)
        hb_ref[0, pl.ds(pl.multiple_of(cb * ll, ll), ll), :] = h
        return sf, sb

    carry = lax.fori_loop(0, ncc, ctx_body, (zero, zero))
    lax.fori_loop(0, nc, lat_body, carry)


def _mlstm(q, k, v, gt, gm, qc, kc, vc, gtc, gmc):
    bz, n, _ = q.shape
    ncx = qc.shape[1]

    def seq(nn):
        return pl.BlockSpec((1, nn, ML_HEAD_DIM), lambda b, h: (b, 0, h))

    def gtok(nn):
        return pl.BlockSpec((1, 1, nn, 4), lambda b, h: (b, h, 0, 0))

    def gmaj(nn):
        return pl.BlockSpec((1, 1, nn // ML_CHUNK, 4, ML_CHUNK), lambda b, h: (b, h, 0, 0, 0))

    return pl.pallas_call(
        _mlstm_kernel,
        grid=(bz, ML_HEADS),
        in_specs=[seq(n)] * 3 + [gtok(n), gmaj(n)] + [seq(ncx)] * 3 + [gtok(ncx), gmaj(ncx)],
        out_specs=[seq(n)] * 2,
        out_shape=[jax.ShapeDtypeStruct((bz, n, ML_WIDTH), F32)] * 2,
        compiler_params=_params("parallel", "arbitrary"),
        name="mlstm",
    )(q, k, v, gt, gm, qc, kc, vc, gtc, gmc)


def _gla_chunk(q, k, v, b, d, mask, st, want_out):
    ll = q.shape[0]
    qf, kf = q.astype(F32), k.astype(F32)
    mid = ll // 2 if d == 0 else ll // 2 - 1
    ref = b[mid:mid + 1]
    b_end = b[ll - 1:ll] if d == 0 else b[0:1]
    o = None
    if want_out:
        att = _dot_nt((qf * jnp.exp(b - ref)).astype(BF16), (kf * jnp.exp(ref - b)).astype(BF16))
        att = jnp.where(mask, att, 0.0)
        o = _dot(att.astype(BF16), v) + _dot_nt((qf * jnp.exp(b)).astype(BF16), st.astype(BF16))
    s_loc = _dot_tn(v, (kf * jnp.exp(b_end - b)).astype(BF16))
    return o, jnp.exp(b_end) * st + s_loc


def _gla_kernel(q_ref, k_ref, v_ref, bf_ref, bb_ref, qc_ref, kc_ref, vc_ref, bfc_ref, bbc_ref,
                of_ref, ob_ref):
    ll = GLA_CHUNK
    nc = q_ref.shape[2] // ll
    ncc = qc_ref.shape[2] // ll
    r = lax.broadcasted_iota(jnp.int32, (ll, ll), 0)
    c = lax.broadcasted_iota(jnp.int32, (ll, ll), 1)
    masks = (c <= r, c >= r)

    def load(refs, ci, d):
        qr, kr, vr, bfr, bbr = refs
        rows = pl.ds(pl.multiple_of(ci * ll, ll), ll)
        br = bfr if d == 0 else bbr
        return qr[0, 0, rows, :], kr[0, 0, rows, :], vr[0, rows, :], br[0, 0, rows, :]

    zero = jnp.zeros((GLA_DV, GLA_DK), F32)
    ctx_refs = (qc_ref, kc_ref, vc_ref, bfc_ref, bbc_ref)
    lat_refs = (q_ref, k_ref, v_ref, bf_ref, bb_ref)

    def ctx_body(i, carry):
        sf, sb = carry
        _, sf = _gla_chunk(*load(ctx_refs, i, 0), 0, masks[0], sf, False)
        _, sb = _gla_chunk(*load(ctx_refs, ncc - 1 - i, 1), 1, masks[1], sb, False)
        return sf, sb

    def lat_body(i, carry):
        sf, sb = carry
        cb = nc - 1 - i
        o, sf = _gla_chunk(*load(lat_refs, i, 0), 0, masks[0], sf, True)
        of_ref[0, pl.ds(pl.multiple_of(i * ll, ll), ll), :] = o
        o, sb = _gla_chunk(*load(lat_refs, cb, 1), 1, masks[1], sb, True)
        ob_ref[0, pl.ds(pl.multiple_of(cb * ll, ll), ll), :] = o
        return sf, sb

    carry = lax.fori_loop(0, ncc, ctx_body, (zero, zero))
    lax.fori_loop(0, nc, lat_body, carry)


def _gla(q, k, v, bf, bb, qc, kc, vc, bfc, bbc):
    bz, _, n, _ = q.shape
    ncx = qc.shape[2]

    def key(nn):
        return pl.BlockSpec((1, 1, nn, GLA_DK), lambda b, h: (b, h, 0, 0))

    def val(nn):
        return pl.BlockSpec((1, nn, GLA_DV), lambda b, h: (b, 0, h))

    return pl.pallas_call(
        _gla_kernel,
        grid=(bz, GLA_HEADS),
        in_specs=[key(n), key(n), val(n), key(n), key(n), key(ncx), key(ncx), val(ncx), key(ncx), key(ncx)],
        out_specs=[val(n)] * 2,
        out_shape=[jax.ShapeDtypeStruct((bz, n, GLA_WIDTH), F32)] * 2,
        compiler_params=_params("parallel", "arbitrary"),
        name="gla",
    )(q, k, v, bf, bb, qc, kc, vc, bfc, bbc)


def _head_rms(h, g, width):
    parts = [_rms(h[:, i:i + width]) for i in range(0, h.shape[1], width)]
    return jnp.concatenate(parts, axis=1) * g


def _top16(s, ties):
    nk, tl = s.shape
    slot = lax.broadcasted_iota(jnp.int32, (PEER_TOPK, tl), 0)
    key_id = lax.broadcasted_iota(jnp.int32, (nk, tl), 0).astype(F32) if ties else None

    def body(r, carry):
        s, rank, vals = carry
        m = jnp.max(s, axis=0, keepdims=True)
        sel = s == m
        if ties:
            first = jnp.min(jnp.where(sel, key_id, float(nk)), axis=0, keepdims=True)
            sel = key_id == first
        rank = jnp.where(sel, r.astype(F32), rank)
        s = jnp.where(sel, NEG_INF, s)
        vals = jnp.where(slot == r, m, vals)
        return s, rank, vals

    init = (s, jnp.full((nk, tl), float(PEER_TOPK), F32), jnp.zeros((PEER_TOPK, tl), F32))
    _, rank, vals = lax.fori_loop(0, PEER_TOPK, body, init)
    return vals, rank


def _pair_candidates(v1, v2):
    blocks = [v1[0:1] + v2[0:8], v1[0:1] + v2[8:16]]
    blocks += [v1[a:a + 1] + v2[0:8] for a in range(1, 8)]
    blocks += [v1[8:16] + v2[0:1]]
    return jnp.concatenate(blocks, axis=0)


def _pick16(cand, ties):
    if ties:
        i = lax.broadcasted_iota(jnp.int32, cand.shape, 0)
        blk, rr = i // 8, i % 8
        flat = jnp.where(blk == 0, rr, jnp.where(blk == 1, 8 + rr,
                         jnp.where(blk <= 8, (blk - 1) * 16 + rr, (8 + rr) * 16))).astype(F32)

    def body(_, carry):
        cand, picked = carry
        m = jnp.max(cand, axis=0, keepdims=True)
        sel = cand == m
        if ties:
            first = jnp.min(jnp.where(sel, flat, 1e9), axis=0, keepdims=True)
            sel = flat == first
        return jnp.where(sel, NEG_INF, cand), jnp.where(sel, 1.0, picked)

    _, picked = lax.fori_loop(0, PEER_TOPK, body, (cand, jnp.zeros_like(cand)))
    return picked


def _pair_counts(cand, picked):
    z = jnp.sum(picked * jnp.exp(cand - cand[0:1]), axis=0, keepdims=True)
    n_rows = [jnp.sum(picked[0:16], axis=0, keepdims=True)]
    n_rows += [jnp.sum(picked[8 * (a + 1):8 * (a + 2)], axis=0, keepdims=True) for a in range(1, 8)]
    return jnp.concatenate(n_rows + [picked[72:80]], axis=0), z


def _mix_kernel(x_ref, mhf_ref, mhb_ref, ghf_ref, ghb_ref, mo_ref, gr_ref, mlg_ref, glag_ref,
                wout_ref, g1_ref, sh2_ref, sc2_ref, n2g_ref, wq_ref, keys_ref,
                x1_ref, h2t_ref, e1_ref, l_ref, e2_ref, rk_ref, q_scr, s_scr, vals_scr, rank_scr,
                picked_scr):
    ml = _head_rms(mhf_ref[0] + mhb_ref[0], mlg_ref[...], ML_HEAD_DIM)
    ml = _sigmoid(mo_ref[0].astype(F32)) * ml
    gl = _head_rms(ghf_ref[0] + ghb_ref[0], glag_ref[...], GLA_DV)
    gr = gr_ref[0].astype(F32)
    gl = gr * _sigmoid(gr) * gl
    mix = jnp.concatenate([ml, gl], axis=1).astype(BF16)
    x1 = x_ref[0] + g1_ref[0] * _dot(mix, wout_ref[...])
    x1_ref[0] = x1
    h2 = _rms(x1) * n2g_ref[...] * (1.0 + sc2_ref[0]) + sh2_ref[0]
    h2t_ref[...] = h2.T.astype(BF16)
    qall = _dot(h2.astype(BF16), wq_ref[...]).astype(BF16)
    for j in range(2 * PEER_HEADS):
        q_scr[j] = qall[:, j * PEER_HALF:(j + 1) * PEER_HALF]

    tm = x_ref.shape[1]
    topk = float(PEER_TOPK)
    lane_chunks = [slice(i, i + LANES) for i in range(0, tm, LANES)]

    def head_body(p, carry):
        for hf in range(2):
            s = _dot_nt(keys_ref[2 * p + hf], q_scr[2 * p + hf])
            s_scr[hf] = s
            for lanes in lane_chunks:
                vals, rank = _top16(s[:, lanes], ties=False)
                vals_scr[hf, :, lanes] = vals
                rank_scr[hf, :, lanes] = rank
                n_ranked = jnp.sum(jnp.where(rank < topk, 1.0, 0.0), axis=0, keepdims=True)

                @pl.when(jnp.max(n_ranked) > topk)
                def _():
                    vals, rank = _top16(s_scr[hf, :, lanes], ties=True)
                    vals_scr[hf, :, lanes] = vals
                    rank_scr[hf, :, lanes] = rank

        for lanes in lane_chunks:
            v1, v2 = vals_scr[0, :, lanes], vals_scr[1, :, lanes]
            cand = _pair_candidates(v1, v2)
            picked = _pick16(cand, ties=False)
            picked_scr[:, lanes] = picked

            @pl.when(jnp.max(jnp.sum(picked, axis=0, keepdims=True)) > topk)
            def _():
                picked_scr[:, lanes] = _pick16(
                    _pair_candidates(vals_scr[0, :, lanes], vals_scr[1, :, lanes]), ties=True)

            n, z = _pair_counts(cand, picked_scr[:, lanes])
            rank1 = rank_scr[0, :, lanes]
            lim = jnp.zeros_like(rank1)
            for a in range(PEER_TOPK):
                lim = jnp.where(rank1 == float(a), n[a:a + 1], lim)
            e1_ref[p, :, lanes] = jnp.exp(s_scr[0, :, lanes] - v1[0:1]) / z
            l_ref[p, :, lanes] = lim
            e2 = jnp.exp(s_scr[1, :, lanes] - v2[0:1]).astype(BF16)
            e2_ref[p, :, :, lanes] = e2.reshape(PEER_NKEYS // 16, 16, LANES)
            rk_ref[p, :, :, lanes] = rank_scr[1, :, lanes].astype(BF16).reshape(PEER_NKEYS // 16, 16, LANES)
        return carry

    lax.fori_loop(0, PEER_HEADS, head_body, 0)


def _mix(x, mhf, mhb, ghf, ghb, mo, gr, mlg, glag, wout, g1, sh2, sc2, n2g, wq, keys):
    bz, n, d = x.shape
    tm = min(MIX_TM, n)
    assert n % tm == 0 and tm % LANES == 0
    nt = n // tm
    t_all = bz * n

    def tok(width):
        return pl.BlockSpec((1, tm, width), lambda b, i: (b, i, 0))

    def full(a):
        return pl.BlockSpec(a.shape, lambda b, i, nd=a.ndim: (0,) * nd)

    vec = pl.BlockSpec((1, 1, d), lambda b, i: (b, 0, 0))
    tab1 = pl.BlockSpec((PEER_HEADS, PEER_NKEYS, tm), lambda b, i: (0, 0, b * nt + i))
    tab2 = pl.BlockSpec((PEER_HEADS, PEER_NKEYS // 16, 16, tm), lambda b, i: (0, 0, 0, b * nt + i))
    tab1_shape = jax.ShapeDtypeStruct((PEER_HEADS, PEER_NKEYS, t_all), F32)
    tab2_shape = jax.ShapeDtypeStruct((PEER_HEADS, PEER_NKEYS // 16, 16, t_all), BF16)
    return pl.pallas_call(
        _mix_kernel,
        grid=(bz, nt),
        in_specs=[tok(d)] + [tok(ML_WIDTH)] * 6 + [full(mlg), full(glag), full(wout), vec, vec, vec,
                                                    full(n2g), full(wq), full(keys)],
        out_specs=[tok(d), pl.BlockSpec((d, tm), lambda b, i: (0, b * nt + i)), tab1, tab1, tab2, tab2],
        out_shape=[jax.ShapeDtypeStruct((bz, n, d), F32), jax.ShapeDtypeStruct((d, t_all), BF16),
                   tab1_shape, tab1_shape, tab2_shape, tab2_shape],
        scratch_shapes=[pltpu.VMEM((2 * PEER_HEADS, tm, PEER_HALF), BF16),
                        pltpu.VMEM((2, PEER_NKEYS, tm), F32),
                        pltpu.VMEM((2, PEER_TOPK, tm), F32),
                        pltpu.VMEM((2, PEER_NKEYS, tm), F32),
                        pltpu.VMEM((80, tm), F32)],
        compiler_params=_params("parallel", "arbitrary"),
        name="mix",
    )(x, mhf, mhb, ghf, ghb, mo, gr, mlg, glag, wout, g1, sh2, sc2, n2g, wq, keys)


def _peer_kernel(h2t_ref, u_ref, vt_ref, e1_ref, l_ref, e2_ref, rk_ref, x1_ref, g2_ref, nfg_ref,
                 out_ref, acc_ref):
    j = pl.program_id(2)
    te, tm = u_ref.shape[0], h2t_ref.shape[1]
    rows_per_step = te // PEER_NKEYS
    tiles = PEER_NKEYS // 16

    @pl.when(j == 0)
    def _():
        acc_ref[...] = jnp.zeros_like(acc_ref)

    act = _dot(u_ref[...], h2t_ref[...])
    act = 0.5 * act * (1.0 + lax.erf(act * 2.0 ** -0.5))
    act = act.astype(BF16).reshape(rows_per_step, tiles, 16, tm)
    parts = []
    for i in range(rows_per_step):
        w = None
        for p in range(PEER_HEADS):
            e1 = jnp.broadcast_to(e1_ref[p, i:i + 1, :], (16, tm)).astype(BF16)[None]
            lim = jnp.broadcast_to(l_ref[p, i:i + 1, :], (16, tm)).astype(BF16)[None]
            term = jnp.where(rk_ref[p] < lim, e2_ref[p] * e1, jnp.zeros((), BF16))
            w = term if w is None else w + term
        parts.append(w * act[i])
    gated = jnp.concatenate(parts, axis=0).reshape(te, tm)
    acc_ref[...] += _dot(vt_ref[...], gated)

    @pl.when(j == pl.num_programs(2) - 1)
    def _():
        xf = x1_ref[0] + g2_ref[0] * acc_ref[...].T
        out_ref[0] = _rms(xf) * nfg_ref[...]


def _peer(h2t, u, vt, e1, lim, e2, rk, x1, g2, nfg):
    bz, n, d = x1.shape
    n_exp = u.shape[0]
    tm = min(PEER_TM, n)
    te = PEER_TE
    assert n % tm == 0 and n_exp % te == 0 and te % PEER_NKEYS == 0
    nt = n // tm
    tab1 = pl.BlockSpec((PEER_HEADS, te // PEER_NKEYS, tm), lambda b, i, j: (0, j, b * nt + i))
    tab2 = pl.BlockSpec((PEER_HEADS, PEER_NKEYS // 16, 16, tm), lambda b, i, j: (0, 0, 0, b * nt + i))
    tok = pl.BlockSpec((1, tm, d), lambda b, i, j: (b, i, 0))
    return pl.pallas_call(
        _peer_kernel,
        grid=(bz, nt, n_exp // te),
        in_specs=[pl.BlockSpec((d, tm), lambda b, i, j: (0, b * nt + i)),
                  pl.BlockSpec((te, d), lambda b, i, j: (j, 0)),
                  pl.BlockSpec((d, te), lambda b, i, j: (0, j)),
                  tab1, tab1, tab2, tab2, tok,
                  pl.BlockSpec((1, 1, d), lambda b, i, j: (b, 0, 0)),
                  pl.BlockSpec((1, d), lambda b, i, j: (0, 0))],
        out_specs=tok,
        out_shape=jax.ShapeDtypeStruct((bz, n, d), F32),
        scratch_shapes=[pltpu.VMEM((d, tm), F32)],
        compiler_params=_params("parallel", "parallel", "arbitrary"),
        name="peer",
    )(h2t, u, vt, e1, lim, e2, rk, x1, g2, nfg)


def _inproj_weights(w_in, conv_w, gate_b, lr_w2, alpha_b):
    widths = (ML_WIDTH, ML_WIDTH, ML_WIDTH, ML_WIDTH, N_GATES,
              GLA_KEY_WIDTH, GLA_KEY_WIDTH, GLA_WIDTH, GLA_WIDTH, 2 * GLA_RANK)
    offs = [0]
    for w in widths:
        offs.append(offs[-1] + w)
    wb = w_in.astype(BF16)
    col = lambda a, b: wb[:, offs[a]:offs[b]]
    pad_lanes = lambda a: jnp.pad(a, ((0, 0), (0, LANES - a.shape[1])))
    w2 = jnp.zeros((LANES, 2 * GLA_KEY_WIDTH), F32)
    w2 = w2.at[:GLA_RANK, :GLA_KEY_WIDTH].set(lr_w2[0])
    w2 = w2.at[GLA_RANK:2 * GLA_RANK, GLA_KEY_WIDTH:].set(lr_w2[1])
    return (col(0, 2), col(2, 3), col(3, 4), pad_lanes(col(4, 5)), col(4, 5).T,
            col(5, 7), col(7, 8), col(8, 9), pad_lanes(col(9, 10)),
            conv_w, pad_lanes(gate_b[None, :]), gate_b[:, None],
            w2.astype(BF16), alpha_b.reshape(1, 2 * GLA_KEY_WIDTH))


def _per_head_gates(gtok, gmaj):
    bz, n, _ = gtok.shape
    gt = gtok.reshape(bz, n, 2, 2, ML_HEADS).transpose(0, 4, 1, 2, 3).reshape(bz, ML_HEADS, n, 4)
    gm = gmaj.reshape(bz, 2, 2, ML_HEADS, n // ML_CHUNK, ML_CHUNK)
    gm = gm.transpose(0, 3, 4, 1, 2, 5).reshape(bz, ML_HEADS, n // ML_CHUNK, 4, ML_CHUNK)
    return gt, gm


def _per_head_keys(a):
    bz, n, _ = a.shape
    return a.reshape(bz, n, GLA_HEADS, GLA_DK).transpose(0, 2, 1, 3)


def _token_mix_inputs(x, sh, sc, ng, wts, row_len):
    mq, mk, mv, mo, gtok, gmaj, gq, gk, gv, gr, bg = _inproj(x, sh, sc, ng, wts, row_len)
    gt, gm = _per_head_gates(gtok, gmaj)
    ml = (mq, mk, mv, gt, gm)
    gla = (_per_head_keys(gq), _per_head_keys(gk), gv,
           _per_head_keys(bg[..., :GLA_KEY_WIDTH]), _per_head_keys(bg[..., GLA_KEY_WIDTH:]))
    return ml, gla, mo, gr


def kernel(x, c, ctx, c_ctx, w_mod, b_mod, norm1_g, w_in, ml_conv_w, ml_gate_b, ml_norm_g,
           gla_lr_w2, gla_alpha_b, gla_norm_g, w_out, norm2_g, peer_wq, peer_keys, peer_u,
           peer_v, norm_f_g):
    assert w_mod.shape[0] == 1, "single trunk layer"
    bz, n, d = x.shape
    c_all = jnp.concatenate([c, c_ctx[None, :]], axis=0)
    c_all = jnp.pad(c_all, ((0, (-c_all.shape[0]) % SUBLANES), (0, 0)))
    mod = _modulation(c_all, w_mod[0], b_mod[0][None, :])
    sh1, sc1, g1, sh2, sc2, g2 = [m[:, None, :] for m in jnp.split(mod[:bz], 6, axis=1)]
    mod_c = mod[bz]
    csh1, csc1 = mod_c[None, None, :d], mod_c[None, None, d:2 * d]

    ng1 = norm1_g[0][None, :]
    wts = _inproj_weights(w_in[0], ml_conv_w[0], ml_gate_b[0], gla_lr_w2[0], gla_alpha_b[0])
    ml_lat, gla_lat, mo, gr = _token_mix_inputs(x, sh1, sc1, ng1, wts, GRID_W)
    ml_ctx, gla_ctx, _, _ = _token_mix_inputs(ctx, csh1, csc1, ng1, wts, ctx.shape[1])
    mhf, mhb = _mlstm(*ml_lat, *ml_ctx)
    ghf, ghb = _gla(*gla_lat, *gla_ctx)

    keys = peer_keys[0].reshape(2 * PEER_HEADS, PEER_NKEYS, PEER_HALF).astype(BF16)
    x1, h2t, e1, lim, e2, rk = _mix(
        x, mhf, mhb, ghf, ghb, mo, gr, ml_norm_g[0][None, :], gla_norm_g[0][None, :],
        w_out[0].astype(BF16), g1, sh2, sc2, norm2_g[0][None, :], peer_wq[0].astype(BF16), keys)
    return _peer(h2t, peer_u[0].astype(BF16), peer_v[0].astype(BF16).T, e1, lim, e2, rk,
                 x1, g2, norm_f_g[None, :])
```

```python
import functools

import jax
import jax.numpy as jnp
from jax import lax
from jax.experimental import pallas as pl
from jax.experimental.pallas import tpu as pltpu

F32 = jnp.float32
BF16 = jnp.bfloat16

EPS = 1e-6
D_MODEL = 1024
GRID_W = 64

ML_HEADS = 4
ML_HEAD_DIM = 128
ML_WIDTH = ML_HEADS * ML_HEAD_DIM
ML_CHUNK = 128
GLA_HEADS = 4
GLA_DK = 64
GLA_DV = 128
GLA_KEY_WIDTH = GLA_HEADS * GLA_DK
GLA_WIDTH = GLA_HEADS * GLA_DV
GLA_RANK = 16
GLA_TAU = 16.0
GLA_CHUNK = 64
N_GATES = 4 * ML_HEADS

PEER_HEADS = 8
PEER_NKEYS = 128
PEER_TOPK = 16
PEER_HALF = 128

LANES = 128
SUBLANES = 8
VMEM_LIMIT = 56 * 1024 * 1024

INPROJ_TM = 256
MIX_TM = 256
PEER_TM = 512
PEER_TE = 1024

NEG_INF = float("-inf")


def _params(*sem):
    return pltpu.CompilerParams(dimension_semantics=sem, vmem_limit_bytes=VMEM_LIMIT)


def _dot(a, b):
    return jnp.dot(a, b, preferred_element_type=F32)


def _dot_nt(a, b):
    return lax.dot_general(a, b, (((1,), (1,)), ((), ())), preferred_element_type=F32)


def _dot_tn(a, b):
    return lax.dot_general(a, b, (((0,), (0,)), ((), ())), preferred_element_type=F32)


def _split3(x):
    hi = x.astype(BF16)
    r1 = x - hi.astype(F32)
    mid = r1.astype(BF16)
    lo = (r1 - mid.astype(F32)).astype(BF16)
    return hi, mid, lo


def _dot_exact_rhs(a01, x):
    hi, mid, lo = _split3(x)
    return _dot(a01, hi) + _dot(a01, mid) + _dot(a01, lo)


def _dot_exact_lhs(x, a01):
    hi, mid, lo = _split3(x)
    return _dot(hi, a01) + _dot(mid, a01) + _dot(lo, a01)


def _sigmoid(x):
    return 1.0 / (1.0 + jnp.exp(-x))


def _log_sigmoid(x):
    return jnp.minimum(x, 0.0) - jnp.log(1.0 + jnp.exp(-jnp.abs(x)))


def _rms(x):
    return x * lax.rsqrt(jnp.mean(x * x, axis=-1, keepdims=True) + EPS)


def _mod_kernel(c_ref, w_ref, b_ref, o_ref):
    cond = c_ref[...]
    cond = cond * _sigmoid(cond)
    ch, cl = cond.astype(BF16), (cond - cond.astype(BF16).astype(F32)).astype(BF16)
    w = w_ref[...]
    wh = w.astype(BF16)
    wl = (w - wh.astype(F32)).astype(BF16)
    o_ref[...] = _dot(ch, wh) + _dot(ch, wl) + _dot(cl, wh) + b_ref[...]


def _modulation(c_all, w_mod, b_mod):
    rows, d = c_all.shape
    n_out = w_mod.shape[1]
    tn = 512
    return pl.pallas_call(
        _mod_kernel,
        grid=(n_out // tn,),
        in_specs=[pl.BlockSpec((rows, d), lambda j: (0, 0)),
                  pl.BlockSpec((d, tn), lambda j: (0, j)),
                  pl.BlockSpec((1, tn), lambda j: (0, j))],
        out_specs=pl.BlockSpec((rows, tn), lambda j: (0, j)),
        out_shape=jax.ShapeDtypeStruct((rows, n_out), F32),
        compiler_params=_params("arbitrary"),
        name="mod",
    )(c_all, w_mod, b_mod)


def _chunk_masks(tm, chunk):
    r = lax.broadcasted_iota(jnp.int32, (tm, tm), 0)
    c = lax.broadcasted_iota(jnp.int32, (tm, tm), 1)
    same = (r // chunk) == (c // chunk)
    prefix = jnp.where(same & (c <= r), 1.0, 0.0).astype(BF16)
    suffix = jnp.where(same & (c >= r), 1.0, 0.0).astype(BF16)
    return prefix, suffix


def _inproj_kernel(x_ref, sh_ref, sc_ref, ng_ref, wqk_ref, wv_ref, wo_ref, wg_ref, wgt_ref,
                   wgqk_ref, wgv_ref, wgr_ref, wlr_ref, conv_ref, gb_ref, gbc_ref, w2_ref, ab_ref,
                   mq_ref, mk_ref, mv_ref, mo_ref, gtok_ref, gmaj_ref,
                   gq_ref, gk_ref, gv_ref, gr_ref, bg_ref, *, row_len):
    tm = x_ref.shape[1]
    x = x_ref[0]
    h = _rms(x) * ng_ref[...] * (1.0 + sc_ref[0]) + sh_ref[0]
    hb = h.astype(BF16)

    u = _dot(hb, wqk_ref[...])
    pos = lax.broadcasted_iota(jnp.int32, (tm, 1), 0) % row_len
    up = jnp.where(pos == 0, 0.0, pltpu.roll(u, 1, 0))
    dn = jnp.where(pos == row_len - 1, 0.0, pltpu.roll(u, tm - 1, 0))
    cw = conv_ref[...]
    y = cw[0:1] * up + cw[1:2] * u + cw[2:3] * dn
    y = y * _sigmoid(y)
    mq_ref[0] = y[:, :ML_WIDTH].astype(BF16)
    mk_ref[0] = (y[:, ML_WIDTH:] * ML_HEAD_DIM ** -0.5).astype(BF16)
    mv_ref[0] = _dot(hb, wv_ref[...]).astype(BF16)
    mo_ref[0] = _dot(hb, wo_ref[...]).astype(BF16)

    pre128, suf128 = _chunk_masks(tm, ML_CHUNK)
    g = _dot(hb, wg_ref[...]) + gb_ref[...]
    col = lax.broadcasted_iota(jnp.int32, (1, LANES), 1)
    is_f = ((col % 8) >= 4) & (col < N_GATES)
    is_bwd = col >= 8
    lf = jnp.where(is_f, _log_sigmoid(g), 0.0)
    cum = jnp.where(is_bwd, _dot_exact_rhs(suf128, lf), _dot_exact_rhs(pre128, lf))
    gtok_ref[0] = jnp.where(is_f, cum, g)[:, :N_GATES]

    gt = _dot_nt(wgt_ref[...], hb) + gbc_ref[...]
    row = lax.broadcasted_iota(jnp.int32, (N_GATES, 1), 0)
    is_f_r = (row % 8) >= 4
    lft = jnp.where(is_f_r, _log_sigmoid(gt), 0.0)
    cum_t = jnp.where(row >= 8, _dot_exact_lhs(lft, pre128), _dot_exact_lhs(lft, suf128))
    gmaj_ref[0] = jnp.where(is_f_r, cum_t, gt)

    gqk = _dot(hb, wgqk_ref[...])
    gq_ref[0] = (gqk[:, :GLA_KEY_WIDTH] * GLA_DK ** -0.5).astype(BF16)
    gk_ref[0] = gqk[:, GLA_KEY_WIDTH:].astype(BF16)
    gv_ref[0] = _dot(hb, wgv_ref[...]).astype(BF16)
    gr_ref[0] = _dot(hb, wgr_ref[...]).astype(BF16)
    glr = _dot(hb, wlr_ref[...]).astype(BF16)
    alpha = _dot(glr, w2_ref[...]) + ab_ref[...]
    la = _log_sigmoid(alpha) * (1.0 / GLA_TAU)
    pre64, suf64 = _chunk_masks(tm, GLA_CHUNK)
    bg_ref[0, :, :GLA_KEY_WIDTH] = _dot_exact_rhs(pre64, la[:, :GLA_KEY_WIDTH])
    bg_ref[0, :, GLA_KEY_WIDTH:] = _dot_exact_rhs(suf64, la[:, GLA_KEY_WIDTH:])


def _inproj(x, sh, sc, ng, wts, row_len):
    bz, n, d = x.shape
    tm = min(INPROJ_TM, n)
    assert n % tm == 0 and tm % row_len == 0 and tm % ML_CHUNK == 0
    grid = (bz, n // tm)
    per_b = (lambda b, i: (b, 0, 0)) if sh.shape[0] == bz else (lambda b, i: (0, 0, 0))

    def full(a):
        return pl.BlockSpec(a.shape, lambda b, i, nd=a.ndim: (0,) * nd)

    def tok(width):
        return pl.BlockSpec((1, tm, width), lambda b, i: (b, i, 0))

    def out(width, dtype):
        return jax.ShapeDtypeStruct((bz, n, width), dtype)

    in_specs = [tok(d), pl.BlockSpec((1, 1, d), per_b), pl.BlockSpec((1, 1, d), per_b), full(ng)]
    in_specs += [full(w) for w in wts]
    out_specs = [tok(ML_WIDTH)] * 4 + [tok(N_GATES), pl.BlockSpec((1, N_GATES, tm), lambda b, i: (b, 0, i))]
    out_specs += [tok(GLA_KEY_WIDTH)] * 2 + [tok(GLA_WIDTH)] * 2 + [tok(2 * GLA_KEY_WIDTH)]
    out_shape = [out(ML_WIDTH, BF16)] * 4 + [out(N_GATES, F32),
                                              jax.ShapeDtypeStruct((bz, N_GATES, n), F32)]
    out_shape += [out(GLA_KEY_WIDTH, BF16)] * 2 + [out(GLA_WIDTH, BF16)] * 2 + [out(2 * GLA_KEY_WIDTH, F32)]
    return pl.pallas_call(
        functools.partial(_inproj_kernel, row_len=row_len),
        grid=grid, in_specs=in_specs, out_specs=out_specs, out_shape=out_shape,
        compiler_params=_params("parallel", "arbitrary"),
        name="inproj",
    )(x, sh, sc, ng, *wts)


def _ml_chunk(q, k, v, gt, gm, d, mask, state, want_out):
    ct, nv, m = state
    ll = q.shape[0]
    ig_col, b_col = gt[:, 2 * d:2 * d + 1], gt[:, 2 * d + 1:2 * d + 2]
    ig_row, b_row = gm[2 * d:2 * d + 1, :], gm[2 * d + 1:2 * d + 2, :]
    b_end = b_row[:, ll - 1:ll] if d == 0 else b_row[:, 0:1]
    h = None
    if want_out:
        d_log = jnp.where(mask, b_col - b_row + ig_row, NEG_INF)
        inter_log = b_col + m
        m_t = jnp.maximum(inter_log, jnp.max(d_log, axis=1, keepdims=True))
        scores = _dot_nt(q, k) * jnp.exp(d_log - m_t)
        inter = jnp.exp(inter_log - m_t)
        qf = q.astype(F32)
        num = _dot(scores.astype(BF16), v) + inter * _dot(q, ct.astype(BF16))
        den = jnp.sum(scores, axis=1, keepdims=True) + inter * jnp.sum(qf * nv, axis=1, keepdims=True)
        h = num / jnp.maximum(jnp.abs(den), jnp.exp(-m_t))
    m_loc = jnp.max(b_end - b_row + ig_row, axis=1, keepdims=True)
    w_col = jnp.exp(b_end - b_col + ig_col - m_loc)
    c_loc = _dot_tn(k, (w_col * v.astype(F32)).astype(BF16))
    n_loc = jnp.sum(w_col * k.astype(F32), axis=0, keepdims=True)
    m_new = jnp.maximum(b_end + m, m_loc)
    a = jnp.exp(b_end + m - m_new)
    bb = jnp.exp(m_loc - m_new)
    return h, (a * ct + bb * c_loc, a * nv + bb * n_loc, m_new)


def _mlstm_kernel(q_ref, k_ref, v_ref, gt_ref, gm_ref, qc_ref, kc_ref, vc_ref, gtc_ref, gmc_ref,
                  hf_ref, hb_ref):
    ll = ML_CHUNK
    nc = q_ref.shape[1] // ll
    ncc = qc_ref.shape[1] // ll
    r = lax.broadcasted_iota(jnp.int32, (ll, ll), 0)
    c = lax.broadcasted_iota(jnp.int32, (ll, ll), 1)
    masks = (c <= r, c >= r)

    def load(refs, ci):
        qr, kr, vr, gtr, gmr = refs
        rows = pl.ds(pl.multiple_of(ci * ll, ll), ll)
        return qr[0, rows, :], kr[0, rows, :], vr[0, rows, :], gtr[0, 0, rows, :], gmr[0, 0, ci]

    zero = (jnp.zeros((ML_HEAD_DIM, ML_HEAD_DIM), F32), jnp.zeros((1, ML_HEAD_DIM), F32),
            jnp.zeros((1, 1), F32))
    ctx_refs = (qc_ref, kc_ref, vc_ref, gtc_ref, gmc_ref)
    lat_refs = (q_ref, k_ref, v_ref, gt_ref, gm_ref)

    def ctx_body(i, carry):
        sf, sb = carry
        _, sf = _ml_chunk(*load(ctx_refs, i), 0, masks[0], sf, False)
        _, sb = _ml_chunk(*load(ctx_refs, ncc - 1 - i), 1, masks[1], sb, False)
        return sf, sb

    def lat_body(i, carry):
        sf, sb = carry
        cb = nc - 1 - i
        h, sf = _ml_chunk(*load(lat_refs, i), 0, masks[0], sf, True)
        hf_ref[0, pl.ds(pl.multiple_of(i * ll, ll), ll), :] = h
        h, sb = _ml_chunk(*load(lat_refs, cb), 1, masks[1], sb, True)
        hb_ref[0, pl.ds(pl.multiple_of(cb * ll, ll), ll), :] = h
        return sf, sb

    carry = lax.fori_loop(0, ncc, ctx_body, (zero, zero))
    lax.fori_loop(0, nc, lat_body, carry)


def _mlstm(q, k, v, gt, gm, qc, kc, vc, gtc, gmc):
    bz, n, _ = q.shape
    ncx = qc.shape[1]

    def seq(nn):
        return pl.BlockSpec((1, nn, ML_HEAD_DIM), lambda b, h: (b, 0, h))

    def gtok(nn):
        return pl.BlockSpec((1, 1, nn, 4), lambda b, h: (b, h, 0, 0))

    def gmaj(nn):
        return pl.BlockSpec((1, 1, nn // ML_CHUNK, 4, ML_CHUNK), lambda b, h: (b, h, 0, 0, 0))

    return pl.pallas_call(
        _mlstm_kernel,
        grid=(bz, ML_HEADS),
        in_specs=[seq(n)] * 3 + [gtok(n), gmaj(n)] + [seq(ncx)] * 3 + [gtok(ncx), gmaj(ncx)],
        out_specs=[seq(n)] * 2,
        out_shape=[jax.ShapeDtypeStruct((bz, n, ML_WIDTH), F32)] * 2,
        compiler_params=_params("parallel", "arbitrary"),
        name="mlstm",
    )(q, k, v, gt, gm, qc, kc, vc, gtc, gmc)


def _gla_chunk(q, k, v, b, d, mask, st, want_out):
    ll = q.shape[0]
    qf, kf = q.astype(F32), k.astype(F32)
    mid = ll // 2 if d == 0 else ll // 2 - 1
    ref = b[mid:mid + 1]
    b_end = b[ll - 1:ll] if d == 0 else b[0:1]
    o = None
    if want_out:
        att = _dot_nt((qf * jnp.exp(b - ref)).astype(BF16), (kf * jnp.exp(ref - b)).astype(BF16))
        att = jnp.where(mask, att, 0.0)
        o = _dot(att.astype(BF16), v) + _dot_nt((qf * jnp.exp(b)).astype(BF16), st.astype(BF16))
    s_loc = _dot_tn(v, (kf * jnp.exp(b_end - b)).astype(BF16))
    return o, jnp.exp(b_end) * st + s_loc


def _gla_kernel(q_ref, k_ref, v_ref, bf_ref, bb_ref, qc_ref, kc_ref, vc_ref, bfc_ref, bbc_ref,
                of_ref, ob_ref):
    ll = GLA_CHUNK
    nc = q_ref.shape[2] // ll
    ncc = qc_ref.shape[2] // ll
    r = lax.broadcasted_iota(jnp.int32, (ll, ll), 0)
    c = lax.broadcasted_iota(jnp.int32, (ll, ll), 1)
    masks = (c <= r, c >= r)

    def load(refs, ci, d):
        qr, kr, vr, bfr, bbr = refs
        rows = pl.ds(pl.multiple_of(ci * ll, ll), ll)
        br = bfr if d == 0 else bbr
        return qr[0, 0, rows, :], kr[0, 0, rows, :], vr[0, rows, :], br[0, 0, rows, :]

    zero = jnp.zeros((GLA_DV, GLA_DK), F32)
    ctx_refs = (qc_ref, kc_ref, vc_ref, bfc_ref, bbc_ref)
    lat_refs = (q_ref, k_ref, v_ref, bf_ref, bb_ref)

    def ctx_body(i, carry):
        sf, sb = carry
        _, sf = _gla_chunk(*load(ctx_refs, i, 0), 0, masks[0], sf, False)
        _, sb = _gla_chunk(*load(ctx_refs, ncc - 1 - i, 1), 1, masks[1], sb, False)
        return sf, sb

    def lat_body(i, carry):
        sf, sb = carry
        cb = nc - 1 - i
        o, sf = _gla_chunk(*load(lat_refs, i, 0), 0, masks[0], sf, True)
        of_ref[0, pl.ds(pl.multiple_of(i * ll, ll), ll), :] = o
        o, sb = _gla_chunk(*load(lat_refs, cb, 1), 1, masks[1], sb, True)
        ob_ref[0, pl.ds(pl.multiple_of(cb * ll, ll), ll), :] = o
        return sf, sb

    carry = lax.fori_loop(0, ncc, ctx_body, (zero, zero))
    lax.fori_loop(0, nc, lat_body, carry, unroll=2)


def _gla(q, k, v, bf, bb, qc, kc, vc, bfc, bbc):
    bz, _, n, _ = q.shape
    ncx = qc.shape[2]

    def key(nn):
        return pl.BlockSpec((1, 1, nn, GLA_DK), lambda b, h: (b, h, 0, 0))

    def val(nn):
        return pl.BlockSpec((1, nn, GLA_DV), lambda b, h: (b, 0, h))

    return pl.pallas_call(
        _gla_kernel,
        grid=(bz, GLA_HEADS),
        in_specs=[key(n), key(n), val(n), key(n), key(n), key(ncx), key(ncx), val(ncx), key(ncx), key(ncx)],
        out_specs=[val(n)] * 2,
        out_shape=[jax.ShapeDtypeStruct((bz, n, GLA_WIDTH), F32)] * 2,
        compiler_params=_params("parallel", "arbitrary"),
        name="gla",
    )(q, k, v, bf, bb, qc, kc, vc, bfc, bbc)


def _head_rms(h, g, width):
    parts = [_rms(h[:, i:i + width]) for i in range(0, h.shape[1], width)]
    return jnp.concatenate(parts, axis=1) * g


def _largest16(loads):
    slot = lax.broadcasted_iota(jnp.int32, (PEER_TOPK, LANES), 0)

    def body(r, carry):
        out = []
        for (m_prev, vals), load in zip(carry, loads):
            s = load()
            m = jnp.max(jnp.where(s < m_prev, s, NEG_INF), axis=0, keepdims=True)
            out.append((m, jnp.where(slot == r, m, vals)))
        return tuple(out)

    init = tuple((jnp.full((1, LANES), jnp.inf, F32), jnp.zeros((PEER_TOPK, LANES), F32))
                 for _ in loads)
    return [vals for _, vals in lax.fori_loop(0, PEER_TOPK, body, init)]


def _count(mask):
    return jnp.sum(jnp.where(mask, 1.0, 0.0), axis=0, keepdims=True)


def _top16(s):
    nk, tl = s.shape
    slot = lax.broadcasted_iota(jnp.int32, (PEER_TOPK, tl), 0)
    key_id = lax.broadcasted_iota(jnp.int32, (nk, tl), 0).astype(F32)

    def body(r, carry):
        s, rank, vals = carry
        m = jnp.max(s, axis=0, keepdims=True)
        first = jnp.min(jnp.where(s == m, key_id, float(nk)), axis=0, keepdims=True)
        sel = key_id == first
        rank = jnp.where(sel, jnp.asarray(r).astype(F32), rank)
        s = jnp.where(sel, NEG_INF, s)
        vals = jnp.where(slot == r, m, vals)
        return s, rank, vals

    init = (s, jnp.full((nk, tl), float(PEER_TOPK), F32), jnp.zeros((PEER_TOPK, tl), F32))
    _, rank, vals = lax.fori_loop(0, PEER_TOPK, body, init)
    return vals, rank


def _pair_candidates(v1, v2):
    blocks = [v1[0:1] + v2[0:8], v1[0:1] + v2[8:16]]
    blocks += [v1[a:a + 1] + v2[0:8] for a in range(1, 8)]
    blocks += [v1[8:16] + v2[0:1]]
    return jnp.concatenate(blocks, axis=0)


def _pick16(cand):
    i = lax.broadcasted_iota(jnp.int32, cand.shape, 0)
    blk, rr = i // 8, i % 8
    flat = jnp.where(blk == 0, rr, jnp.where(blk == 1, 8 + rr,
                     jnp.where(blk <= 8, (blk - 1) * 16 + rr, (8 + rr) * 16))).astype(F32)

    def body(_, carry):
        cand, picked = carry
        m = jnp.max(cand, axis=0, keepdims=True)
        first = jnp.min(jnp.where(cand == m, flat, 1e9), axis=0, keepdims=True)
        sel = flat == first
        return jnp.where(sel, NEG_INF, cand), jnp.where(sel, 1.0, picked)

    _, picked = lax.fori_loop(0, PEER_TOPK, body, (cand, jnp.zeros_like(cand)))
    return picked


def _first_key_thresholds(picked, v1):
    inf = jnp.inf
    lo = jnp.where(picked[0:8] > 0.0, v1[0:1], inf)
    for a in range(1, 8):
        lo = jnp.minimum(lo, jnp.where(picked[8 * (a + 1):8 * (a + 2)] > 0.0, v1[a:a + 1], inf))
    tail = jnp.min(jnp.where(picked[72:80] > 0.0, v1[8:16], inf), axis=0, keepdims=True)
    row = lax.broadcasted_iota(jnp.int32, lo.shape, 0)
    lo = jnp.minimum(lo, jnp.where(row == 0, tail, inf))
    hi = jnp.where(picked[8:16] > 0.0, v1[0:1], inf)
    return jnp.concatenate([lo, hi], axis=0)


def _pair_counts(cand, picked):
    z = jnp.sum(picked * jnp.exp(cand - cand[0:1]), axis=0, keepdims=True)
    n_rows = [jnp.sum(picked[0:16], axis=0, keepdims=True)]
    n_rows += [jnp.sum(picked[8 * (a + 1):8 * (a + 2)], axis=0, keepdims=True) for a in range(1, 8)]
    return jnp.concatenate(n_rows + [picked[72:80]], axis=0), z


def _mix_kernel(x_ref, mhf_ref, mhb_ref, ghf_ref, ghb_ref, mo_ref, gr_ref, mlg_ref, glag_ref,
                wout_ref, g1_ref, sh2_ref, sc2_ref, n2g_ref, wq_ref, keys_ref,
                x1_ref, h2t_ref, e1_ref, l_ref, e2_ref, rk_ref, q_scr, s_scr):
    ml = _head_rms(mhf_ref[0] + mhb_ref[0], mlg_ref[...], ML_HEAD_DIM)
    ml = _sigmoid(mo_ref[0].astype(F32)) * ml
    gl = _head_rms(ghf_ref[0] + ghb_ref[0], glag_ref[...], GLA_DV)
    gr = gr_ref[0].astype(F32)
    gl = gr * _sigmoid(gr) * gl
    mix = jnp.concatenate([ml, gl], axis=1).astype(BF16)
    x1 = x_ref[0] + g1_ref[0] * _dot(mix, wout_ref[...])
    x1_ref[0] = x1
    h2 = _rms(x1) * n2g_ref[...] * (1.0 + sc2_ref[0]) + sh2_ref[0]
    h2t_ref[...] = h2.T.astype(BF16)
    qall = _dot(h2.astype(BF16), wq_ref[...]).astype(BF16)
    for j in range(2 * PEER_HEADS):
        q_scr[j] = qall[:, j * PEER_HALF:(j + 1) * PEER_HALF]

    tm = x_ref.shape[1]
    topk = float(PEER_TOPK)
    lane_chunks = [slice(i, i + LANES) for i in range(0, tm, LANES)]

    def write_tables(p, lanes, s1, s2, v1, v2, z, lim, rank2):
        e1_ref[p, :, lanes] = jnp.exp(s1 - v1[0:1]) / z
        l_ref[p, :, lanes] = lim
        e2 = jnp.exp(s2 - v2[0:1]).astype(BF16)
        e2_ref[p, :, :, lanes] = e2.reshape(PEER_NKEYS // 16, 16, LANES)
        rk_ref[p, :, :, lanes] = rank2.astype(BF16).reshape(PEER_NKEYS // 16, 16, LANES)

    def head_body(p, carry):
        for hf in range(2):
            s_scr[hf] = _dot_nt(keys_ref[2 * p + hf], q_scr[2 * p + hf])
        vals = _largest16([functools.partial(lambda hf, lanes: s_scr[hf, :, lanes], hf, lanes)
                           for hf in range(2) for lanes in lane_chunks])
        v1s, v2s = vals[:len(lane_chunks)], vals[len(lane_chunks):]
        cands = [_pair_candidates(v1, v2) for v1, v2 in zip(v1s, v2s)]
        taus = _largest16([functools.partial(lambda c: c, c) for c in cands])
        excess = jnp.zeros((1, LANES), F32)
        for lanes, v1, v2, cand, tau in zip(lane_chunks, v1s, v2s, cands, taus):
            s1, s2 = s_scr[0, :, lanes], s_scr[1, :, lanes]
            picked = jnp.where(cand >= tau[PEER_TOPK - 1:PEER_TOPK], 1.0, 0.0)
            _, z = _pair_counts(cand, picked)
            theta = _first_key_thresholds(picked, v1)
            lim = jnp.zeros_like(s1)
            rank2 = jnp.zeros_like(s2)
            for i in range(PEER_TOPK):
                lim = lim + jnp.where(s1 >= theta[i:i + 1], 1.0, 0.0)
                rank2 = rank2 + jnp.where(s2 < v2[i:i + 1], 1.0, 0.0)
            write_tables(p, lanes, s1, s2, v1, v2, z, lim, rank2)
            n_marked = (_count(s1 >= v1[PEER_TOPK - 1:PEER_TOPK]) + _count(rank2 < topk)
                        + jnp.sum(picked, axis=0, keepdims=True))
            excess = jnp.maximum(excess, n_marked - 3.0 * topk)

        @pl.when(jnp.max(excess) > 0.0)
        def _():
            for lanes in lane_chunks:
                s1, s2 = s_scr[0, :, lanes], s_scr[1, :, lanes]
                v1, rank1 = _top16(s1)
                v2, rank2 = _top16(s2)
                cand = _pair_candidates(v1, v2)
                n, z = _pair_counts(cand, _pick16(cand))
                lim = jnp.zeros_like(rank1)
                for a in range(PEER_TOPK):
                    lim = jnp.where(rank1 == float(a), n[a:a + 1], lim)
                write_tables(p, lanes, s1, s2, v1, v2, z, lim, rank2)

        return carry

    lax.fori_loop(0, PEER_HEADS, head_body, 0)


def _mix(x, mhf, mhb, ghf, ghb, mo, gr, mlg, glag, wout, g1, sh2, sc2, n2g, wq, keys):
    bz, n, d = x.shape
    tm = min(MIX_TM, n)
    assert n % tm == 0 and tm % LANES == 0
    nt = n // tm
    t_all = bz * n

    def tok(width):
        return pl.BlockSpec((1, tm, width), lambda b, i: (b, i, 0))

    def full(a):
        return pl.BlockSpec(a.shape, lambda b, i, nd=a.ndim: (0,) * nd)

    vec = pl.BlockSpec((1, 1, d), lambda b, i: (b, 0, 0))
    tab1 = pl.BlockSpec((PEER_HEADS, PEER_NKEYS, tm), lambda b, i: (0, 0, b * nt + i))
    tab2 = pl.BlockSpec((PEER_HEADS, PEER_NKEYS // 16, 16, tm), lambda b, i: (0, 0, 0, b * nt + i))
    tab1_shape = jax.ShapeDtypeStruct((PEER_HEADS, PEER_NKEYS, t_all), F32)
    tab2_shape = jax.ShapeDtypeStruct((PEER_HEADS, PEER_NKEYS // 16, 16, t_all), BF16)
    return pl.pallas_call(
        _mix_kernel,
        grid=(bz, nt),
        in_specs=[tok(d)] + [tok(ML_WIDTH)] * 6 + [full(mlg), full(glag), full(wout), vec, vec, vec,
                                                    full(n2g), full(wq), full(keys)],
        out_specs=[tok(d), pl.BlockSpec((d, tm), lambda b, i: (0, b * nt + i)), tab1, tab1, tab2, tab2],
        out_shape=[jax.ShapeDtypeStruct((bz, n, d), F32), jax.ShapeDtypeStruct((d, t_all), BF16),
                   tab1_shape, tab1_shape, tab2_shape, tab2_shape],
        scratch_shapes=[pltpu.VMEM((2 * PEER_HEADS, tm, PEER_HALF), BF16),
                        pltpu.VMEM((2, PEER_NKEYS, tm), F32)],
        compiler_params=_params("parallel", "arbitrary"),
        name="mix",
    )(x, mhf, mhb, ghf, ghb, mo, gr, mlg, glag, wout, g1, sh2, sc2, n2g, wq, keys)


def _peer_kernel(h2t_ref, u_ref, vt_ref, e1_ref, l_ref, e2_ref, rk_ref, x1_ref, g2_ref, nfg_ref,
                 out_ref, acc_ref):
    j = pl.program_id(2)
    te, tm = u_ref.shape[0], h2t_ref.shape[1]
    rows_per_step = te // PEER_NKEYS
    tiles = PEER_NKEYS // 16

    @pl.when(j == 0)
    def _():
        acc_ref[...] = jnp.zeros_like(acc_ref)

    act = _dot(u_ref[...], h2t_ref[...])
    act = 0.5 * act * (1.0 + lax.erf(act * 2.0 ** -0.5))
    act = act.astype(BF16).reshape(rows_per_step, tiles, 16, tm)
    parts = []
    for i in range(rows_per_step):
        w = None
        for p in range(PEER_HEADS):
            e1 = jnp.broadcast_to(e1_ref[p, i:i + 1, :], (16, tm)).astype(BF16)[None]
            lim = jnp.broadcast_to(l_ref[p, i:i + 1, :], (16, tm)).astype(BF16)[None]
            term = jnp.where(rk_ref[p] < lim, e2_ref[p] * e1, jnp.zeros((), BF16))
            w = term if w is None else w + term
        parts.append(w * act[i])
    gated = jnp.concatenate(parts, axis=0).reshape(te, tm)
    acc_ref[...] += _dot(vt_ref[...], gated)

    @pl.when(j == pl.num_programs(2) - 1)
    def _():
        xf = x1_ref[0] + g2_ref[0] * acc_ref[...].T
        out_ref[0] = _rms(xf) * nfg_ref[...]


def _peer(h2t, u, vt, e1, lim, e2, rk, x1, g2, nfg):
    bz, n, d = x1.shape
    n_exp = u.shape[0]
    tm = min(PEER_TM, n)
    te = PEER_TE
    assert n % tm == 0 and n_exp % te == 0 and te % PEER_NKEYS == 0
    nt = n // tm
    tab1 = pl.BlockSpec((PEER_HEADS, te // PEER_NKEYS, tm), lambda b, i, j: (0, j, b * nt + i))
    tab2 = pl.BlockSpec((PEER_HEADS, PEER_NKEYS // 16, 16, tm), lambda b, i, j: (0, 0, 0, b * nt + i))
    tok = pl.BlockSpec((1, tm, d), lambda b, i, j: (b, i, 0))
    return pl.pallas_call(
        _peer_kernel,
        grid=(bz, nt, n_exp // te),
        in_specs=[pl.BlockSpec((d, tm), lambda b, i, j: (0, b * nt + i)),
                  pl.BlockSpec((te, d), lambda b, i, j: (j, 0)),
                  pl.BlockSpec((d, te), lambda b, i, j: (0, j)),
                  tab1, tab1, tab2, tab2, tok,
                  pl.BlockSpec((1, 1, d), lambda b, i, j: (b, 0, 0)),
                  pl.BlockSpec((1, d), lambda b, i, j: (0, 0))],
        out_specs=tok,
        out_shape=jax.ShapeDtypeStruct((bz, n, d), F32),
        scratch_shapes=[pltpu.VMEM((d, tm), F32)],
        compiler_params=_params("parallel", "parallel", "arbitrary"),
        name="peer",
    )(h2t, u, vt, e1, lim, e2, rk, x1, g2, nfg)


def _inproj_weights(w_in, conv_w, gate_b, lr_w2, alpha_b):
    widths = (ML_WIDTH, ML_WIDTH, ML_WIDTH, ML_WIDTH, N_GATES,
              GLA_KEY_WIDTH, GLA_KEY_WIDTH, GLA_WIDTH, GLA_WIDTH, 2 * GLA_RANK)
    offs = [0]
    for w in widths:
        offs.append(offs[-1] + w)
    wb = w_in.astype(BF16)
    col = lambda a, b: wb[:, offs[a]:offs[b]]
    pad_lanes = lambda a: jnp.pad(a, ((0, 0), (0, LANES - a.shape[1])))
    w2 = jnp.zeros((LANES, 2 * GLA_KEY_WIDTH), F32)
    w2 = w2.at[:GLA_RANK, :GLA_KEY_WIDTH].set(lr_w2[0])
    w2 = w2.at[GLA_RANK:2 * GLA_RANK, GLA_KEY_WIDTH:].set(lr_w2[1])
    return (col(0, 2), col(2, 3), col(3, 4), pad_lanes(col(4, 5)), col(4, 5).T,
            col(5, 7), col(7, 8), col(8, 9), pad_lanes(col(9, 10)),
            conv_w, pad_lanes(gate_b[None, :]), gate_b[:, None],
            w2.astype(BF16), alpha_b.reshape(1, 2 * GLA_KEY_WIDTH))


def _per_head_gates(gtok, gmaj):
    bz, n, _ = gtok.shape
    gt = gtok.reshape(bz, n, 2, 2, ML_HEADS).transpose(0, 4, 1, 2, 3).reshape(bz, ML_HEADS, n, 4)
    gm = gmaj.reshape(bz, 2, 2, ML_HEADS, n // ML_CHUNK, ML_CHUNK)
    gm = gm.transpose(0, 3, 4, 1, 2, 5).reshape(bz, ML_HEADS, n // ML_CHUNK, 4, ML_CHUNK)
    return gt, gm


def _per_head_keys(a):
    bz, n, _ = a.shape
    return a.reshape(bz, n, GLA_HEADS, GLA_DK).transpose(0, 2, 1, 3)


def _token_mix_inputs(x, sh, sc, ng, wts, row_len):
    mq, mk, mv, mo, gtok, gmaj, gq, gk, gv, gr, bg = _inproj(x, sh, sc, ng, wts, row_len)
    gt, gm = _per_head_gates(gtok, gmaj)
    ml = (mq, mk, mv, gt, gm)
    gla = (_per_head_keys(gq), _per_head_keys(gk), gv,
           _per_head_keys(bg[..., :GLA_KEY_WIDTH]), _per_head_keys(bg[..., GLA_KEY_WIDTH:]))
    return ml, gla, mo, gr


def kernel(x, c, ctx, c_ctx, w_mod, b_mod, norm1_g, w_in, ml_conv_w, ml_gate_b, ml_norm_g,
           gla_lr_w2, gla_alpha_b, gla_norm_g, w_out, norm2_g, peer_wq, peer_keys, peer_u,
           peer_v, norm_f_g):
    assert w_mod.shape[0] == 1, "single trunk layer"
    bz, n, d = x.shape
    c_all = jnp.concatenate([c, c_ctx[None, :]], axis=0)
    c_all = jnp.pad(c_all, ((0, (-c_all.shape[0]) % SUBLANES), (0, 0)))
    mod = _modulation(c_all, w_mod[0], b_mod[0][None, :])
    sh1, sc1, g1, sh2, sc2, g2 = [m[:, None, :] for m in jnp.split(mod[:bz], 6, axis=1)]
    mod_c = mod[bz]
    csh1, csc1 = mod_c[None, None, :d], mod_c[None, None, d:2 * d]

    ng1 = norm1_g[0][None, :]
    wts = _inproj_weights(w_in[0], ml_conv_w[0], ml_gate_b[0], gla_lr_w2[0], gla_alpha_b[0])
    ml_lat, gla_lat, mo, gr = _token_mix_inputs(x, sh1, sc1, ng1, wts, GRID_W)
    ml_ctx, gla_ctx, _, _ = _token_mix_inputs(ctx, csh1, csc1, ng1, wts, ctx.shape[1])
    mhf, mhb = _mlstm(*ml_lat, *ml_ctx)
    ghf, ghb = _gla(*gla_lat, *gla_ctx)

    keys = peer_keys[0].reshape(2 * PEER_HEADS, PEER_NKEYS, PEER_HALF).astype(BF16)
    x1, h2t, e1, lim, e2, rk = _mix(
        x, mhf, mhb, ghf, ghb, mo, gr, ml_norm_g[0][None, :], gla_norm_g[0][None, :],
        w_out[0].astype(BF16), g1, sh2, sc2, norm2_g[0][None, :], peer_wq[0].astype(BF16), keys)
    return _peer(h2t, peer_u[0].astype(BF16), peer_v[0].astype(BF16).T, e1, lim, e2, rk,
                 x1, g2, norm_f_g[None, :])
```

```python
import functools

import jax
import jax.numpy as jnp
from jax import lax
from jax.experimental import pallas as pl
from jax.experimental.pallas import tpu as pltpu

F32 = jnp.float32
BF16 = jnp.bfloat16

EPS = 1e-6
D_MODEL = 1024
GRID_W = 64

ML_HEADS = 4
ML_HEAD_DIM = 128
ML_WIDTH = ML_HEADS * ML_HEAD_DIM
ML_CHUNK = 128
GLA_HEADS = 4
GLA_DK = 64
GLA_DV = 128
GLA_KEY_WIDTH = GLA_HEADS * GLA_DK
GLA_WIDTH = GLA_HEADS * GLA_DV
GLA_RANK = 16
GLA_TAU = 16.0
GLA_CHUNK = 64
N_GATES = 4 * ML_HEADS

PEER_HEADS = 8
PEER_NKEYS = 128
PEER_TOPK = 16
PEER_HALF = 128

LANES = 128
SUBLANES = 8
VMEM_LIMIT = 56 * 1024 * 1024

INPROJ_TM = 256
ML_HEADS_PER_STEP = 4
GLA_HEADS_PER_STEP = 2
MIX_TM = 256
PEER_TM = 512
PEER_TE = 2048

NEG_INF = float("-inf")


def _params(*sem):
    return pltpu.CompilerParams(dimension_semantics=sem, vmem_limit_bytes=VMEM_LIMIT)


def _dot(a, b):
    return jnp.dot(a, b, preferred_element_type=F32)


def _dot_nt(a, b):
    return lax.dot_general(a, b, (((1,), (1,)), ((), ())), preferred_element_type=F32)


def _split3(x):
    hi = x.astype(BF16)
    r1 = x - hi.astype(F32)
    mid = r1.astype(BF16)
    lo = (r1 - mid.astype(F32)).astype(BF16)
    return hi, mid, lo


def _dot_exact_rhs(a01, x):
    hi, mid, lo = _split3(x)
    return _dot(a01, hi) + _dot(a01, mid) + _dot(a01, lo)


def _dot_exact_lhs(x, a01):
    hi, mid, lo = _split3(x)
    return _dot(hi, a01) + _dot(mid, a01) + _dot(lo, a01)


def _sigmoid(x):
    return 1.0 / (1.0 + jnp.exp(-x))


def _log_sigmoid(x):
    return jnp.minimum(x, 0.0) - jnp.log(1.0 + jnp.exp(-jnp.abs(x)))


def _rms(x):
    return x * lax.rsqrt(jnp.mean(x * x, axis=-1, keepdims=True) + EPS)


def _mod_kernel(c_ref, w_ref, b_ref, o_ref):
    cond = c_ref[...]
    cond = cond * _sigmoid(cond)
    ch, cl = cond.astype(BF16), (cond - cond.astype(BF16).astype(F32)).astype(BF16)
    w = w_ref[...]
    wh = w.astype(BF16)
    wl = (w - wh.astype(F32)).astype(BF16)
    o_ref[...] = _dot(ch, wh) + _dot(ch, wl) + _dot(cl, wh) + b_ref[...]


def _modulation(c_all, w_mod, b_mod):
    rows, d = c_all.shape
    n_out = w_mod.shape[1]
    tn = 512
    return pl.pallas_call(
        _mod_kernel,
        grid=(n_out // tn,),
        in_specs=[pl.BlockSpec((rows, d), lambda j: (0, 0)),
                  pl.BlockSpec((d, tn), lambda j: (0, j)),
                  pl.BlockSpec((1, tn), lambda j: (0, j))],
        out_specs=pl.BlockSpec((rows, tn), lambda j: (0, j)),
        out_shape=jax.ShapeDtypeStruct((rows, n_out), F32),
        compiler_params=_params("arbitrary"),
        name="mod",
    )(c_all, w_mod, b_mod)


def _chunk_masks(tm, chunk):
    r = lax.broadcasted_iota(jnp.int32, (tm, tm), 0)
    c = lax.broadcasted_iota(jnp.int32, (tm, tm), 1)
    same = (r // chunk) == (c // chunk)
    prefix = jnp.where(same & (c <= r), 1.0, 0.0).astype(BF16)
    suffix = jnp.where(same & (c >= r), 1.0, 0.0).astype(BF16)
    return prefix, suffix


def _inproj_kernel(x_ref, sh_ref, sc_ref, ng_ref, wqk_ref, wv_ref, wo_ref, wg_ref, wgt_ref,
                   wgqk_ref, wgv_ref, wgr_ref, wlr_ref, conv_ref, gb_ref, gbc_ref, w2_ref, ab_ref,
                   mq_ref, mk_ref, mv_ref, mo_ref, gtok_ref, gmaj_ref,
                   gq_ref, gk_ref, gv_ref, gr_ref, bg_ref, mkt_ref, gvt_ref, *, row_len):
    tm = x_ref.shape[1]
    x = x_ref[0]
    h = _rms(x) * ng_ref[...] * (1.0 + sc_ref[0]) + sh_ref[0]
    hb = h.astype(BF16)

    u = _dot(hb, wqk_ref[...])
    pos = lax.broadcasted_iota(jnp.int32, (tm, 1), 0) % row_len
    up = jnp.where(pos == 0, 0.0, pltpu.roll(u, 1, 0))
    dn = jnp.where(pos == row_len - 1, 0.0, pltpu.roll(u, tm - 1, 0))
    cw = conv_ref[...]
    y = cw[0:1] * up + cw[1:2] * u + cw[2:3] * dn
    y = y * _sigmoid(y)
    mq_ref[0] = y[:, :ML_WIDTH].astype(BF16)
    mk = y[:, ML_WIDTH:] * ML_HEAD_DIM ** -0.5
    mk_ref[0] = mk.astype(BF16)
    mkt_ref[0] = mk.T.astype(BF16)
    mv_ref[0] = _dot(hb, wv_ref[...]).astype(BF16)
    mo_ref[0] = _dot(hb, wo_ref[...]).astype(BF16)

    pre128, suf128 = _chunk_masks(tm, ML_CHUNK)
    g = _dot(hb, wg_ref[...]) + gb_ref[...]
    col = lax.broadcasted_iota(jnp.int32, (1, LANES), 1)
    is_f = ((col % 8) >= 4) & (col < N_GATES)
    is_bwd = col >= 8
    lf = jnp.where(is_f, _log_sigmoid(g), 0.0)
    cum = jnp.where(is_bwd, _dot_exact_rhs(suf128, lf), _dot_exact_rhs(pre128, lf))
    gtok_ref[0] = jnp.where(is_f, cum, g)[:, :N_GATES]

    gt = _dot_nt(wgt_ref[...], hb) + gbc_ref[...]
    row = lax.broadcasted_iota(jnp.int32, (N_GATES, 1), 0)
    is_f_r = (row % 8) >= 4
    lft = jnp.where(is_f_r, _log_sigmoid(gt), 0.0)
    cum_t = jnp.where(row >= 8, _dot_exact_lhs(lft, pre128), _dot_exact_lhs(lft, suf128))
    gmaj_ref[0] = jnp.where(is_f_r, cum_t, gt)

    gqk = _dot(hb, wgqk_ref[...])
    gq_ref[0] = (gqk[:, :GLA_KEY_WIDTH] * GLA_DK ** -0.5).astype(BF16)
    gk_ref[0] = gqk[:, GLA_KEY_WIDTH:].astype(BF16)
    gv = _dot(hb, wgv_ref[...])
    gv_ref[0] = gv.astype(BF16)
    gvt_ref[0] = gv.T.astype(BF16)
    gr_ref[0] = _dot(hb, wgr_ref[...]).astype(BF16)
    glr = _dot(hb, wlr_ref[...]).astype(BF16)
    alpha = _dot(glr, w2_ref[...]) + ab_ref[...]
    la = _log_sigmoid(alpha) * (1.0 / GLA_TAU)
    pre64, suf64 = _chunk_masks(tm, GLA_CHUNK)
    bg_ref[0, :, :GLA_KEY_WIDTH] = _dot_exact_rhs(pre64, la[:, :GLA_KEY_WIDTH])
    bg_ref[0, :, GLA_KEY_WIDTH:] = _dot_exact_rhs(suf64, la[:, GLA_KEY_WIDTH:])


def _inproj(x, sh, sc, ng, wts, row_len):
    bz, n, d = x.shape
    tm = min(INPROJ_TM, n)
    assert n % tm == 0 and tm % row_len == 0 and tm % ML_CHUNK == 0
    grid = (bz, n // tm)
    per_b = (lambda b, i: (b, 0, 0)) if sh.shape[0] == bz else (lambda b, i: (0, 0, 0))

    def full(a):
        return pl.BlockSpec(a.shape, lambda b, i, nd=a.ndim: (0,) * nd)

    def tok(width):
        return pl.BlockSpec((1, tm, width), lambda b, i: (b, i, 0))

    def out(width, dtype):
        return jax.ShapeDtypeStruct((bz, n, width), dtype)

    in_specs = [tok(d), pl.BlockSpec((1, 1, d), per_b), pl.BlockSpec((1, 1, d), per_b), full(ng)]
    in_specs += [full(w) for w in wts]
    out_specs = [tok(ML_WIDTH)] * 4 + [tok(N_GATES), pl.BlockSpec((1, N_GATES, tm), lambda b, i: (b, 0, i))]
    out_specs += [tok(GLA_KEY_WIDTH)] * 2 + [tok(GLA_WIDTH)] * 2 + [tok(2 * GLA_KEY_WIDTH)]
    out_shape = [out(ML_WIDTH, BF16)] * 4 + [out(N_GATES, F32),
                                              jax.ShapeDtypeStruct((bz, N_GATES, n), F32)]
    out_shape += [out(GLA_KEY_WIDTH, BF16)] * 2 + [out(GLA_WIDTH, BF16)] * 2 + [out(2 * GLA_KEY_WIDTH, F32)]
    out_specs += [pl.BlockSpec((1, ML_WIDTH, tm), lambda b, i: (b, 0, i))] * 2
    out_shape += [jax.ShapeDtypeStruct((bz, ML_WIDTH, n), BF16)] * 2
    return pl.pallas_call(
        functools.partial(_inproj_kernel, row_len=row_len),
        grid=grid, in_specs=in_specs, out_specs=out_specs, out_shape=out_shape,
        compiler_params=_params("parallel", "arbitrary"),
        name="inproj",
    )(x, sh, sc, ng, *wts)


def _bmm(a, b):
    return lax.dot_general(a, b, (((2,), (1,)), ((0,), (0,))), preferred_element_type=F32)


def _bmm_nt(a, b):
    return lax.dot_general(a, b, (((2,), (2,)), ((0,), (0,))), preferred_element_type=F32)


def _by_direction(x, fwd, bwd):
    half = x.shape[0] // 2
    return jnp.concatenate([fwd(x[:half]), bwd(x[half:])], axis=0)


def _ml_chunks(q, k, kt, v, gt, gm, masks, state, want_out):
    ct, nv, m = state
    ll = q.shape[1]
    ig_col = _by_direction(gt, lambda g: g[:, :, 0:1], lambda g: g[:, :, 2:3])
    b_col = _by_direction(gt, lambda g: g[:, :, 1:2], lambda g: g[:, :, 3:4])
    ig_row = _by_direction(gm, lambda g: g[:, 0:1, :], lambda g: g[:, 2:3, :])
    b_row = _by_direction(gm, lambda g: g[:, 1:2, :], lambda g: g[:, 3:4, :])
    b_end = _by_direction(b_row, lambda b: b[:, :, ll - 1:ll], lambda b: b[:, :, 0:1])
    h = None
    if want_out:
        d_log = _by_direction(b_col - b_row + ig_row,
                              lambda x: jnp.where(masks[0], x, NEG_INF),
                              lambda x: jnp.where(masks[1], x, NEG_INF))
        inter_log = b_col + m
        m_t = jnp.maximum(inter_log, jnp.max(d_log, axis=2, keepdims=True))
        scores = _bmm_nt(q, k) * jnp.exp(d_log - m_t)
        inter = jnp.exp(inter_log - m_t)
        qf = q.astype(F32)
        num = _bmm(scores.astype(BF16), v) + inter * _bmm(q, ct.astype(BF16))
        den = jnp.sum(scores, axis=2, keepdims=True) + inter * jnp.sum(qf * nv, axis=2, keepdims=True)
        h = num / jnp.maximum(jnp.abs(den), jnp.exp(-m_t))
    m_loc = jnp.max(b_end - b_row + ig_row, axis=2, keepdims=True)
    w_col = jnp.exp(b_end - b_col + ig_col - m_loc)
    c_loc = _bmm(kt, (w_col * v.astype(F32)).astype(BF16))
    n_loc = jnp.sum(w_col * k.astype(F32), axis=1, keepdims=True)
    m_new = jnp.maximum(b_end + m, m_loc)
    a = jnp.exp(b_end + m - m_new)
    bb = jnp.exp(m_loc - m_new)
    return h, (a * ct + bb * c_loc, a * nv + bb * n_loc, m_new)


def _mlstm_kernel(q_ref, k_ref, kt_ref, v_ref, gt_ref, gm_ref,
                  qc_ref, kc_ref, ktc_ref, vc_ref, gtc_ref, gmc_ref, hf_ref, hb_ref):
    ll = ML_CHUNK
    nc = q_ref.shape[1] // ll
    ncc = qc_ref.shape[1] // ll
    r = lax.broadcasted_iota(jnp.int32, (ll, ll), 0)
    c = lax.broadcasted_iota(jnp.int32, (ll, ll), 1)
    masks = (c <= r, c >= r)

    heads = gt_ref.shape[1]
    head_cols = [slice(hh * ML_HEAD_DIM, (hh + 1) * ML_HEAD_DIM) for hh in range(heads)]

    def rows_of(ci):
        return pl.ds(pl.multiple_of(ci * ll, ll), ll)

    def load(refs, cf, cb):
        qr, kr, ktr, vr, gtr, gmr = refs
        sites = [(hh, ci) for ci in (cf, cb) for hh in range(heads)]
        seq = lambda r: jnp.stack([r[0, rows_of(ci), head_cols[hh]] for hh, ci in sites])
        return (seq(qr), seq(kr), jnp.stack([ktr[0, hh, ci] for hh, ci in sites]), seq(vr),
                jnp.stack([gtr[0, hh, rows_of(ci), :] for hh, ci in sites]),
                jnp.stack([gmr[0, hh, ci] for hh, ci in sites]))

    chains = 2 * heads
    zero = (jnp.zeros((chains, ML_HEAD_DIM, ML_HEAD_DIM), F32), jnp.zeros((chains, 1, ML_HEAD_DIM), F32),
            jnp.zeros((chains, 1, 1), F32))
    ctx_refs = (qc_ref, kc_ref, ktc_ref, vc_ref, gtc_ref, gmc_ref)
    lat_refs = (q_ref, k_ref, kt_ref, v_ref, gt_ref, gm_ref)

    def ctx_body(i, state):
        _, state = _ml_chunks(*load(ctx_refs, i, ncc - 1 - i), masks, state, False)
        return state

    def lat_body(i, state):
        cb = nc - 1 - i
        h, state = _ml_chunks(*load(lat_refs, i, cb), masks, state, True)
        for hh in range(heads):
            hf_ref[0, rows_of(i), head_cols[hh]] = h[hh]
            hb_ref[0, rows_of(cb), head_cols[hh]] = h[heads + hh]
        return state

    state = lax.fori_loop(0, ncc, ctx_body, zero)
    lax.fori_loop(0, nc, lat_body, state)


def _mlstm(q, k, kt, v, gt, gm, qc, kc, ktc, vc, gtc, gmc):
    bz, n, _ = q.shape
    ncx = qc.shape[1]
    hps = ML_HEADS_PER_STEP

    def seq(nn):
        return pl.BlockSpec((1, nn, hps * ML_HEAD_DIM), lambda b, h: (b, 0, h))

    def ktr(nn):
        return pl.BlockSpec((1, hps, nn // ML_CHUNK, ML_HEAD_DIM, ML_CHUNK), lambda b, h: (b, h, 0, 0, 0))

    def gtok(nn):
        return pl.BlockSpec((1, hps, nn, 4), lambda b, h: (b, h, 0, 0))

    def gmaj(nn):
        return pl.BlockSpec((1, hps, nn // ML_CHUNK, 4, ML_CHUNK), lambda b, h: (b, h, 0, 0, 0))

    def specs(nn):
        return [seq(nn), seq(nn), ktr(nn), seq(nn), gtok(nn), gmaj(nn)]

    return pl.pallas_call(
        _mlstm_kernel,
        grid=(bz, ML_HEADS // hps),
        in_specs=specs(n) + specs(ncx),
        out_specs=[seq(n)] * 2,
        out_shape=[jax.ShapeDtypeStruct((bz, n, ML_WIDTH), F32)] * 2,
        compiler_params=_params("parallel", "arbitrary"),
        name="mlstm",
    )(q, k, kt, v, gt, gm, qc, kc, ktc, vc, gtc, gmc)


def _gla_chunks(q, k, v, vt, b, masks, st, want_out):
    ll = q.shape[1]
    qf, kf = q.astype(F32), k.astype(F32)
    ref = _by_direction(b, lambda x: x[:, ll // 2:ll // 2 + 1], lambda x: x[:, ll // 2 - 1:ll // 2])
    b_end = _by_direction(b, lambda x: x[:, ll - 1:ll], lambda x: x[:, 0:1])
    o = None
    if want_out:
        att = _bmm_nt((qf * jnp.exp(b - ref)).astype(BF16), (kf * jnp.exp(ref - b)).astype(BF16))
        att = _by_direction(att, lambda x: jnp.where(masks[0], x, 0.0), lambda x: jnp.where(masks[1], x, 0.0))
        o = _bmm(att.astype(BF16), v) + _bmm_nt((qf * jnp.exp(b)).astype(BF16), st.astype(BF16))
    s_loc = _bmm(vt, (kf * jnp.exp(b_end - b)).astype(BF16))
    return o, jnp.exp(b_end) * st + s_loc


def _gla_kernel(q_ref, k_ref, v_ref, vt_ref, bf_ref, bb_ref,
                qc_ref, kc_ref, vc_ref, vtc_ref, bfc_ref, bbc_ref, of_ref, ob_ref):
    ll = GLA_CHUNK
    nc = q_ref.shape[2] // ll
    ncc = qc_ref.shape[2] // ll
    r = lax.broadcasted_iota(jnp.int32, (ll, ll), 0)
    c = lax.broadcasted_iota(jnp.int32, (ll, ll), 1)
    masks = (c <= r, c >= r)

    heads = q_ref.shape[1]

    def rows_of(ci):
        return pl.ds(pl.multiple_of(ci * ll, ll), ll)

    head_cols = [slice(hh * GLA_DV, (hh + 1) * GLA_DV) for hh in range(heads)]

    def load(refs, cf, cb):
        qr, kr, vr, vtr, bfr, bbr = refs
        sites = [(hh, ci) for ci in (cf, cb) for hh in range(heads)]
        per_head = lambda r: jnp.stack([r[0, hh, rows_of(ci), :] for hh, ci in sites])
        return (per_head(qr), per_head(kr),
                jnp.stack([vr[0, rows_of(ci), head_cols[hh]] for hh, ci in sites]),
                jnp.stack([vtr[0, hh, ci] for hh, ci in sites]),
                jnp.stack([bfr[0, hh, rows_of(cf), :] for hh in range(heads)]
                          + [bbr[0, hh, rows_of(cb), :] for hh in range(heads)]))

    zero = jnp.zeros((2 * heads, GLA_DV, GLA_DK), F32)
    ctx_refs = (qc_ref, kc_ref, vc_ref, vtc_ref, bfc_ref, bbc_ref)
    lat_refs = (q_ref, k_ref, v_ref, vt_ref, bf_ref, bb_ref)

    def ctx_body(i, st):
        _, st = _gla_chunks(*load(ctx_refs, i, ncc - 1 - i), masks, st, False)
        return st

    def lat_body(i, st):
        cb = nc - 1 - i
        o, st = _gla_chunks(*load(lat_refs, i, cb), masks, st, True)
        for hh in range(heads):
            of_ref[0, rows_of(i), head_cols[hh]] = o[hh]
            ob_ref[0, rows_of(cb), head_cols[hh]] = o[heads + hh]
        return st

    st = lax.fori_loop(0, ncc, ctx_body, zero)
    lax.fori_loop(0, nc, lat_body, st)


def _gla(q, k, v, vt, bf, bb, qc, kc, vc, vtc, bfc, bbc):
    bz, _, n, _ = q.shape
    ncx = qc.shape[2]
    hps = GLA_HEADS_PER_STEP

    def key(nn):
        return pl.BlockSpec((1, hps, nn, GLA_DK), lambda b, h: (b, h, 0, 0))

    def val(nn):
        return pl.BlockSpec((1, nn, hps * GLA_DV), lambda b, h: (b, 0, h))

    def valt(nn):
        return pl.BlockSpec((1, hps, nn // GLA_CHUNK, GLA_DV, GLA_CHUNK), lambda b, h: (b, h, 0, 0, 0))

    def specs(nn):
        return [key(nn), key(nn), val(nn), valt(nn), key(nn), key(nn)]

    return pl.pallas_call(
        _gla_kernel,
        grid=(bz, GLA_HEADS // hps),
        in_specs=specs(n) + specs(ncx),
        out_specs=[val(n)] * 2,
        out_shape=[jax.ShapeDtypeStruct((bz, n, GLA_WIDTH), F32)] * 2,
        compiler_params=_params("parallel", "arbitrary"),
        name="gla",
    )(q, k, v, vt, bf, bb, qc, kc, vc, vtc, bfc, bbc)


def _head_rms(h, g, width):
    parts = [_rms(h[:, i:i + width]) for i in range(0, h.shape[1], width)]
    return jnp.concatenate(parts, axis=1) * g


def _largest16(loads):
    slot = lax.broadcasted_iota(jnp.int32, (PEER_TOPK, LANES), 0)

    def body(r, carry):
        out = []
        for (m_prev, vals), load in zip(carry, loads):
            s = load()
            m = jnp.max(jnp.where(s < m_prev, s, NEG_INF), axis=0, keepdims=True)
            out.append((m, jnp.where(slot == r, m, vals)))
        return tuple(out)

    init = tuple((jnp.full((1, LANES), jnp.inf, F32), jnp.zeros((PEER_TOPK, LANES), F32))
                 for _ in loads)
    return [vals for _, vals in lax.fori_loop(0, PEER_TOPK, body, init)]


def _count(mask):
    return jnp.sum(jnp.where(mask, 1.0, 0.0), axis=0, keepdims=True)


def _top16(s):
    nk, tl = s.shape
    slot = lax.broadcasted_iota(jnp.int32, (PEER_TOPK, tl), 0)
    key_id = lax.broadcasted_iota(jnp.int32, (nk, tl), 0).astype(F32)

    def body(r, carry):
        s, rank, vals = carry
        m = jnp.max(s, axis=0, keepdims=True)
        first = jnp.min(jnp.where(s == m, key_id, float(nk)), axis=0, keepdims=True)
        sel = key_id == first
        rank = jnp.where(sel, jnp.asarray(r).astype(F32), rank)
        s = jnp.where(sel, NEG_INF, s)
        vals = jnp.where(slot == r, m, vals)
        return s, rank, vals

    init = (s, jnp.full((nk, tl), float(PEER_TOPK), F32), jnp.zeros((PEER_TOPK, tl), F32))
    _, rank, vals = lax.fori_loop(0, PEER_TOPK, body, init)
    return vals, rank


def _pair_candidates(v1, v2):
    blocks = [v1[0:1] + v2[0:8], v1[0:1] + v2[8:16]]
    blocks += [v1[a:a + 1] + v2[0:8] for a in range(1, 8)]
    blocks += [v1[8:16] + v2[0:1]]
    return jnp.concatenate(blocks, axis=0)


def _pick16(cand):
    i = lax.broadcasted_iota(jnp.int32, cand.shape, 0)
    blk, rr = i // 8, i % 8
    flat = jnp.where(blk == 0, rr, jnp.where(blk == 1, 8 + rr,
                     jnp.where(blk <= 8, (blk - 1) * 16 + rr, (8 + rr) * 16))).astype(F32)

    def body(_, carry):
        cand, picked = carry
        m = jnp.max(cand, axis=0, keepdims=True)
        first = jnp.min(jnp.where(cand == m, flat, 1e9), axis=0, keepdims=True)
        sel = flat == first
        return jnp.where(sel, NEG_INF, cand), jnp.where(sel, 1.0, picked)

    _, picked = lax.fori_loop(0, PEER_TOPK, body, (cand, jnp.zeros_like(cand)))
    return picked


def _first_key_thresholds(picked, v1):
    inf = jnp.inf
    lo = jnp.where(picked[0:8] > 0.0, v1[0:1], inf)
    for a in range(1, 8):
        lo = jnp.minimum(lo, jnp.where(picked[8 * (a + 1):8 * (a + 2)] > 0.0, v1[a:a + 1], inf))
    tail = jnp.min(jnp.where(picked[72:80] > 0.0, v1[8:16], inf), axis=0, keepdims=True)
    row = lax.broadcasted_iota(jnp.int32, lo.shape, 0)
    lo = jnp.minimum(lo, jnp.where(row == 0, tail, inf))
    hi = jnp.where(picked[8:16] > 0.0, v1[0:1], inf)
    return jnp.concatenate([lo, hi], axis=0)


def _pair_counts(cand, picked):
    z = jnp.sum(picked * jnp.exp(cand - cand[0:1]), axis=0, keepdims=True)
    n_rows = [jnp.sum(picked[0:16], axis=0, keepdims=True)]
    n_rows += [jnp.sum(picked[8 * (a + 1):8 * (a + 2)], axis=0, keepdims=True) for a in range(1, 8)]
    return jnp.concatenate(n_rows + [picked[72:80]], axis=0), z


def _mix_kernel(x_ref, mhf_ref, mhb_ref, ghf_ref, ghb_ref, mo_ref, gr_ref, mlg_ref, glag_ref,
                wout_ref, g1_ref, sh2_ref, sc2_ref, n2g_ref, wq_ref, keys_ref,
                x1_ref, h2t_ref, e1_ref, l_ref, e2_ref, rk_ref, q_scr, s_scr):
    ml = _head_rms(mhf_ref[0] + mhb_ref[0], mlg_ref[...], ML_HEAD_DIM)
    ml = _sigmoid(mo_ref[0].astype(F32)) * ml
    gl = _head_rms(ghf_ref[0] + ghb_ref[0], glag_ref[...], GLA_DV)
    gr = gr_ref[0].astype(F32)
    gl = gr * _sigmoid(gr) * gl
    mix = jnp.concatenate([ml, gl], axis=1).astype(BF16)
    x1 = x_ref[0] + g1_ref[0] * _dot(mix, wout_ref[...])
    x1_ref[0] = x1
    h2 = _rms(x1) * n2g_ref[...] * (1.0 + sc2_ref[0]) + sh2_ref[0]
    h2t_ref[...] = h2.T.astype(BF16)
    qall = _dot(h2.astype(BF16), wq_ref[...]).astype(BF16)
    for j in range(2 * PEER_HEADS):
        q_scr[j] = qall[:, j * PEER_HALF:(j + 1) * PEER_HALF]

    tm = x_ref.shape[1]
    topk = float(PEER_TOPK)
    lane_chunks = [slice(i, i + LANES) for i in range(0, tm, LANES)]

    def write_tables(p, lanes, s1, s2, v1, v2, z, lim, rank2):
        e1_ref[p, :, lanes] = jnp.exp(s1 - v1[0:1]) / z
        l_ref[p, :, lanes] = lim
        e2 = jnp.exp(s2 - v2[0:1]).astype(BF16)
        e2_ref[p, :, :, lanes] = e2.reshape(PEER_NKEYS // 16, 16, LANES)
        rk_ref[p, :, :, lanes] = rank2.astype(BF16).reshape(PEER_NKEYS // 16, 16, LANES)

    def head_body(p, carry):
        for hf in range(2):
            s_scr[hf] = _dot_nt(keys_ref[2 * p + hf], q_scr[2 * p + hf])
        vals = _largest16([functools.partial(lambda hf, lanes: s_scr[hf, :, lanes], hf, lanes)
                           for hf in range(2) for lanes in lane_chunks])
        v1s, v2s = vals[:len(lane_chunks)], vals[len(lane_chunks):]
        cands = [_pair_candidates(v1, v2) for v1, v2 in zip(v1s, v2s)]
        taus = _largest16([functools.partial(lambda c: c, c) for c in cands])
        excess = jnp.zeros((1, LANES), F32)
        for lanes, v1, v2, cand, tau in zip(lane_chunks, v1s, v2s, cands, taus):
            s1, s2 = s_scr[0, :, lanes], s_scr[1, :, lanes]
            picked = jnp.where(cand >= tau[PEER_TOPK - 1:PEER_TOPK], 1.0, 0.0)
            _, z = _pair_counts(cand, picked)
            theta = _first_key_thresholds(picked, v1)
            lim = jnp.zeros_like(s1)
            rank2 = jnp.zeros_like(s2)
            for i in range(PEER_TOPK):
                lim = lim + jnp.where(s1 >= theta[i:i + 1], 1.0, 0.0)
                rank2 = rank2 + jnp.where(s2 < v2[i:i + 1], 1.0, 0.0)
            write_tables(p, lanes, s1, s2, v1, v2, z, lim, rank2)
            n_marked = (_count(s1 >= v1[PEER_TOPK - 1:PEER_TOPK]) + _count(rank2 < topk)
                        + jnp.sum(picked, axis=0, keepdims=True))
            excess = jnp.maximum(excess, n_marked - 3.0 * topk)

        @pl.when(jnp.max(excess) > 0.0)
        def _():
            for lanes in lane_chunks:
                s1, s2 = s_scr[0, :, lanes], s_scr[1, :, lanes]
                v1, rank1 = _top16(s1)
                v2, rank2 = _top16(s2)
                cand = _pair_candidates(v1, v2)
                n, z = _pair_counts(cand, _pick16(cand))
                lim = jnp.zeros_like(rank1)
                for a in range(PEER_TOPK):
                    lim = jnp.where(rank1 == float(a), n[a:a + 1], lim)
                write_tables(p, lanes, s1, s2, v1, v2, z, lim, rank2)

        return carry

    lax.fori_loop(0, PEER_HEADS, head_body, 0)


def _mix(x, mhf, mhb, ghf, ghb, mo, gr, mlg, glag, wout, g1, sh2, sc2, n2g, wq, keys):
    bz, n, d = x.shape
    tm = min(MIX_TM, n)
    assert n % tm == 0 and tm % LANES == 0
    nt = n // tm
    t_all = bz * n

    def tok(width):
        return pl.BlockSpec((1, tm, width), lambda b, i: (b, i, 0))

    def full(a):
        return pl.BlockSpec(a.shape, lambda b, i, nd=a.ndim: (0,) * nd)

    vec = pl.BlockSpec((1, 1, d), lambda b, i: (b, 0, 0))
    tab1 = pl.BlockSpec((PEER_HEADS, PEER_NKEYS, tm), lambda b, i: (0, 0, b * nt + i))
    tab2 = pl.BlockSpec((PEER_HEADS, PEER_NKEYS // 16, 16, tm), lambda b, i: (0, 0, 0, b * nt + i))
    tab1_shape = jax.ShapeDtypeStruct((PEER_HEADS, PEER_NKEYS, t_all), F32)
    tab2_shape = jax.ShapeDtypeStruct((PEER_HEADS, PEER_NKEYS // 16, 16, t_all), BF16)
    return pl.pallas_call(
        _mix_kernel,
        grid=(bz, nt),
        in_specs=[tok(d)] + [tok(ML_WIDTH)] * 6 + [full(mlg), full(glag), full(wout), vec, vec, vec,
                                                    full(n2g), full(wq), full(keys)],
        out_specs=[tok(d), pl.BlockSpec((d, tm), lambda b, i: (0, b * nt + i)), tab1, tab1, tab2, tab2],
        out_shape=[jax.ShapeDtypeStruct((bz, n, d), F32), jax.ShapeDtypeStruct((d, t_all), BF16),
                   tab1_shape, tab1_shape, tab2_shape, tab2_shape],
        scratch_shapes=[pltpu.VMEM((2 * PEER_HEADS, tm, PEER_HALF), BF16),
                        pltpu.VMEM((2, PEER_NKEYS, tm), F32)],
        compiler_params=_params("parallel", "arbitrary"),
        name="mix",
    )(x, mhf, mhb, ghf, ghb, mo, gr, mlg, glag, wout, g1, sh2, sc2, n2g, wq, keys)


def _peer_kernel(h2t_ref, u_ref, vt_ref, e1_ref, l_ref, e2_ref, rk_ref, x1_ref, g2_ref, nfg_ref,
                 out_ref, acc_ref):
    j = pl.program_id(2)
    te, tm = u_ref.shape[0], h2t_ref.shape[1]
    rows_per_step = te // PEER_NKEYS
    tiles = PEER_NKEYS // 16

    @pl.when(j == 0)
    def _():
        acc_ref[...] = jnp.zeros_like(acc_ref)

    act = _dot(u_ref[...], h2t_ref[...])
    act = 0.5 * act * (1.0 + lax.erf(act * 2.0 ** -0.5))
    act = act.astype(BF16).reshape(rows_per_step, tiles, 16, tm)
    parts = []
    for i in range(rows_per_step):
        w = None
        for p in range(PEER_HEADS):
            e1 = jnp.broadcast_to(e1_ref[p, i:i + 1, :], (16, tm)).astype(BF16)[None]
            lim = jnp.broadcast_to(l_ref[p, i:i + 1, :], (16, tm)).astype(BF16)[None]
            term = jnp.where(rk_ref[p] < lim, e2_ref[p] * e1, jnp.zeros((), BF16))
            w = term if w is None else w + term
        parts.append(w * act[i])
    gated = jnp.concatenate(parts, axis=0).reshape(te, tm)
    acc_ref[...] += _dot(vt_ref[...], gated)

    @pl.when(j == pl.num_programs(2) - 1)
    def _():
        xf = x1_ref[0] + g2_ref[0] * acc_ref[...].T
        out_ref[0] = _rms(xf) * nfg_ref[...]


def _peer(h2t, u, vt, e1, lim, e2, rk, x1, g2, nfg):
    bz, n, d = x1.shape
    n_exp = u.shape[0]
    tm = min(PEER_TM, n)
    te = PEER_TE
    assert n % tm == 0 and n_exp % te == 0 and te % PEER_NKEYS == 0
    nt = n // tm
    tab1 = pl.BlockSpec((PEER_HEADS, te // PEER_NKEYS, tm), lambda b, i, j: (0, j, b * nt + i))
    tab2 = pl.BlockSpec((PEER_HEADS, PEER_NKEYS // 16, 16, tm), lambda b, i, j: (0, 0, 0, b * nt + i))
    tok = pl.BlockSpec((1, tm, d), lambda b, i, j: (b, i, 0))
    return pl.pallas_call(
        _peer_kernel,
        grid=(bz, nt, n_exp // te),
        in_specs=[pl.BlockSpec((d, tm), lambda b, i, j: (0, b * nt + i)),
                  pl.BlockSpec((te, d), lambda b, i, j: (j, 0)),
                  pl.BlockSpec((d, te), lambda b, i, j: (0, j)),
                  tab1, tab1, tab2, tab2, tok,
                  pl.BlockSpec((1, 1, d), lambda b, i, j: (b, 0, 0)),
                  pl.BlockSpec((1, d), lambda b, i, j: (0, 0))],
        out_specs=tok,
        out_shape=jax.ShapeDtypeStruct((bz, n, d), F32),
        scratch_shapes=[pltpu.VMEM((d, tm), F32)],
        compiler_params=_params("parallel", "parallel", "arbitrary"),
        name="peer",
    )(h2t, u, vt, e1, lim, e2, rk, x1, g2, nfg)


def _inproj_weights(w_in, conv_w, gate_b, lr_w2, alpha_b):
    widths = (ML_WIDTH, ML_WIDTH, ML_WIDTH, ML_WIDTH, N_GATES,
              GLA_KEY_WIDTH, GLA_KEY_WIDTH, GLA_WIDTH, GLA_WIDTH, 2 * GLA_RANK)
    offs = [0]
    for w in widths:
        offs.append(offs[-1] + w)
    wb = w_in.astype(BF16)
    col = lambda a, b: wb[:, offs[a]:offs[b]]
    pad_lanes = lambda a: jnp.pad(a, ((0, 0), (0, LANES - a.shape[1])))
    w2 = jnp.zeros((LANES, 2 * GLA_KEY_WIDTH), F32)
    w2 = w2.at[:GLA_RANK, :GLA_KEY_WIDTH].set(lr_w2[0])
    w2 = w2.at[GLA_RANK:2 * GLA_RANK, GLA_KEY_WIDTH:].set(lr_w2[1])
    return (col(0, 2), col(2, 3), col(3, 4), pad_lanes(col(4, 5)), col(4, 5).T,
            col(5, 7), col(7, 8), col(8, 9), pad_lanes(col(9, 10)),
            conv_w, pad_lanes(gate_b[None, :]), gate_b[:, None],
            w2.astype(BF16), alpha_b.reshape(1, 2 * GLA_KEY_WIDTH))


def _per_head_gates(gtok, gmaj):
    bz, n, _ = gtok.shape
    gt = gtok.reshape(bz, n, 2, 2, ML_HEADS).transpose(0, 4, 1, 2, 3).reshape(bz, ML_HEADS, n, 4)
    gm = gmaj.reshape(bz, 2, 2, ML_HEADS, n // ML_CHUNK, ML_CHUNK)
    gm = gm.transpose(0, 3, 4, 1, 2, 5).reshape(bz, ML_HEADS, n // ML_CHUNK, 4, ML_CHUNK)
    return gt, gm


def _per_head_keys(a):
    bz, n, _ = a.shape
    return a.reshape(bz, n, GLA_HEADS, GLA_DK).transpose(0, 2, 1, 3)


def _per_head_chunks_t(at, heads, chunk):
    bz, width, n = at.shape
    return at.reshape(bz, heads, width // heads, n // chunk, chunk).transpose(0, 1, 3, 2, 4)


def _token_mix_inputs(x, sh, sc, ng, wts, row_len):
    mq, mk, mv, mo, gtok, gmaj, gq, gk, gv, gr, bg, mkt, gvt = _inproj(x, sh, sc, ng, wts, row_len)
    gt, gm = _per_head_gates(gtok, gmaj)
    ml = (mq, mk, _per_head_chunks_t(mkt, ML_HEADS, ML_CHUNK), mv, gt, gm)
    gla = (_per_head_keys(gq), _per_head_keys(gk), gv, _per_head_chunks_t(gvt, GLA_HEADS, GLA_CHUNK),
           _per_head_keys(bg[..., :GLA_KEY_WIDTH]), _per_head_keys(bg[..., GLA_KEY_WIDTH:]))
    return ml, gla, mo, gr


def kernel(x, c, ctx, c_ctx, w_mod, b_mod, norm1_g, w_in, ml_conv_w, ml_gate_b, ml_norm_g,
           gla_lr_w2, gla_alpha_b, gla_norm_g, w_out, norm2_g, peer_wq, peer_keys, peer_u,
           peer_v, norm_f_g):
    assert w_mod.shape[0] == 1, "single trunk layer"
    bz, n, d = x.shape
    c_all = jnp.concatenate([c, c_ctx[None, :]], axis=0)
    c_all = jnp.pad(c_all, ((0, (-c_all.shape[0]) % SUBLANES), (0, 0)))
    mod = _modulation(c_all, w_mod[0], b_mod[0][None, :])
    sh1, sc1, g1, sh2, sc2, g2 = [m[:, None, :] for m in jnp.split(mod[:bz], 6, axis=1)]
    mod_c = mod[bz]
    csh1, csc1 = mod_c[None, None, :d], mod_c[None, None, d:2 * d]

    ng1 = norm1_g[0][None, :]
    wts = _inproj_weights(w_in[0], ml_conv_w[0], ml_gate_b[0], gla_lr_w2[0], gla_alpha_b[0])
    ml_lat, gla_lat, mo, gr = _token_mix_inputs(x, sh1, sc1, ng1, wts, GRID_W)
    ml_ctx, gla_ctx, _, _ = _token_mix_inputs(ctx, csh1, csc1, ng1, wts, ctx.shape[1])
    mhf, mhb = _mlstm(*ml_lat, *ml_ctx)
    ghf, ghb = _gla(*gla_lat, *gla_ctx)

    keys = peer_keys[0].reshape(2 * PEER_HEADS, PEER_NKEYS, PEER_HALF).astype(BF16)
    x1, h2t, e1, lim, e2, rk = _mix(
        x, mhf, mhb, ghf, ghb, mo, gr, ml_norm_g[0][None, :], gla_norm_g[0][None, :],
        w_out[0].astype(BF16), g1, sh2, sc2, norm2_g[0][None, :], peer_wq[0].astype(BF16), keys)
    return _peer(h2t, peer_u[0].astype(BF16), peer_v[0].astype(BF16).T, e1, lim, e2, rk,
                 x1, g2, norm_f_g[None, :])
```

```python
import functools

import jax
import jax.numpy as jnp
from jax import lax
from jax.experimental import pallas as pl
from jax.experimental.pallas import tpu as pltpu

F32 = jnp.float32
BF16 = jnp.bfloat16

EPS = 1e-6
D_MODEL = 1024
GRID_W = 64

ML_HEADS = 4
ML_HEAD_DIM = 128
ML_WIDTH = ML_HEADS * ML_HEAD_DIM
ML_CHUNK = 128
GLA_HEADS = 4
GLA_DK = 64
GLA_DV = 128
GLA_KEY_WIDTH = GLA_HEADS * GLA_DK
GLA_WIDTH = GLA_HEADS * GLA_DV
GLA_RANK = 16
GLA_TAU = 16.0
GLA_CHUNK = 64
N_GATES = 4 * ML_HEADS

PEER_HEADS = 8
PEER_NKEYS = 128
PEER_TOPK = 16
PEER_HALF = 128

LANES = 128
SUBLANES = 8
VMEM_LIMIT = 56 * 1024 * 1024

INPROJ_TM = 256
ML_HEADS_PER_STEP = 4
GLA_HEADS_PER_STEP = 2
MIX_TM = 512
PEER_TM = 512
PEER_TE = 2048

NEG_INF = float("-inf")


def _params(*sem):
    return pltpu.CompilerParams(dimension_semantics=sem, vmem_limit_bytes=VMEM_LIMIT)


def _dot(a, b):
    return jnp.dot(a, b, preferred_element_type=F32)


def _dot_nt(a, b):
    return lax.dot_general(a, b, (((1,), (1,)), ((), ())), preferred_element_type=F32)


def _split3(x):
    hi = x.astype(BF16)
    r1 = x - hi.astype(F32)
    mid = r1.astype(BF16)
    lo = (r1 - mid.astype(F32)).astype(BF16)
    return hi, mid, lo


def _dot_exact_rhs(a01, x):
    hi, mid, lo = _split3(x)
    return _dot(a01, hi) + _dot(a01, mid) + _dot(a01, lo)


def _dot_exact_lhs(x, a01):
    hi, mid, lo = _split3(x)
    return _dot(hi, a01) + _dot(mid, a01) + _dot(lo, a01)


def _sigmoid(x):
    return 1.0 / (1.0 + jnp.exp(-x))


def _log_sigmoid(x):
    return jnp.minimum(x, 0.0) - jnp.log(1.0 + jnp.exp(-jnp.abs(x)))


def _rms(x):
    return x * lax.rsqrt(jnp.mean(x * x, axis=-1, keepdims=True) + EPS)


def _mod_kernel(c_ref, w_ref, b_ref, o_ref):
    cond = c_ref[...]
    cond = cond * _sigmoid(cond)
    ch, cl = cond.astype(BF16), (cond - cond.astype(BF16).astype(F32)).astype(BF16)
    w = w_ref[...]
    wh = w.astype(BF16)
    wl = (w - wh.astype(F32)).astype(BF16)
    o_ref[...] = _dot(ch, wh) + _dot(ch, wl) + _dot(cl, wh) + b_ref[...]


def _modulation(c_all, w_mod, b_mod):
    rows, d = c_all.shape
    n_out = w_mod.shape[1]
    tn = 512
    return pl.pallas_call(
        _mod_kernel,
        grid=(n_out // tn,),
        in_specs=[pl.BlockSpec((rows, d), lambda j: (0, 0)),
                  pl.BlockSpec((d, tn), lambda j: (0, j)),
                  pl.BlockSpec((1, tn), lambda j: (0, j))],
        out_specs=pl.BlockSpec((rows, tn), lambda j: (0, j)),
        out_shape=jax.ShapeDtypeStruct((rows, n_out), F32),
        compiler_params=_params("arbitrary"),
        name="mod",
    )(c_all, w_mod, b_mod)


def _chunk_masks(tm, chunk):
    r = lax.broadcasted_iota(jnp.int32, (tm, tm), 0)
    c = lax.broadcasted_iota(jnp.int32, (tm, tm), 1)
    same = (r // chunk) == (c // chunk)
    prefix = jnp.where(same & (c <= r), 1.0, 0.0).astype(BF16)
    suffix = jnp.where(same & (c >= r), 1.0, 0.0).astype(BF16)
    return prefix, suffix


def _inproj_kernel(x_ref, sh_ref, sc_ref, ng_ref, wqk_ref, wv_ref, wo_ref, wg_ref, wgt_ref,
                   wgqk_ref, wgv_ref, wgr_ref, wlr_ref, conv_ref, gb_ref, gbc_ref, w2_ref, ab_ref,
                   mq_ref, mk_ref, mv_ref, mo_ref, gtok_ref, gmaj_ref,
                   gq_ref, gk_ref, gv_ref, gr_ref, bf_ref, bb_ref, mkt_ref, gvt_ref, *, row_len):
    tm = x_ref.shape[1]
    x = x_ref[0]
    h = _rms(x) * ng_ref[...] * (1.0 + sc_ref[0]) + sh_ref[0]
    hb = h.astype(BF16)

    u = _dot(hb, wqk_ref[...])
    pos = lax.broadcasted_iota(jnp.int32, (tm, 1), 0) % row_len
    up = jnp.where(pos == 0, 0.0, pltpu.roll(u, 1, 0))
    dn = jnp.where(pos == row_len - 1, 0.0, pltpu.roll(u, tm - 1, 0))
    cw = conv_ref[...]
    y = cw[0:1] * up + cw[1:2] * u + cw[2:3] * dn
    y = y * _sigmoid(y)
    mq_ref[0] = y[:, :ML_WIDTH].astype(BF16)
    mk = y[:, ML_WIDTH:] * ML_HEAD_DIM ** -0.5
    mk_ref[0] = mk.astype(BF16)
    mkt = mk.T.astype(BF16)
    for hh in range(ML_HEADS):
        for ci in range(tm // ML_CHUNK):
            mkt_ref[0, hh, ci] = mkt[hh * ML_HEAD_DIM:(hh + 1) * ML_HEAD_DIM, ci * ML_CHUNK:(ci + 1) * ML_CHUNK]
    mv_ref[0] = _dot(hb, wv_ref[...]).astype(BF16)
    mo_ref[0] = _dot(hb, wo_ref[...]).astype(BF16)

    pre128, suf128 = _chunk_masks(tm, ML_CHUNK)
    g = _dot(hb, wg_ref[...]) + gb_ref[...]
    col = lax.broadcasted_iota(jnp.int32, (1, LANES), 1)
    is_f = ((col % 8) >= 4) & (col < N_GATES)
    is_bwd = col >= 8
    lf = jnp.where(is_f, _log_sigmoid(g), 0.0)
    cum = jnp.where(is_bwd, _dot_exact_rhs(suf128, lf), _dot_exact_rhs(pre128, lf))
    gtok_ref[0] = jnp.where(is_f, cum, g)[:, :N_GATES]

    gt = _dot_nt(wgt_ref[...], hb) + gbc_ref[...]
    row = lax.broadcasted_iota(jnp.int32, (N_GATES, 1), 0)
    is_f_r = (row % 8) >= 4
    lft = jnp.where(is_f_r, _log_sigmoid(gt), 0.0)
    cum_t = jnp.where(row >= 8, _dot_exact_lhs(lft, pre128), _dot_exact_lhs(lft, suf128))
    gmaj_ref[0] = jnp.where(is_f_r, cum_t, gt)

    gqk = _dot(hb, wgqk_ref[...])
    def per_head(ref, a):
        for hh in range(GLA_HEADS):
            ref[0, hh] = a[:, hh * GLA_DK:(hh + 1) * GLA_DK]

    per_head(gq_ref, (gqk[:, :GLA_KEY_WIDTH] * GLA_DK ** -0.5).astype(BF16))
    per_head(gk_ref, gqk[:, GLA_KEY_WIDTH:].astype(BF16))
    gv = _dot(hb, wgv_ref[...])
    gv_ref[0] = gv.astype(BF16)
    gvt = gv.T.astype(BF16)
    for hh in range(GLA_HEADS):
        for ci in range(tm // GLA_CHUNK):
            gvt_ref[0, hh, ci] = gvt[hh * GLA_DV:(hh + 1) * GLA_DV, ci * GLA_CHUNK:(ci + 1) * GLA_CHUNK]
    gr_ref[0] = _dot(hb, wgr_ref[...]).astype(BF16)
    glr = _dot(hb, wlr_ref[...]).astype(BF16)
    alpha = _dot(glr, w2_ref[...]) + ab_ref[...]
    la = _log_sigmoid(alpha) * (1.0 / GLA_TAU)
    pre64, suf64 = _chunk_masks(tm, GLA_CHUNK)
    per_head(bf_ref, _dot_exact_rhs(pre64, la[:, :GLA_KEY_WIDTH]))
    per_head(bb_ref, _dot_exact_rhs(suf64, la[:, GLA_KEY_WIDTH:]))


def _inproj(x, sh, sc, ng, wts, row_len):
    bz, n, d = x.shape
    tm = min(INPROJ_TM, n)
    assert n % tm == 0 and tm % row_len == 0 and tm % ML_CHUNK == 0
    grid = (bz, n // tm)
    per_b = (lambda b, i: (b, 0, 0)) if sh.shape[0] == bz else (lambda b, i: (0, 0, 0))

    def full(a):
        return pl.BlockSpec(a.shape, lambda b, i, nd=a.ndim: (0,) * nd)

    def tok(width):
        return pl.BlockSpec((1, tm, width), lambda b, i: (b, i, 0))

    def out(width, dtype):
        return jax.ShapeDtypeStruct((bz, n, width), dtype)

    in_specs = [tok(d), pl.BlockSpec((1, 1, d), per_b), pl.BlockSpec((1, 1, d), per_b), full(ng)]
    in_specs += [full(w) for w in wts]
    out_specs = [tok(ML_WIDTH)] * 4 + [tok(N_GATES), pl.BlockSpec((1, N_GATES, tm), lambda b, i: (b, 0, i))]
    key = pl.BlockSpec((1, GLA_HEADS, tm, GLA_DK), lambda b, i: (b, 0, i, 0))
    out_specs += [key] * 2 + [tok(GLA_WIDTH)] * 2 + [key] * 2
    out_shape = [out(ML_WIDTH, BF16)] * 4 + [out(N_GATES, F32),
                                              jax.ShapeDtypeStruct((bz, N_GATES, n), F32)]
    keys = lambda dtype: jax.ShapeDtypeStruct((bz, GLA_HEADS, n, GLA_DK), dtype)
    out_shape += [keys(BF16)] * 2 + [out(GLA_WIDTH, BF16)] * 2 + [keys(F32)] * 2
    for heads, dim, chunk in ((ML_HEADS, ML_HEAD_DIM, ML_CHUNK), (GLA_HEADS, GLA_DV, GLA_CHUNK)):
        out_specs.append(pl.BlockSpec((1, heads, tm // chunk, dim, chunk), lambda b, i: (b, 0, i, 0, 0)))
        out_shape.append(jax.ShapeDtypeStruct((bz, heads, n // chunk, dim, chunk), BF16))
    return pl.pallas_call(
        functools.partial(_inproj_kernel, row_len=row_len),
        grid=grid, in_specs=in_specs, out_specs=out_specs, out_shape=out_shape,
        compiler_params=_params("parallel", "arbitrary"),
        name="inproj",
    )(x, sh, sc, ng, *wts)


def _bmm(a, b):
    return lax.dot_general(a, b, (((2,), (1,)), ((0,), (0,))), preferred_element_type=F32)


def _bmm_nt(a, b):
    return lax.dot_general(a, b, (((2,), (2,)), ((0,), (0,))), preferred_element_type=F32)


def _by_direction(x, fwd, bwd):
    half = x.shape[0] // 2
    return jnp.concatenate([fwd(x[:half]), bwd(x[half:])], axis=0)


def _ml_chunks(q, k, kt, v, gt, gm, masks, state, want_out):
    ct, nv, m = state
    ll = q.shape[1]
    ig_col = _by_direction(gt, lambda g: g[:, :, 0:1], lambda g: g[:, :, 2:3])
    b_col = _by_direction(gt, lambda g: g[:, :, 1:2], lambda g: g[:, :, 3:4])
    ig_row = _by_direction(gm, lambda g: g[:, 0:1, :], lambda g: g[:, 2:3, :])
    b_row = _by_direction(gm, lambda g: g[:, 1:2, :], lambda g: g[:, 3:4, :])
    b_end = _by_direction(b_row, lambda b: b[:, :, ll - 1:ll], lambda b: b[:, :, 0:1])
    h = None
    if want_out:
        d_log = _by_direction(b_col - b_row + ig_row,
                              lambda x: jnp.where(masks[0], x, NEG_INF),
                              lambda x: jnp.where(masks[1], x, NEG_INF))
        inter_log = b_col + m
        m_t = jnp.maximum(inter_log, jnp.max(d_log, axis=2, keepdims=True))
        scores = _bmm_nt(q, k) * jnp.exp(d_log - m_t)
        inter = jnp.exp(inter_log - m_t)
        qf = q.astype(F32)
        num = _bmm(scores.astype(BF16), v) + inter * _bmm(q, ct.astype(BF16))
        den = jnp.sum(scores, axis=2, keepdims=True) + inter * jnp.sum(qf * nv, axis=2, keepdims=True)
        h = num / jnp.maximum(jnp.abs(den), jnp.exp(-m_t))
    m_loc = jnp.max(b_end - b_row + ig_row, axis=2, keepdims=True)
    w_col = jnp.exp(b_end - b_col + ig_col - m_loc)
    c_loc = _bmm(kt, (w_col * v.astype(F32)).astype(BF16))
    n_loc = jnp.sum(w_col * k.astype(F32), axis=1, keepdims=True)
    m_new = jnp.maximum(b_end + m, m_loc)
    a = jnp.exp(b_end + m - m_new)
    bb = jnp.exp(m_loc - m_new)
    return h, (a * ct + bb * c_loc, a * nv + bb * n_loc, m_new)


def _mlstm_kernel(q_ref, k_ref, kt_ref, v_ref, gt_ref, gm_ref,
                  qc_ref, kc_ref, ktc_ref, vc_ref, gtc_ref, gmc_ref, hf_ref, hb_ref):
    ll = ML_CHUNK
    nc = q_ref.shape[1] // ll
    ncc = qc_ref.shape[1] // ll
    r = lax.broadcasted_iota(jnp.int32, (ll, ll), 0)
    c = lax.broadcasted_iota(jnp.int32, (ll, ll), 1)
    masks = (c <= r, c >= r)

    heads = gt_ref.shape[1]
    head_cols = [slice(hh * ML_HEAD_DIM, (hh + 1) * ML_HEAD_DIM) for hh in range(heads)]

    def rows_of(ci):
        return pl.ds(pl.multiple_of(ci * ll, ll), ll)

    def load(refs, cf, cb):
        qr, kr, ktr, vr, gtr, gmr = refs
        sites = [(hh, ci) for ci in (cf, cb) for hh in range(heads)]
        seq = lambda r: jnp.stack([r[0, rows_of(ci), head_cols[hh]] for hh, ci in sites])
        return (seq(qr), seq(kr), jnp.stack([ktr[0, hh, ci] for hh, ci in sites]), seq(vr),
                jnp.stack([gtr[0, hh, rows_of(ci), :] for hh, ci in sites]),
                jnp.stack([gmr[0, hh, ci] for hh, ci in sites]))

    chains = 2 * heads
    zero = (jnp.zeros((chains, ML_HEAD_DIM, ML_HEAD_DIM), F32), jnp.zeros((chains, 1, ML_HEAD_DIM), F32),
            jnp.zeros((chains, 1, 1), F32))
    ctx_refs = (qc_ref, kc_ref, ktc_ref, vc_ref, gtc_ref, gmc_ref)
    lat_refs = (q_ref, k_ref, kt_ref, v_ref, gt_ref, gm_ref)

    def ctx_body(i, state):
        _, state = _ml_chunks(*load(ctx_refs, i, ncc - 1 - i), masks, state, False)
        return state

    def lat_body(i, state):
        cb = nc - 1 - i
        h, state = _ml_chunks(*load(lat_refs, i, cb), masks, state, True)
        for hh in range(heads):
            hf_ref[0, rows_of(i), head_cols[hh]] = h[hh]
            hb_ref[0, rows_of(cb), head_cols[hh]] = h[heads + hh]
        return state

    state = lax.fori_loop(0, ncc, ctx_body, zero)
    lax.fori_loop(0, nc, lat_body, state)


def _mlstm(q, k, kt, v, gt, gm, qc, kc, ktc, vc, gtc, gmc):
    bz, n, _ = q.shape
    ncx = qc.shape[1]
    hps = ML_HEADS_PER_STEP

    def seq(nn):
        return pl.BlockSpec((1, nn, hps * ML_HEAD_DIM), lambda b, h: (b, 0, h))

    def ktr(nn):
        return pl.BlockSpec((1, hps, nn // ML_CHUNK, ML_HEAD_DIM, ML_CHUNK), lambda b, h: (b, h, 0, 0, 0))

    def gtok(nn):
        return pl.BlockSpec((1, hps, nn, 4), lambda b, h: (b, h, 0, 0))

    def gmaj(nn):
        return pl.BlockSpec((1, hps, nn // ML_CHUNK, 4, ML_CHUNK), lambda b, h: (b, h, 0, 0, 0))

    def specs(nn):
        return [seq(nn), seq(nn), ktr(nn), seq(nn), gtok(nn), gmaj(nn)]

    return pl.pallas_call(
        _mlstm_kernel,
        grid=(bz, ML_HEADS // hps),
        in_specs=specs(n) + specs(ncx),
        out_specs=[seq(n)] * 2,
        out_shape=[jax.ShapeDtypeStruct((bz, n, ML_WIDTH), F32)] * 2,
        compiler_params=_params("parallel", "arbitrary"),
        name="mlstm",
    )(q, k, kt, v, gt, gm, qc, kc, ktc, vc, gtc, gmc)


def _gla_chunks(q, k, v, vt, b, masks, st, want_out):
    ll = q.shape[1]
    qf, kf = q.astype(F32), k.astype(F32)
    ref = _by_direction(b, lambda x: x[:, ll // 2:ll // 2 + 1], lambda x: x[:, ll // 2 - 1:ll // 2])
    b_end = _by_direction(b, lambda x: x[:, ll - 1:ll], lambda x: x[:, 0:1])
    o = None
    if want_out:
        att = _bmm_nt((qf * jnp.exp(b - ref)).astype(BF16), (kf * jnp.exp(ref - b)).astype(BF16))
        att = _by_direction(att, lambda x: jnp.where(masks[0], x, 0.0), lambda x: jnp.where(masks[1], x, 0.0))
        o = _bmm(att.astype(BF16), v) + _bmm_nt((qf * jnp.exp(b)).astype(BF16), st.astype(BF16))
    s_loc = _bmm(vt, (kf * jnp.exp(b_end - b)).astype(BF16))
    return o, jnp.exp(b_end) * st + s_loc


def _gla_kernel(q_ref, k_ref, v_ref, vt_ref, bf_ref, bb_ref,
                qc_ref, kc_ref, vc_ref, vtc_ref, bfc_ref, bbc_ref, of_ref, ob_ref):
    ll = GLA_CHUNK
    nc = q_ref.shape[2] // ll
    ncc = qc_ref.shape[2] // ll
    r = lax.broadcasted_iota(jnp.int32, (ll, ll), 0)
    c = lax.broadcasted_iota(jnp.int32, (ll, ll), 1)
    masks = (c <= r, c >= r)

    heads = q_ref.shape[1]

    def rows_of(ci):
        return pl.ds(pl.multiple_of(ci * ll, ll), ll)

    head_cols = [slice(hh * GLA_DV, (hh + 1) * GLA_DV) for hh in range(heads)]

    def load(refs, cf, cb):
        qr, kr, vr, vtr, bfr, bbr = refs
        sites = [(hh, ci) for ci in (cf, cb) for hh in range(heads)]
        per_head = lambda r: jnp.stack([r[0, hh, rows_of(ci), :] for hh, ci in sites])
        return (per_head(qr), per_head(kr),
                jnp.stack([vr[0, rows_of(ci), head_cols[hh]] for hh, ci in sites]),
                jnp.stack([vtr[0, hh, ci] for hh, ci in sites]),
                jnp.stack([bfr[0, hh, rows_of(cf), :] for hh in range(heads)]
                          + [bbr[0, hh, rows_of(cb), :] for hh in range(heads)]))

    zero = jnp.zeros((2 * heads, GLA_DV, GLA_DK), F32)
    ctx_refs = (qc_ref, kc_ref, vc_ref, vtc_ref, bfc_ref, bbc_ref)
    lat_refs = (q_ref, k_ref, v_ref, vt_ref, bf_ref, bb_ref)

    def ctx_body(i, st):
        _, st = _gla_chunks(*load(ctx_refs, i, ncc - 1 - i), masks, st, False)
        return st

    def lat_body(i, st):
        cb = nc - 1 - i
        o, st = _gla_chunks(*load(lat_refs, i, cb), masks, st, True)
        for hh in range(heads):
            of_ref[0, rows_of(i), head_cols[hh]] = o[hh]
            ob_ref[0, rows_of(cb), head_cols[hh]] = o[heads + hh]
        return st

    st = lax.fori_loop(0, ncc, ctx_body, zero)
    lax.fori_loop(0, nc, lat_body, st)


def _gla(q, k, v, vt, bf, bb, qc, kc, vc, vtc, bfc, bbc):
    bz, _, n, _ = q.shape
    ncx = qc.shape[2]
    hps = GLA_HEADS_PER_STEP

    def key(nn):
        return pl.BlockSpec((1, hps, nn, GLA_DK), lambda b, h: (b, h, 0, 0))

    def val(nn):
        return pl.BlockSpec((1, nn, hps * GLA_DV), lambda b, h: (b, 0, h))

    def valt(nn):
        return pl.BlockSpec((1, hps, nn // GLA_CHUNK, GLA_DV, GLA_CHUNK), lambda b, h: (b, h, 0, 0, 0))

    def specs(nn):
        return [key(nn), key(nn), val(nn), valt(nn), key(nn), key(nn)]

    return pl.pallas_call(
        _gla_kernel,
        grid=(bz, GLA_HEADS // hps),
        in_specs=specs(n) + specs(ncx),
        out_specs=[val(n)] * 2,
        out_shape=[jax.ShapeDtypeStruct((bz, n, GLA_WIDTH), F32)] * 2,
        compiler_params=_params("parallel", "arbitrary"),
        name="gla",
    )(q, k, v, vt, bf, bb, qc, kc, vc, vtc, bfc, bbc)


def _head_rms(h, g, width):
    parts = [_rms(h[:, i:i + width]) for i in range(0, h.shape[1], width)]
    return jnp.concatenate(parts, axis=1) * g


def _largest16(loads):
    slot = lax.broadcasted_iota(jnp.int32, (PEER_TOPK, LANES), 0)

    def body(r, carry):
        out = []
        for (m_prev, vals), load in zip(carry, loads):
            s = load()
            m = jnp.max(jnp.where(s < m_prev, s, NEG_INF), axis=0, keepdims=True)
            out.append((m, jnp.where(slot == r, m, vals)))
        return tuple(out)

    init = tuple((jnp.full((1, LANES), jnp.inf, F32), jnp.zeros((PEER_TOPK, LANES), F32))
                 for _ in loads)
    return [vals for _, vals in lax.fori_loop(0, PEER_TOPK, body, init)]


def _count(mask):
    return jnp.sum(jnp.where(mask, 1.0, 0.0), axis=0, keepdims=True)


def _top16(s):
    nk, tl = s.shape
    slot = lax.broadcasted_iota(jnp.int32, (PEER_TOPK, tl), 0)
    key_id = lax.broadcasted_iota(jnp.int32, (nk, tl), 0).astype(F32)

    def body(r, carry):
        s, rank, vals = carry
        m = jnp.max(s, axis=0, keepdims=True)
        first = jnp.min(jnp.where(s == m, key_id, float(nk)), axis=0, keepdims=True)
        sel = key_id == first
        rank = jnp.where(sel, jnp.asarray(r).astype(F32), rank)
        s = jnp.where(sel, NEG_INF, s)
        vals = jnp.where(slot == r, m, vals)
        return s, rank, vals

    init = (s, jnp.full((nk, tl), float(PEER_TOPK), F32), jnp.zeros((PEER_TOPK, tl), F32))
    _, rank, vals = lax.fori_loop(0, PEER_TOPK, body, init)
    return vals, rank


def _pair_candidates(v1, v2):
    blocks = [v1[0:1] + v2[0:8], v1[0:1] + v2[8:16]]
    blocks += [v1[a:a + 1] + v2[0:8] for a in range(1, 8)]
    blocks += [v1[8:16] + v2[0:1]]
    return jnp.concatenate(blocks, axis=0)


def _pick16(cand):
    i = lax.broadcasted_iota(jnp.int32, cand.shape, 0)
    blk, rr = i // 8, i % 8
    flat = jnp.where(blk == 0, rr, jnp.where(blk == 1, 8 + rr,
                     jnp.where(blk <= 8, (blk - 1) * 16 + rr, (8 + rr) * 16))).astype(F32)

    def body(_, carry):
        cand, picked = carry
        m = jnp.max(cand, axis=0, keepdims=True)
        first = jnp.min(jnp.where(cand == m, flat, 1e9), axis=0, keepdims=True)
        sel = flat == first
        return jnp.where(sel, NEG_INF, cand), jnp.where(sel, 1.0, picked)

    _, picked = lax.fori_loop(0, PEER_TOPK, body, (cand, jnp.zeros_like(cand)))
    return picked


def _first_key_thresholds(picked, v1):
    inf = jnp.inf
    lo = jnp.where(picked[0:8] > 0.0, v1[0:1], inf)
    for a in range(1, 8):
        lo = jnp.minimum(lo, jnp.where(picked[8 * (a + 1):8 * (a + 2)] > 0.0, v1[a:a + 1], inf))
    tail = jnp.min(jnp.where(picked[72:80] > 0.0, v1[8:16], inf), axis=0, keepdims=True)
    row = lax.broadcasted_iota(jnp.int32, lo.shape, 0)
    lo = jnp.minimum(lo, jnp.where(row == 0, tail, inf))
    hi = jnp.where(picked[8:16] > 0.0, v1[0:1], inf)
    return jnp.concatenate([lo, hi], axis=0)


def _pair_counts(cand, picked):
    z = jnp.sum(picked * jnp.exp(cand - cand[0:1]), axis=0, keepdims=True)
    n_rows = [jnp.sum(picked[0:16], axis=0, keepdims=True)]
    n_rows += [jnp.sum(picked[8 * (a + 1):8 * (a + 2)], axis=0, keepdims=True) for a in range(1, 8)]
    return jnp.concatenate(n_rows + [picked[72:80]], axis=0), z


def _mix_kernel(x_ref, mhf_ref, mhb_ref, ghf_ref, ghb_ref, mo_ref, gr_ref, mlg_ref, glag_ref,
                wout_ref, g1_ref, sh2_ref, sc2_ref, n2g_ref, wq_ref, keys_ref,
                x1_ref, h2t_ref, e1_ref, l_ref, e2_ref, rk_ref, q_scr, s_scr):
    ml = _head_rms(mhf_ref[0] + mhb_ref[0], mlg_ref[...], ML_HEAD_DIM)
    ml = _sigmoid(mo_ref[0].astype(F32)) * ml
    gl = _head_rms(ghf_ref[0] + ghb_ref[0], glag_ref[...], GLA_DV)
    gr = gr_ref[0].astype(F32)
    gl = gr * _sigmoid(gr) * gl
    mix = jnp.concatenate([ml, gl], axis=1).astype(BF16)
    x1 = x_ref[0] + g1_ref[0] * _dot(mix, wout_ref[...])
    x1_ref[0] = x1
    h2 = _rms(x1) * n2g_ref[...] * (1.0 + sc2_ref[0]) + sh2_ref[0]
    h2t_ref[...] = h2.T.astype(BF16)
    qall = _dot(h2.astype(BF16), wq_ref[...]).astype(BF16)
    for j in range(2 * PEER_HEADS):
        q_scr[j] = qall[:, j * PEER_HALF:(j + 1) * PEER_HALF]

    tm = x_ref.shape[1]
    topk = float(PEER_TOPK)
    lane_chunks = [slice(i, i + LANES) for i in range(0, tm, LANES)]

    def write_tables(p, lanes, s1, s2, v1, v2, z, lim, rank2):
        e1_ref[p, :, lanes] = jnp.exp(s1 - v1[0:1]) / z
        l_ref[p, :, lanes] = lim
        e2 = jnp.exp(s2 - v2[0:1]).astype(BF16)
        e2_ref[p, :, :, lanes] = e2.reshape(PEER_NKEYS // 16, 16, LANES)
        rk_ref[p, :, :, lanes] = rank2.astype(BF16).reshape(PEER_NKEYS // 16, 16, LANES)

    def head_body(p, carry):
        for hf in range(2):
            s_scr[hf] = _dot_nt(keys_ref[2 * p + hf], q_scr[2 * p + hf])
        vals = _largest16([functools.partial(lambda hf, lanes: s_scr[hf, :, lanes], hf, lanes)
                           for hf in range(2) for lanes in lane_chunks])
        v1s, v2s = vals[:len(lane_chunks)], vals[len(lane_chunks):]
        cands = [_pair_candidates(v1, v2) for v1, v2 in zip(v1s, v2s)]
        taus = _largest16([functools.partial(lambda c: c, c) for c in cands])
        excess = jnp.zeros((1, LANES), F32)
        for lanes, v1, v2, cand, tau in zip(lane_chunks, v1s, v2s, cands, taus):
            s1, s2 = s_scr[0, :, lanes], s_scr[1, :, lanes]
            picked = jnp.where(cand >= tau[PEER_TOPK - 1:PEER_TOPK], 1.0, 0.0)
            _, z = _pair_counts(cand, picked)
            theta = _first_key_thresholds(picked, v1)
            lim = jnp.zeros_like(s1)
            rank2 = jnp.zeros_like(s2)
            for i in range(PEER_TOPK):
                lim = lim + jnp.where(s1 >= theta[i:i + 1], 1.0, 0.0)
                rank2 = rank2 + jnp.where(s2 < v2[i:i + 1], 1.0, 0.0)
            write_tables(p, lanes, s1, s2, v1, v2, z, lim, rank2)
            n_marked = (_count(s1 >= v1[PEER_TOPK - 1:PEER_TOPK]) + _count(rank2 < topk)
                        + jnp.sum(picked, axis=0, keepdims=True))
            excess = jnp.maximum(excess, n_marked - 3.0 * topk)

        @pl.when(jnp.max(excess) > 0.0)
        def _():
            for lanes in lane_chunks:
                s1, s2 = s_scr[0, :, lanes], s_scr[1, :, lanes]
                v1, rank1 = _top16(s1)
                v2, rank2 = _top16(s2)
                cand = _pair_candidates(v1, v2)
                n, z = _pair_counts(cand, _pick16(cand))
                lim = jnp.zeros_like(rank1)
                for a in range(PEER_TOPK):
                    lim = jnp.where(rank1 == float(a), n[a:a + 1], lim)
                write_tables(p, lanes, s1, s2, v1, v2, z, lim, rank2)

        return carry

    lax.fori_loop(0, PEER_HEADS, head_body, 0)


def _mix(x, mhf, mhb, ghf, ghb, mo, gr, mlg, glag, wout, g1, sh2, sc2, n2g, wq, keys):
    bz, n, d = x.shape
    tm = min(MIX_TM, n)
    assert n % tm == 0 and tm % LANES == 0
    nt = n // tm
    t_all = bz * n

    def tok(width):
        return pl.BlockSpec((1, tm, width), lambda b, i: (b, i, 0))

    def full(a):
        return pl.BlockSpec(a.shape, lambda b, i, nd=a.ndim: (0,) * nd)

    vec = pl.BlockSpec((1, 1, d), lambda b, i: (b, 0, 0))
    tab1 = pl.BlockSpec((PEER_HEADS, PEER_NKEYS, tm), lambda b, i: (0, 0, b * nt + i))
    tab2 = pl.BlockSpec((PEER_HEADS, PEER_NKEYS // 16, 16, tm), lambda b, i: (0, 0, 0, b * nt + i))
    tab1_shape = jax.ShapeDtypeStruct((PEER_HEADS, PEER_NKEYS, t_all), F32)
    tab2_shape = jax.ShapeDtypeStruct((PEER_HEADS, PEER_NKEYS // 16, 16, t_all), BF16)
    return pl.pallas_call(
        _mix_kernel,
        grid=(bz, nt),
        in_specs=[tok(d)] + [tok(ML_WIDTH)] * 6 + [full(mlg), full(glag), full(wout), vec, vec, vec,
                                                    full(n2g), full(wq), full(keys)],
        out_specs=[tok(d), pl.BlockSpec((d, tm), lambda b, i: (0, b * nt + i)), tab1, tab1, tab2, tab2],
        out_shape=[jax.ShapeDtypeStruct((bz, n, d), F32), jax.ShapeDtypeStruct((d, t_all), BF16),
                   tab1_shape, tab1_shape, tab2_shape, tab2_shape],
        scratch_shapes=[pltpu.VMEM((2 * PEER_HEADS, tm, PEER_HALF), BF16),
                        pltpu.VMEM((2, PEER_NKEYS, tm), F32)],
        compiler_params=_params("parallel", "arbitrary"),
        name="mix",
    )(x, mhf, mhb, ghf, ghb, mo, gr, mlg, glag, wout, g1, sh2, sc2, n2g, wq, keys)


def _peer_kernel(h2t_ref, u_ref, vt_ref, e1_ref, l_ref, e2_ref, rk_ref, x1_ref, g2_ref, nfg_ref,
                 out_ref, acc_ref):
    j = pl.program_id(2)
    te, tm = u_ref.shape[0], h2t_ref.shape[1]
    rows_per_step = te // PEER_NKEYS
    tiles = PEER_NKEYS // 16

    @pl.when(j == 0)
    def _():
        acc_ref[...] = jnp.zeros_like(acc_ref)

    act = _dot(u_ref[...], h2t_ref[...])
    act = act.astype(BF16)
    act = (0.5 * act * (1.0 + lax.erf(act * 2.0 ** -0.5))).reshape(rows_per_step, tiles, 16, tm)
    parts = []
    for i in range(rows_per_step):
        w = None
        for p in range(PEER_HEADS):
            e1 = jnp.broadcast_to(e1_ref[p, i:i + 1, :], (16, tm)).astype(BF16)[None]
            lim = jnp.broadcast_to(l_ref[p, i:i + 1, :], (16, tm)).astype(BF16)[None]
            term = jnp.where(rk_ref[p] < lim, e2_ref[p] * e1, jnp.zeros((), BF16))
            w = term if w is None else w + term
        parts.append(w * act[i])
    gated = jnp.concatenate(parts, axis=0).reshape(te, tm)
    acc_ref[...] += _dot(vt_ref[...], gated)

    @pl.when(j == pl.num_programs(2) - 1)
    def _():
        xf = x1_ref[0] + g2_ref[0] * acc_ref[...].T
        out_ref[0] = _rms(xf) * nfg_ref[...]


def _peer(h2t, u, vt, e1, lim, e2, rk, x1, g2, nfg):
    bz, n, d = x1.shape
    n_exp = u.shape[0]
    tm = min(PEER_TM, n)
    te = PEER_TE
    assert n % tm == 0 and n_exp % te == 0 and te % PEER_NKEYS == 0
    nt = n // tm
    tab1 = pl.BlockSpec((PEER_HEADS, te // PEER_NKEYS, tm), lambda b, i, j: (0, j, b * nt + i))
    tab2 = pl.BlockSpec((PEER_HEADS, PEER_NKEYS // 16, 16, tm), lambda b, i, j: (0, 0, 0, b * nt + i))
    tok = pl.BlockSpec((1, tm, d), lambda b, i, j: (b, i, 0))
    return pl.pallas_call(
        _peer_kernel,
        grid=(bz, nt, n_exp // te),
        in_specs=[pl.BlockSpec((d, tm), lambda b, i, j: (0, b * nt + i)),
                  pl.BlockSpec((te, d), lambda b, i, j: (j, 0)),
                  pl.BlockSpec((d, te), lambda b, i, j: (0, j)),
                  tab1, tab1, tab2, tab2, tok,
                  pl.BlockSpec((1, 1, d), lambda b, i, j: (b, 0, 0)),
                  pl.BlockSpec((1, d), lambda b, i, j: (0, 0))],
        out_specs=tok,
        out_shape=jax.ShapeDtypeStruct((bz, n, d), F32),
        scratch_shapes=[pltpu.VMEM((d, tm), F32)],
        compiler_params=_params("parallel", "parallel", "arbitrary"),
        name="peer",
    )(h2t, u, vt, e1, lim, e2, rk, x1, g2, nfg)


def _inproj_weights(w_in, conv_w, gate_b, lr_w2, alpha_b):
    widths = (ML_WIDTH, ML_WIDTH, ML_WIDTH, ML_WIDTH, N_GATES,
              GLA_KEY_WIDTH, GLA_KEY_WIDTH, GLA_WIDTH, GLA_WIDTH, 2 * GLA_RANK)
    offs = [0]
    for w in widths:
        offs.append(offs[-1] + w)
    wb = w_in.astype(BF16)
    col = lambda a, b: wb[:, offs[a]:offs[b]]
    pad_lanes = lambda a: jnp.pad(a, ((0, 0), (0, LANES - a.shape[1])))
    w2 = jnp.zeros((LANES, 2 * GLA_KEY_WIDTH), F32)
    w2 = w2.at[:GLA_RANK, :GLA_KEY_WIDTH].set(lr_w2[0])
    w2 = w2.at[GLA_RANK:2 * GLA_RANK, GLA_KEY_WIDTH:].set(lr_w2[1])
    return (col(0, 2), col(2, 3), col(3, 4), pad_lanes(col(4, 5)), col(4, 5).T,
            col(5, 7), col(7, 8), col(8, 9), pad_lanes(col(9, 10)),
            conv_w, pad_lanes(gate_b[None, :]), gate_b[:, None],
            w2.astype(BF16), alpha_b.reshape(1, 2 * GLA_KEY_WIDTH))


def _per_head_gates(gtok, gmaj):
    bz, n, _ = gtok.shape
    gt = gtok.reshape(bz, n, 2, 2, ML_HEADS).transpose(0, 4, 1, 2, 3).reshape(bz, ML_HEADS, n, 4)
    gm = gmaj.reshape(bz, 2, 2, ML_HEADS, n // ML_CHUNK, ML_CHUNK)
    gm = gm.transpose(0, 3, 4, 1, 2, 5).reshape(bz, ML_HEADS, n // ML_CHUNK, 4, ML_CHUNK)
    return gt, gm


def _token_mix_inputs(x, sh, sc, ng, wts, row_len):
    mq, mk, mv, mo, gtok, gmaj, gq, gk, gv, gr, bf, bb, mkt, gvt = _inproj(x, sh, sc, ng, wts, row_len)
    gt, gm = _per_head_gates(gtok, gmaj)
    return (mq, mk, mkt, mv, gt, gm), (gq, gk, gv, gvt, bf, bb), mo, gr


def kernel(x, c, ctx, c_ctx, w_mod, b_mod, norm1_g, w_in, ml_conv_w, ml_gate_b, ml_norm_g,
           gla_lr_w2, gla_alpha_b, gla_norm_g, w_out, norm2_g, peer_wq, peer_keys, peer_u,
           peer_v, norm_f_g):
    assert w_mod.shape[0] == 1, "single trunk layer"
    bz, n, d = x.shape
    c_all = jnp.concatenate([c, c_ctx[None, :]], axis=0)
    c_all = jnp.pad(c_all, ((0, (-c_all.shape[0]) % SUBLANES), (0, 0)))
    mod = _modulation(c_all, w_mod[0], b_mod[0][None, :])
    sh1, sc1, g1, sh2, sc2, g2 = [m[:, None, :] for m in jnp.split(mod[:bz], 6, axis=1)]
    mod_c = mod[bz]
    csh1, csc1 = mod_c[None, None, :d], mod_c[None, None, d:2 * d]

    ng1 = norm1_g[0][None, :]
    wts = _inproj_weights(w_in[0], ml_conv_w[0], ml_gate_b[0], gla_lr_w2[0], gla_alpha_b[0])
    ml_lat, gla_lat, mo, gr = _token_mix_inputs(x, sh1, sc1, ng1, wts, GRID_W)
    ml_ctx, gla_ctx, _, _ = _token_mix_inputs(ctx, csh1, csc1, ng1, wts, ctx.shape[1])
    mhf, mhb = _mlstm(*ml_lat, *ml_ctx)
    ghf, ghb = _gla(*gla_lat, *gla_ctx)

    keys = peer_keys[0].reshape(2 * PEER_HEADS, PEER_NKEYS, PEER_HALF).astype(BF16)
    x1, h2t, e1, lim, e2, rk = _mix(
        x, mhf, mhb, ghf, ghb, mo, gr, ml_norm_g[0][None, :], gla_norm_g[0][None, :],
        w_out[0].astype(BF16), g1, sh2, sc2, norm2_g[0][None, :], peer_wq[0].astype(BF16), keys)
    return _peer(h2t, peer_u[0].astype(BF16), peer_v[0].astype(BF16).T, e1, lim, e2, rk,
                 x1, g2, norm_f_g[None, :])
```

```python
import functools

import jax
import jax.numpy as jnp
from jax import lax
from jax.experimental import pallas as pl
from jax.experimental.pallas import tpu as pltpu

F32 = jnp.float32
BF16 = jnp.bfloat16

EPS = 1e-6
D_MODEL = 1024
GRID_W = 64

ML_HEADS = 4
ML_HEAD_DIM = 128
ML_WIDTH = ML_HEADS * ML_HEAD_DIM
ML_CHUNK = 128
GLA_HEADS = 4
GLA_DK = 64
GLA_DV = 128
GLA_KEY_WIDTH = GLA_HEADS * GLA_DK
GLA_WIDTH = GLA_HEADS * GLA_DV
GLA_RANK = 16
GLA_TAU = 16.0
GLA_CHUNK = 64
N_GATES = 4 * ML_HEADS

PEER_HEADS = 8
PEER_NKEYS = 128
PEER_TOPK = 16
PEER_HALF = 128

LANES = 128
SUBLANES = 8
VMEM_LIMIT = 56 * 1024 * 1024

INPROJ_TM = 256
ML_HEADS_PER_STEP = 4
GLA_HEADS_PER_STEP = 2
MIX_TM = 512
PEER_TM = 512
PEER_TE = 2048

NEG_INF = float("-inf")


def _params(*sem):
    return pltpu.CompilerParams(dimension_semantics=sem, vmem_limit_bytes=VMEM_LIMIT)


def _dot(a, b):
    return jnp.dot(a, b, preferred_element_type=F32)


def _dot_nt(a, b):
    return lax.dot_general(a, b, (((1,), (1,)), ((), ())), preferred_element_type=F32)


def _split2(x):
    hi = x.astype(BF16)
    return hi, (x - hi.astype(F32)).astype(BF16)


def _dot_exact_rhs(a01, x):
    hi, lo = _split2(x)
    return _dot(a01, hi) + _dot(a01, lo)


def _dot_exact_lhs(x, a01):
    hi, lo = _split2(x)
    return _dot(hi, a01) + _dot(lo, a01)


def _sigmoid(x):
    return 1.0 / (1.0 + jnp.exp(-x))


def _log_sigmoid(x):
    return jnp.minimum(x, 0.0) - jnp.log(1.0 + jnp.exp(-jnp.abs(x)))


def _rms(x):
    return x * lax.rsqrt(jnp.mean(x * x, axis=-1, keepdims=True) + EPS)


def _mod_kernel(c_ref, w_ref, b_ref, o_ref):
    cond = c_ref[...]
    cond = cond * _sigmoid(cond)
    ch, cl = cond.astype(BF16), (cond - cond.astype(BF16).astype(F32)).astype(BF16)
    w = w_ref[...]
    wh = w.astype(BF16)
    wl = (w - wh.astype(F32)).astype(BF16)
    o_ref[...] = _dot(ch, wh) + _dot(ch, wl) + _dot(cl, wh) + b_ref[...]


def _modulation(c_all, w_mod, b_mod):
    rows, d = c_all.shape
    n_out = w_mod.shape[1]
    tn = 512
    return pl.pallas_call(
        _mod_kernel,
        grid=(n_out // tn,),
        in_specs=[pl.BlockSpec((rows, d), lambda j: (0, 0)),
                  pl.BlockSpec((d, tn), lambda j: (0, j)),
                  pl.BlockSpec((1, tn), lambda j: (0, j))],
        out_specs=pl.BlockSpec((rows, tn), lambda j: (0, j)),
        out_shape=jax.ShapeDtypeStruct((rows, n_out), F32),
        compiler_params=_params("arbitrary"),
        name="mod",
    )(c_all, w_mod, b_mod)


def _chunk_masks(tm, chunk):
    r = lax.broadcasted_iota(jnp.int32, (tm, tm), 0)
    c = lax.broadcasted_iota(jnp.int32, (tm, tm), 1)
    same = (r // chunk) == (c // chunk)
    prefix = jnp.where(same & (c <= r), 1.0, 0.0).astype(BF16)
    suffix = jnp.where(same & (c >= r), 1.0, 0.0).astype(BF16)
    return prefix, suffix


def _inproj_kernel(x_ref, sh_ref, sc_ref, ng_ref, wqk_ref, wv_ref, wo_ref, wg_ref, wgt_ref,
                   wgqk_ref, wgv_ref, wgr_ref, wlr_ref, conv_ref, gb_ref, gbc_ref, w2_ref, ab_ref,
                   mq_ref, mk_ref, mv_ref, mo_ref, gtok_ref, gmaj_ref,
                   gq_ref, gk_ref, gv_ref, gr_ref, bf_ref, bb_ref, mkt_ref, gvt_ref, *, row_len):
    tm = x_ref.shape[1]
    x = x_ref[0]
    h = _rms(x) * ng_ref[...] * (1.0 + sc_ref[0]) + sh_ref[0]
    hb = h.astype(BF16)

    u = _dot(hb, wqk_ref[...])
    pos = lax.broadcasted_iota(jnp.int32, (tm, 1), 0) % row_len
    up = jnp.where(pos == 0, 0.0, pltpu.roll(u, 1, 0))
    dn = jnp.where(pos == row_len - 1, 0.0, pltpu.roll(u, tm - 1, 0))
    cw = conv_ref[...]
    y = cw[0:1] * up + cw[1:2] * u + cw[2:3] * dn
    y = y * _sigmoid(y)
    mq_ref[0] = y[:, :ML_WIDTH].astype(BF16)
    mk = y[:, ML_WIDTH:] * ML_HEAD_DIM ** -0.5
    mk_ref[0] = mk.astype(BF16)
    mkt = mk.T.astype(BF16)
    for hh in range(ML_HEADS):
        for ci in range(tm // ML_CHUNK):
            mkt_ref[0, hh, ci] = mkt[hh * ML_HEAD_DIM:(hh + 1) * ML_HEAD_DIM, ci * ML_CHUNK:(ci + 1) * ML_CHUNK]
    mv_ref[0] = _dot(hb, wv_ref[...]).astype(BF16)
    mo_ref[0] = _dot(hb, wo_ref[...]).astype(BF16)

    pre128, suf128 = _chunk_masks(tm, ML_CHUNK)
    g = _dot(hb, wg_ref[...]) + gb_ref[...]
    col = lax.broadcasted_iota(jnp.int32, (1, LANES), 1)
    is_f = ((col % 8) >= 4) & (col < N_GATES)
    is_bwd = col >= 8
    lf = jnp.where(is_f, _log_sigmoid(g), 0.0)
    cum = jnp.where(is_bwd, _dot_exact_rhs(suf128, lf), _dot_exact_rhs(pre128, lf))
    gtok_ref[0] = jnp.where(is_f, cum, g)[:, :N_GATES]

    gt = _dot_nt(wgt_ref[...], hb) + gbc_ref[...]
    row = lax.broadcasted_iota(jnp.int32, (N_GATES, 1), 0)
    is_f_r = (row % 8) >= 4
    lft = jnp.where(is_f_r, _log_sigmoid(gt), 0.0)
    cum_t = jnp.where(row >= 8, _dot_exact_lhs(lft, pre128), _dot_exact_lhs(lft, suf128))
    gmaj_ref[0] = jnp.where(is_f_r, cum_t, gt)

    gqk = _dot(hb, wgqk_ref[...])
    def per_head(ref, a):
        for hh in range(GLA_HEADS):
            ref[0, hh] = a[:, hh * GLA_DK:(hh + 1) * GLA_DK]

    per_head(gq_ref, (gqk[:, :GLA_KEY_WIDTH] * GLA_DK ** -0.5).astype(BF16))
    per_head(gk_ref, gqk[:, GLA_KEY_WIDTH:].astype(BF16))
    gv = _dot(hb, wgv_ref[...])
    gv_ref[0] = gv.astype(BF16)
    gvt = gv.T.astype(BF16)
    for hh in range(GLA_HEADS):
        for ci in range(tm // GLA_CHUNK):
            gvt_ref[0, hh, ci] = gvt[hh * GLA_DV:(hh + 1) * GLA_DV, ci * GLA_CHUNK:(ci + 1) * GLA_CHUNK]
    gr_ref[0] = _dot(hb, wgr_ref[...]).astype(BF16)
    glr = _dot(hb, wlr_ref[...]).astype(BF16)
    alpha = _dot(glr, w2_ref[...]) + ab_ref[...]
    la = _log_sigmoid(alpha) * (1.0 / GLA_TAU)
    pre64, suf64 = _chunk_masks(tm, GLA_CHUNK)
    per_head(bf_ref, _dot_exact_rhs(pre64, la[:, :GLA_KEY_WIDTH]))
    per_head(bb_ref, _dot_exact_rhs(suf64, la[:, GLA_KEY_WIDTH:]))


def _inproj(x, sh, sc, ng, wts, row_len):
    bz, n, d = x.shape
    tm = min(INPROJ_TM, n)
    assert n % tm == 0 and tm % row_len == 0 and tm % ML_CHUNK == 0
    grid = (bz, n // tm)
    per_b = (lambda b, i: (b, 0, 0)) if sh.shape[0] == bz else (lambda b, i: (0, 0, 0))

    def full(a):
        return pl.BlockSpec(a.shape, lambda b, i, nd=a.ndim: (0,) * nd)

    def tok(width):
        return pl.BlockSpec((1, tm, width), lambda b, i: (b, i, 0))

    def out(width, dtype):
        return jax.ShapeDtypeStruct((bz, n, width), dtype)

    in_specs = [tok(d), pl.BlockSpec((1, 1, d), per_b), pl.BlockSpec((1, 1, d), per_b), full(ng)]
    in_specs += [full(w) for w in wts]
    out_specs = [tok(ML_WIDTH)] * 4 + [tok(N_GATES), pl.BlockSpec((1, N_GATES, tm), lambda b, i: (b, 0, i))]
    key = pl.BlockSpec((1, GLA_HEADS, tm, GLA_DK), lambda b, i: (b, 0, i, 0))
    out_specs += [key] * 2 + [tok(GLA_WIDTH)] * 2 + [key] * 2
    out_shape = [out(ML_WIDTH, BF16)] * 4 + [out(N_GATES, F32),
                                              jax.ShapeDtypeStruct((bz, N_GATES, n), F32)]
    keys = lambda dtype: jax.ShapeDtypeStruct((bz, GLA_HEADS, n, GLA_DK), dtype)
    out_shape += [keys(BF16)] * 2 + [out(GLA_WIDTH, BF16)] * 2 + [keys(F32)] * 2
    for heads, dim, chunk in ((ML_HEADS, ML_HEAD_DIM, ML_CHUNK), (GLA_HEADS, GLA_DV, GLA_CHUNK)):
        out_specs.append(pl.BlockSpec((1, heads, tm // chunk, dim, chunk), lambda b, i: (b, 0, i, 0, 0)))
        out_shape.append(jax.ShapeDtypeStruct((bz, heads, n // chunk, dim, chunk), BF16))
    return pl.pallas_call(
        functools.partial(_inproj_kernel, row_len=row_len),
        grid=grid, in_specs=in_specs, out_specs=out_specs, out_shape=out_shape,
        compiler_params=_params("parallel", "arbitrary"),
        name="inproj",
    )(x, sh, sc, ng, *wts)


def _bmm(a, b):
    return lax.dot_general(a, b, (((2,), (1,)), ((0,), (0,))), preferred_element_type=F32)


def _bmm_nt(a, b):
    return lax.dot_general(a, b, (((2,), (2,)), ((0,), (0,))), preferred_element_type=F32)


def _by_direction(x, fwd, bwd):
    half = x.shape[0] // 2
    return jnp.concatenate([fwd(x[:half]), bwd(x[half:])], axis=0)


def _ml_chunks(q, k, kt, v, gt, gm, masks, state, want_out):
    ct, nv, m = state
    ll = q.shape[1]
    ig_col = _by_direction(gt, lambda g: g[:, :, 0:1], lambda g: g[:, :, 2:3])
    b_col = _by_direction(gt, lambda g: g[:, :, 1:2], lambda g: g[:, :, 3:4])
    ig_row = _by_direction(gm, lambda g: g[:, 0:1, :], lambda g: g[:, 2:3, :])
    b_row = _by_direction(gm, lambda g: g[:, 1:2, :], lambda g: g[:, 3:4, :])
    b_end = _by_direction(b_row, lambda b: b[:, :, ll - 1:ll], lambda b: b[:, :, 0:1])
    h = None
    if want_out:
        d_log = _by_direction(b_col - b_row + ig_row,
                              lambda x: jnp.where(masks[0], x, NEG_INF),
                              lambda x: jnp.where(masks[1], x, NEG_INF))
        inter_log = b_col + m
        m_t = jnp.maximum(inter_log, jnp.max(d_log, axis=2, keepdims=True))
        scores = _bmm_nt(q, k) * jnp.exp(d_log - m_t)
        inter = jnp.exp(inter_log - m_t)
        qf = q.astype(F32)
        num = _bmm(scores.astype(BF16), v) + inter * _bmm(q, ct.astype(BF16))
        den = jnp.sum(scores, axis=2, keepdims=True) + inter * jnp.sum(qf * nv, axis=2, keepdims=True)
        h = num / jnp.maximum(jnp.abs(den), jnp.exp(-m_t))
    m_loc = jnp.max(b_end - b_row + ig_row, axis=2, keepdims=True)
    w_col = jnp.exp(b_end - b_col + ig_col - m_loc)
    c_loc = _bmm(kt, (w_col * v.astype(F32)).astype(BF16))
    n_loc = jnp.sum(w_col * k.astype(F32), axis=1, keepdims=True)
    m_new = jnp.maximum(b_end + m, m_loc)
    a = jnp.exp(b_end + m - m_new)
    bb = jnp.exp(m_loc - m_new)
    return h, (a * ct + bb * c_loc, a * nv + bb * n_loc, m_new)


def _mlstm_kernel(q_ref, k_ref, kt_ref, v_ref, gt_ref, gm_ref,
                  qc_ref, kc_ref, ktc_ref, vc_ref, gtc_ref, gmc_ref, hf_ref, hb_ref):
    ll = ML_CHUNK
    nc = q_ref.shape[1] // ll
    ncc = qc_ref.shape[1] // ll
    r = lax.broadcasted_iota(jnp.int32, (ll, ll), 0)
    c = lax.broadcasted_iota(jnp.int32, (ll, ll), 1)
    masks = (c <= r, c >= r)

    heads = gt_ref.shape[1]
    head_cols = [slice(hh * ML_HEAD_DIM, (hh + 1) * ML_HEAD_DIM) for hh in range(heads)]

    def rows_of(ci):
        return pl.ds(pl.multiple_of(ci * ll, ll), ll)

    def load(refs, cf, cb):
        qr, kr, ktr, vr, gtr, gmr = refs
        sites = [(hh, ci) for ci in (cf, cb) for hh in range(heads)]
        seq = lambda r: jnp.stack([r[0, rows_of(ci), head_cols[hh]] for hh, ci in sites])
        return (seq(qr), seq(kr), jnp.stack([ktr[0, hh, ci] for hh, ci in sites]), seq(vr),
                jnp.stack([gtr[0, hh, rows_of(ci), :] for hh, ci in sites]),
                jnp.stack([gmr[0, hh, ci] for hh, ci in sites]))

    chains = 2 * heads
    zero = (jnp.zeros((chains, ML_HEAD_DIM, ML_HEAD_DIM), F32), jnp.zeros((chains, 1, ML_HEAD_DIM), F32),
            jnp.zeros((chains, 1, 1), F32))
    ctx_refs = (qc_ref, kc_ref, ktc_ref, vc_ref, gtc_ref, gmc_ref)
    lat_refs = (q_ref, k_ref, kt_ref, v_ref, gt_ref, gm_ref)

    def ctx_body(i, state):
        _, state = _ml_chunks(*load(ctx_refs, i, ncc - 1 - i), masks, state, False)
        return state

    def lat_body(i, state):
        cb = nc - 1 - i
        h, state = _ml_chunks(*load(lat_refs, i, cb), masks, state, True)
        for hh in range(heads):
            hf_ref[0, rows_of(i), head_cols[hh]] = h[hh]
            hb_ref[0, rows_of(cb), head_cols[hh]] = h[heads + hh]
        return state

    state = lax.fori_loop(0, ncc, ctx_body, zero)
    lax.fori_loop(0, nc, lat_body, state)


def _mlstm(q, k, kt, v, gt, gm, qc, kc, ktc, vc, gtc, gmc):
    bz, n, _ = q.shape
    ncx = qc.shape[1]
    hps = ML_HEADS_PER_STEP

    def seq(nn):
        return pl.BlockSpec((1, nn, hps * ML_HEAD_DIM), lambda b, h: (b, 0, h))

    def ktr(nn):
        return pl.BlockSpec((1, hps, nn // ML_CHUNK, ML_HEAD_DIM, ML_CHUNK), lambda b, h: (b, h, 0, 0, 0))

    def gtok(nn):
        return pl.BlockSpec((1, hps, nn, 4), lambda b, h: (b, h, 0, 0))

    def gmaj(nn):
        return pl.BlockSpec((1, hps, nn // ML_CHUNK, 4, ML_CHUNK), lambda b, h: (b, h, 0, 0, 0))

    def specs(nn):
        return [seq(nn), seq(nn), ktr(nn), seq(nn), gtok(nn), gmaj(nn)]

    return pl.pallas_call(
        _mlstm_kernel,
        grid=(bz, ML_HEADS // hps),
        in_specs=specs(n) + specs(ncx),
        out_specs=[seq(n)] * 2,
        out_shape=[jax.ShapeDtypeStruct((bz, n, ML_WIDTH), F32)] * 2,
        compiler_params=_params("parallel", "arbitrary"),
        name="mlstm",
    )(q, k, kt, v, gt, gm, qc, kc, ktc, vc, gtc, gmc)


def _gla_chunks(q, k, v, vt, b, masks, st, want_out):
    ll = q.shape[1]
    qf, kf = q.astype(F32), k.astype(F32)
    ref = _by_direction(b, lambda x: x[:, ll // 2:ll // 2 + 1], lambda x: x[:, ll // 2 - 1:ll // 2])
    b_end = _by_direction(b, lambda x: x[:, ll - 1:ll], lambda x: x[:, 0:1])
    o = None
    if want_out:
        att = _bmm_nt((qf * jnp.exp(b - ref)).astype(BF16), (kf * jnp.exp(ref - b)).astype(BF16))
        att = _by_direction(att, lambda x: jnp.where(masks[0], x, 0.0), lambda x: jnp.where(masks[1], x, 0.0))
        o = _bmm(att.astype(BF16), v) + _bmm_nt((qf * jnp.exp(b)).astype(BF16), st.astype(BF16))
    s_loc = _bmm(vt, (kf * jnp.exp(b_end - b)).astype(BF16))
    return o, jnp.exp(b_end) * st + s_loc


def _gla_kernel(q_ref, k_ref, v_ref, vt_ref, bf_ref, bb_ref,
                qc_ref, kc_ref, vc_ref, vtc_ref, bfc_ref, bbc_ref, of_ref, ob_ref):
    ll = GLA_CHUNK
    nc = q_ref.shape[2] // ll
    ncc = qc_ref.shape[2] // ll
    r = lax.broadcasted_iota(jnp.int32, (ll, ll), 0)
    c = lax.broadcasted_iota(jnp.int32, (ll, ll), 1)
    masks = (c <= r, c >= r)

    heads = q_ref.shape[1]

    def rows_of(ci):
        return pl.ds(pl.multiple_of(ci * ll, ll), ll)

    head_cols = [slice(hh * GLA_DV, (hh + 1) * GLA_DV) for hh in range(heads)]

    def load(refs, cf, cb):
        qr, kr, vr, vtr, bfr, bbr = refs
        sites = [(hh, ci) for ci in (cf, cb) for hh in range(heads)]
        per_head = lambda r: jnp.stack([r[0, hh, rows_of(ci), :] for hh, ci in sites])
        return (per_head(qr), per_head(kr),
                jnp.stack([vr[0, rows_of(ci), head_cols[hh]] for hh, ci in sites]),
                jnp.stack([vtr[0, hh, ci] for hh, ci in sites]),
                jnp.stack([bfr[0, hh, rows_of(cf), :] for hh in range(heads)]
                          + [bbr[0, hh, rows_of(cb), :] for hh in range(heads)]))

    zero = jnp.zeros((2 * heads, GLA_DV, GLA_DK), F32)
    ctx_refs = (qc_ref, kc_ref, vc_ref, vtc_ref, bfc_ref, bbc_ref)
    lat_refs = (q_ref, k_ref, v_ref, vt_ref, bf_ref, bb_ref)

    def ctx_body(i, st):
        _, st = _gla_chunks(*load(ctx_refs, i, ncc - 1 - i), masks, st, False)
        return st

    def lat_body(i, st):
        cb = nc - 1 - i
        o, st = _gla_chunks(*load(lat_refs, i, cb), masks, st, True)
        for hh in range(heads):
            of_ref[0, rows_of(i), head_cols[hh]] = o[hh]
            ob_ref[0, rows_of(cb), head_cols[hh]] = o[heads + hh]
        return st

    st = lax.fori_loop(0, ncc, ctx_body, zero)
    lax.fori_loop(0, nc, lat_body, st)


def _gla(q, k, v, vt, bf, bb, qc, kc, vc, vtc, bfc, bbc):
    bz, _, n, _ = q.shape
    ncx = qc.shape[2]
    hps = GLA_HEADS_PER_STEP

    def key(nn):
        return pl.BlockSpec((1, hps, nn, GLA_DK), lambda b, h: (b, h, 0, 0))

    def val(nn):
        return pl.BlockSpec((1, nn, hps * GLA_DV), lambda b, h: (b, 0, h))

    def valt(nn):
        return pl.BlockSpec((1, hps, nn // GLA_CHUNK, GLA_DV, GLA_CHUNK), lambda b, h: (b, h, 0, 0, 0))

    def specs(nn):
        return [key(nn), key(nn), val(nn), valt(nn), key(nn), key(nn)]

    return pl.pallas_call(
        _gla_kernel,
        grid=(bz, GLA_HEADS // hps),
        in_specs=specs(n) + specs(ncx),
        out_specs=[val(n)] * 2,
        out_shape=[jax.ShapeDtypeStruct((bz, n, GLA_WIDTH), F32)] * 2,
        compiler_params=_params("parallel", "arbitrary"),
        name="gla",
    )(q, k, v, vt, bf, bb, qc, kc, vc, vtc, bfc, bbc)


def _head_rms(h, g, width):
    parts = [_rms(h[:, i:i + width]) for i in range(0, h.shape[1], width)]
    return jnp.concatenate(parts, axis=1) * g


def _largest16(loads):
    slot = lax.broadcasted_iota(jnp.int32, (PEER_TOPK, LANES), 0)

    def body(r, carry):
        out = []
        for (m_prev, vals), load in zip(carry, loads):
            s = load()
            m = jnp.max(jnp.where(s < m_prev, s, NEG_INF), axis=0, keepdims=True)
            out.append((m, jnp.where(slot == r, m, vals)))
        return tuple(out)

    init = tuple((jnp.full((1, LANES), jnp.inf, F32), jnp.zeros((PEER_TOPK, LANES), F32))
                 for _ in loads)
    return [vals for _, vals in lax.fori_loop(0, PEER_TOPK, body, init)]


def _count(mask):
    return jnp.sum(jnp.where(mask, 1.0, 0.0), axis=0, keepdims=True)


def _top16(s):
    nk, tl = s.shape
    slot = lax.broadcasted_iota(jnp.int32, (PEER_TOPK, tl), 0)
    key_id = lax.broadcasted_iota(jnp.int32, (nk, tl), 0).astype(F32)

    def body(r, carry):
        s, rank, vals = carry
        m = jnp.max(s, axis=0, keepdims=True)
        first = jnp.min(jnp.where(s == m, key_id, float(nk)), axis=0, keepdims=True)
        sel = key_id == first
        rank = jnp.where(sel, jnp.asarray(r).astype(F32), rank)
        s = jnp.where(sel, NEG_INF, s)
        vals = jnp.where(slot == r, m, vals)
        return s, rank, vals

    init = (s, jnp.full((nk, tl), float(PEER_TOPK), F32), jnp.zeros((PEER_TOPK, tl), F32))
    _, rank, vals = lax.fori_loop(0, PEER_TOPK, body, init)
    return vals, rank


def _pair_candidates(v1, v2):
    blocks = [v1[0:1] + v2[0:8], v1[0:1] + v2[8:16]]
    blocks += [v1[a:a + 1] + v2[0:8] for a in range(1, 8)]
    blocks += [v1[8:16] + v2[0:1]]
    return jnp.concatenate(blocks, axis=0)


def _pick16(cand):
    i = lax.broadcasted_iota(jnp.int32, cand.shape, 0)
    blk, rr = i // 8, i % 8
    flat = jnp.where(blk == 0, rr, jnp.where(blk == 1, 8 + rr,
                     jnp.where(blk <= 8, (blk - 1) * 16 + rr, (8 + rr) * 16))).astype(F32)

    def body(_, carry):
        cand, picked = carry
        m = jnp.max(cand, axis=0, keepdims=True)
        first = jnp.min(jnp.where(cand == m, flat, 1e9), axis=0, keepdims=True)
        sel = flat == first
        return jnp.where(sel, NEG_INF, cand), jnp.where(sel, 1.0, picked)

    _, picked = lax.fori_loop(0, PEER_TOPK, body, (cand, jnp.zeros_like(cand)))
    return picked


def _first_key_thresholds(picked, v1):
    inf = jnp.inf
    lo = jnp.where(picked[0:8] > 0.0, v1[0:1], inf)
    for a in range(1, 8):
        lo = jnp.minimum(lo, jnp.where(picked[8 * (a + 1):8 * (a + 2)] > 0.0, v1[a:a + 1], inf))
    tail = jnp.min(jnp.where(picked[72:80] > 0.0, v1[8:16], inf), axis=0, keepdims=True)
    row = lax.broadcasted_iota(jnp.int32, lo.shape, 0)
    lo = jnp.minimum(lo, jnp.where(row == 0, tail, inf))
    hi = jnp.where(picked[8:16] > 0.0, v1[0:1], inf)
    return jnp.concatenate([lo, hi], axis=0)


def _pair_counts(cand, picked):
    z = jnp.sum(picked * jnp.exp(cand - cand[0:1]), axis=0, keepdims=True)
    n_rows = [jnp.sum(picked[0:16], axis=0, keepdims=True)]
    n_rows += [jnp.sum(picked[8 * (a + 1):8 * (a + 2)], axis=0, keepdims=True) for a in range(1, 8)]
    return jnp.concatenate(n_rows + [picked[72:80]], axis=0), z


def _mix_kernel(x_ref, mhf_ref, mhb_ref, ghf_ref, ghb_ref, mo_ref, gr_ref, mlg_ref, glag_ref,
                wout_ref, g1_ref, sh2_ref, sc2_ref, n2g_ref, wq_ref, keys_ref,
                x1_ref, h2t_ref, e1_ref, l_ref, e2_ref, rk_ref, q_scr, s_scr):
    ml = _head_rms(mhf_ref[0] + mhb_ref[0], mlg_ref[...], ML_HEAD_DIM)
    ml = _sigmoid(mo_ref[0].astype(F32)) * ml
    gl = _head_rms(ghf_ref[0] + ghb_ref[0], glag_ref[...], GLA_DV)
    gr = gr_ref[0].astype(F32)
    gl = gr * _sigmoid(gr) * gl
    mix = jnp.concatenate([ml, gl], axis=1).astype(BF16)
    x1 = x_ref[0] + g1_ref[0] * _dot(mix, wout_ref[...])
    x1_ref[0] = x1
    h2 = _rms(x1) * n2g_ref[...] * (1.0 + sc2_ref[0]) + sh2_ref[0]
    h2t_ref[...] = h2.T.astype(BF16)
    qall = _dot(h2.astype(BF16), wq_ref[...]).astype(BF16)
    for j in range(2 * PEER_HEADS):
        q_scr[j] = qall[:, j * PEER_HALF:(j + 1) * PEER_HALF]

    tm = x_ref.shape[1]
    topk = float(PEER_TOPK)
    lane_chunks = [slice(i, i + LANES) for i in range(0, tm, LANES)]

    def write_tables(p, lanes, s1, s2, v1, v2, z, lim, rank2):
        e1_ref[p, :, lanes] = jnp.exp(s1 - v1[0:1]) * (1.0 / z)
        l_ref[p, :, lanes] = lim
        e2 = jnp.exp(s2 - v2[0:1]).astype(BF16)
        e2_ref[p, :, :, lanes] = e2.reshape(PEER_NKEYS // 16, 16, LANES)
        rk_ref[p, :, :, lanes] = rank2.astype(BF16).reshape(PEER_NKEYS // 16, 16, LANES)

    def head_body(p, carry):
        for hf in range(2):
            s_scr[hf] = _dot_nt(keys_ref[2 * p + hf], q_scr[2 * p + hf])
        vals = _largest16([functools.partial(lambda hf, lanes: s_scr[hf, :, lanes], hf, lanes)
                           for hf in range(2) for lanes in lane_chunks])
        v1s, v2s = vals[:len(lane_chunks)], vals[len(lane_chunks):]
        cands = [_pair_candidates(v1, v2) for v1, v2 in zip(v1s, v2s)]
        taus = _largest16([functools.partial(lambda c: c, c) for c in cands])
        excess = jnp.zeros((1, LANES), F32)
        for lanes, v1, v2, cand, tau in zip(lane_chunks, v1s, v2s, cands, taus):
            s1, s2 = s_scr[0, :, lanes], s_scr[1, :, lanes]
            picked = jnp.where(cand >= tau[PEER_TOPK - 1:PEER_TOPK], 1.0, 0.0)
            _, z = _pair_counts(cand, picked)
            theta = _first_key_thresholds(picked, v1)
            lim = jnp.zeros_like(s1)
            rank2 = jnp.zeros_like(s2)
            for i in range(PEER_TOPK):
                lim = jnp.where(s1 >= theta[i:i + 1], float(i + 1), lim)
                rank2 = jnp.where(s2 < v2[i:i + 1], float(i + 1), rank2)
            write_tables(p, lanes, s1, s2, v1, v2, z, lim, rank2)
            n_marked = (_count(s1 >= v1[PEER_TOPK - 1:PEER_TOPK]) + _count(rank2 < topk)
                        + jnp.sum(picked, axis=0, keepdims=True))
            excess = jnp.maximum(excess, n_marked - 3.0 * topk)

        @pl.when(jnp.max(excess) > 0.0)
        def _():
            for lanes in lane_chunks:
                s1, s2 = s_scr[0, :, lanes], s_scr[1, :, lanes]
                v1, rank1 = _top16(s1)
                v2, rank2 = _top16(s2)
                cand = _pair_candidates(v1, v2)
                n, z = _pair_counts(cand, _pick16(cand))
                lim = jnp.zeros_like(rank1)
                for a in range(PEER_TOPK):
                    lim = jnp.where(rank1 == float(a), n[a:a + 1], lim)
                write_tables(p, lanes, s1, s2, v1, v2, z, lim, rank2)

        return carry

    lax.fori_loop(0, PEER_HEADS, head_body, 0)


def _mix(x, mhf, mhb, ghf, ghb, mo, gr, mlg, glag, wout, g1, sh2, sc2, n2g, wq, keys):
    bz, n, d = x.shape
    tm = min(MIX_TM, n)
    assert n % tm == 0 and tm % LANES == 0
    nt = n // tm
    t_all = bz * n

    def tok(width):
        return pl.BlockSpec((1, tm, width), lambda b, i: (b, i, 0))

    def full(a):
        return pl.BlockSpec(a.shape, lambda b, i, nd=a.ndim: (0,) * nd)

    vec = pl.BlockSpec((1, 1, d), lambda b, i: (b, 0, 0))
    tab1 = pl.BlockSpec((PEER_HEADS, PEER_NKEYS, tm), lambda b, i: (0, 0, b * nt + i))
    tab2 = pl.BlockSpec((PEER_HEADS, PEER_NKEYS // 16, 16, tm), lambda b, i: (0, 0, 0, b * nt + i))
    tab1_shape = jax.ShapeDtypeStruct((PEER_HEADS, PEER_NKEYS, t_all), F32)
    tab2_shape = jax.ShapeDtypeStruct((PEER_HEADS, PEER_NKEYS // 16, 16, t_all), BF16)
    return pl.pallas_call(
        _mix_kernel,
        grid=(bz, nt),
        in_specs=[tok(d)] + [tok(ML_WIDTH)] * 6 + [full(mlg), full(glag), full(wout), vec, vec, vec,
                                                    full(n2g), full(wq), full(keys)],
        out_specs=[tok(d), pl.BlockSpec((d, tm), lambda b, i: (0, b * nt + i)), tab1, tab1, tab2, tab2],
        out_shape=[jax.ShapeDtypeStruct((bz, n, d), F32), jax.ShapeDtypeStruct((d, t_all), BF16),
                   tab1_shape, tab1_shape, tab2_shape, tab2_shape],
        scratch_shapes=[pltpu.VMEM((2 * PEER_HEADS, tm, PEER_HALF), BF16),
                        pltpu.VMEM((2, PEER_NKEYS, tm), F32)],
        compiler_params=_params("parallel", "arbitrary"),
        name="mix",
    )(x, mhf, mhb, ghf, ghb, mo, gr, mlg, glag, wout, g1, sh2, sc2, n2g, wq, keys)


def _peer_kernel(h2t_ref, u_ref, vt_ref, e1_ref, l_ref, e2_ref, rk_ref, x1_ref, g2_ref, nfg_ref,
                 out_ref, acc_ref):
    j = pl.program_id(2)
    te, tm = u_ref.shape[0], h2t_ref.shape[1]
    rows_per_step = te // PEER_NKEYS
    tiles = PEER_NKEYS // 16

    @pl.when(j == 0)
    def _():
        acc_ref[...] = jnp.zeros_like(acc_ref)

    act = _dot(u_ref[...], h2t_ref[...])
    act = act.astype(BF16)
    act = (0.5 * act * (1.0 + lax.erf(act * 2.0 ** -0.5))).reshape(rows_per_step, tiles, 16, tm)
    parts = []
    for i in range(rows_per_step):
        w = None
        for p in range(PEER_HEADS):
            e1 = jnp.broadcast_to(e1_ref[p, i:i + 1, :], (16, tm)).astype(BF16)[None]
            lim = jnp.broadcast_to(l_ref[p, i:i + 1, :], (16, tm)).astype(BF16)[None]
            term = jnp.where(rk_ref[p] < lim, e2_ref[p] * e1, jnp.zeros((), BF16))
            w = term if w is None else w + term
        parts.append(w * act[i])
    gated = jnp.concatenate(parts, axis=0).reshape(te, tm)
    acc_ref[...] += _dot(vt_ref[...], gated)

    @pl.when(j == pl.num_programs(2) - 1)
    def _():
        xf = x1_ref[0] + g2_ref[0] * acc_ref[...].T
        out_ref[0] = _rms(xf) * nfg_ref[...]


def _peer(h2t, u, vt, e1, lim, e2, rk, x1, g2, nfg):
    bz, n, d = x1.shape
    n_exp = u.shape[0]
    tm = min(PEER_TM, n)
    te = PEER_TE
    assert n % tm == 0 and n_exp % te == 0 and te % PEER_NKEYS == 0
    nt = n // tm
    tab1 = pl.BlockSpec((PEER_HEADS, te // PEER_NKEYS, tm), lambda b, i, j: (0, j, b * nt + i))
    tab2 = pl.BlockSpec((PEER_HEADS, PEER_NKEYS // 16, 16, tm), lambda b, i, j: (0, 0, 0, b * nt + i))
    tok = pl.BlockSpec((1, tm, d), lambda b, i, j: (b, i, 0))
    return pl.pallas_call(
        _peer_kernel,
        grid=(bz, nt, n_exp // te),
        in_specs=[pl.BlockSpec((d, tm), lambda b, i, j: (0, b * nt + i)),
                  pl.BlockSpec((te, d), lambda b, i, j: (j, 0)),
                  pl.BlockSpec((d, te), lambda b, i, j: (0, j)),
                  tab1, tab1, tab2, tab2, tok,
                  pl.BlockSpec((1, 1, d), lambda b, i, j: (b, 0, 0)),
                  pl.BlockSpec((1, d), lambda b, i, j: (0, 0))],
        out_specs=tok,
        out_shape=jax.ShapeDtypeStruct((bz, n, d), F32),
        scratch_shapes=[pltpu.VMEM((d, tm), F32)],
        compiler_params=_params("parallel", "parallel", "arbitrary"),
        name="peer",
    )(h2t, u, vt, e1, lim, e2, rk, x1, g2, nfg)


def _inproj_weights(w_in, conv_w, gate_b, lr_w2, alpha_b):
    widths = (ML_WIDTH, ML_WIDTH, ML_WIDTH, ML_WIDTH, N_GATES,
              GLA_KEY_WIDTH, GLA_KEY_WIDTH, GLA_WIDTH, GLA_WIDTH, 2 * GLA_RANK)
    offs = [0]
    for w in widths:
        offs.append(offs[-1] + w)
    wb = w_in.astype(BF16)
    col = lambda a, b: wb[:, offs[a]:offs[b]]
    pad_lanes = lambda a: jnp.pad(a, ((0, 0), (0, LANES - a.shape[1])))
    w2 = jnp.zeros((LANES, 2 * GLA_KEY_WIDTH), F32)
    w2 = w2.at[:GLA_RANK, :GLA_KEY_WIDTH].set(lr_w2[0])
    w2 = w2.at[GLA_RANK:2 * GLA_RANK, GLA_KEY_WIDTH:].set(lr_w2[1])
    return (col(0, 2), col(2, 3), col(3, 4), pad_lanes(col(4, 5)), col(4, 5).T,
            col(5, 7), col(7, 8), col(8, 9), pad_lanes(col(9, 10)),
            conv_w, pad_lanes(gate_b[None, :]), gate_b[:, None],
            w2.astype(BF16), alpha_b.reshape(1, 2 * GLA_KEY_WIDTH))


def _per_head_gates(gtok, gmaj):
    bz, n, _ = gtok.shape
    gt = gtok.reshape(bz, n, 2, 2, ML_HEADS).transpose(0, 4, 1, 2, 3).reshape(bz, ML_HEADS, n, 4)
    gm = gmaj.reshape(bz, 2, 2, ML_HEADS, n // ML_CHUNK, ML_CHUNK)
    gm = gm.transpose(0, 3, 4, 1, 2, 5).reshape(bz, ML_HEADS, n // ML_CHUNK, 4, ML_CHUNK)
    return gt, gm


def _token_mix_inputs(x, sh, sc, ng, wts, row_len):
    mq, mk, mv, mo, gtok, gmaj, gq, gk, gv, gr, bf, bb, mkt, gvt = _inproj(x, sh, sc, ng, wts, row_len)
    gt, gm = _per_head_gates(gtok, gmaj)
    return (mq, mk, mkt, mv, gt, gm), (gq, gk, gv, gvt, bf, bb), mo, gr


def kernel(x, c, ctx, c_ctx, w_mod, b_mod, norm1_g, w_in, ml_conv_w, ml_gate_b, ml_norm_g,
           gla_lr_w2, gla_alpha_b, gla_norm_g, w_out, norm2_g, peer_wq, peer_keys, peer_u,
           peer_v, norm_f_g):
    assert w_mod.shape[0] == 1, "single trunk layer"
    bz, n, d = x.shape
    c_all = jnp.concatenate([c, c_ctx[None, :]], axis=0)
    c_all = jnp.pad(c_all, ((0, (-c_all.shape[0]) % SUBLANES), (0, 0)))
    mod = _modulation(c_all, w_mod[0], b_mod[0][None, :])
    sh1, sc1, g1, sh2, sc2, g2 = [m[:, None, :] for m in jnp.split(mod[:bz], 6, axis=1)]
    mod_c = mod[bz]
    csh1, csc1 = mod_c[None, None, :d], mod_c[None, None, d:2 * d]

    ng1 = norm1_g[0][None, :]
    wts = _inproj_weights(w_in[0], ml_conv_w[0], ml_gate_b[0], gla_lr_w2[0], gla_alpha_b[0])
    ml_lat, gla_lat, mo, gr = _token_mix_inputs(x, sh1, sc1, ng1, wts, GRID_W)
    ml_ctx, gla_ctx, _, _ = _token_mix_inputs(ctx, csh1, csc1, ng1, wts, ctx.shape[1])
    mhf, mhb = _mlstm(*ml_lat, *ml_ctx)
    ghf, ghb = _gla(*gla_lat, *gla_ctx)

    keys = peer_keys[0].reshape(2 * PEER_HEADS, PEER_NKEYS, PEER_HALF).astype(BF16)
    x1, h2t, e1, lim, e2, rk = _mix(
        x, mhf, mhb, ghf, ghb, mo, gr, ml_norm_g[0][None, :], gla_norm_g[0][None, :],
        w_out[0].astype(BF16), g1, sh2, sc2, norm2_g[0][None, :], peer_wq[0].astype(BF16), keys)
    return _peer(h2t, peer_u[0].astype(BF16), peer_v[0].astype(BF16).T, e1, lim, e2, rk,
                 x1, g2, norm_f_g[None, :])
```

```python
import functools

import jax
import jax.numpy as jnp
from jax import lax
from jax.experimental import pallas as pl
from jax.experimental.pallas import tpu as pltpu

F32 = jnp.float32
BF16 = jnp.bfloat16

EPS = 1e-6
D_MODEL = 1024
GRID_W = 64

ML_HEADS = 4
ML_HEAD_DIM = 128
ML_WIDTH = ML_HEADS * ML_HEAD_DIM
ML_CHUNK = 128
GLA_HEADS = 4
GLA_DK = 64
GLA_DV = 128
GLA_KEY_WIDTH = GLA_HEADS * GLA_DK
GLA_WIDTH = GLA_HEADS * GLA_DV
GLA_RANK = 16
GLA_TAU = 16.0
GLA_CHUNK = 64
N_GATES = 4 * ML_HEADS

PEER_HEADS = 8
PEER_NKEYS = 128
PEER_TOPK = 16
PEER_HALF = 128

LANES = 128
SUBLANES = 8
VMEM_LIMIT = 56 * 1024 * 1024

INPROJ_TM = 512
CUMSUM_ROWS = 256
ML_HEADS_PER_STEP = 4
GLA_HEADS_PER_STEP = 2
MIX_TM = 512
PEER_TM = 512
PEER_TE = 2048

NEG_INF = float("-inf")


def _params(*sem):
    return pltpu.CompilerParams(dimension_semantics=sem, vmem_limit_bytes=VMEM_LIMIT)


def _dot(a, b):
    return jnp.dot(a, b, preferred_element_type=F32)


def _dot_nt(a, b):
    return lax.dot_general(a, b, (((1,), (1,)), ((), ())), preferred_element_type=F32)


def _split2(x):
    hi = x.astype(BF16)
    return hi, (x - hi.astype(F32)).astype(BF16)


def _dot_exact_rhs(a01, x):
    hi, lo = _split2(x)
    return _dot(a01, hi) + _dot(a01, lo)


def _dot_exact_lhs(x, a01):
    hi, lo = _split2(x)
    return _dot(hi, a01) + _dot(lo, a01)


def _block_rows(mask, x):
    r = mask.shape[0]
    return jnp.concatenate([_dot_exact_rhs(mask, x[i:i + r]) for i in range(0, x.shape[0], r)], axis=0)


def _block_cols(x, mask):
    r = mask.shape[0]
    return jnp.concatenate([_dot_exact_lhs(x[:, i:i + r], mask) for i in range(0, x.shape[1], r)], axis=1)


def _sigmoid(x):
    return 1.0 / (1.0 + jnp.exp(-x))


def _log_sigmoid(x):
    return jnp.minimum(x, 0.0) - jnp.log(1.0 + jnp.exp(-jnp.abs(x)))


def _rms(x):
    return x * lax.rsqrt(jnp.mean(x * x, axis=-1, keepdims=True) + EPS)


def _mod_kernel(c_ref, w_ref, b_ref, o_ref):
    cond = c_ref[...]
    cond = cond * _sigmoid(cond)
    ch, cl = cond.astype(BF16), (cond - cond.astype(BF16).astype(F32)).astype(BF16)
    w = w_ref[...]
    wh = w.astype(BF16)
    wl = (w - wh.astype(F32)).astype(BF16)
    o_ref[...] = _dot(ch, wh) + _dot(ch, wl) + _dot(cl, wh) + b_ref[...]


def _modulation(c_all, w_mod, b_mod):
    rows, d = c_all.shape
    n_out = w_mod.shape[1]
    tn = 512
    return pl.pallas_call(
        _mod_kernel,
        grid=(n_out // tn,),
        in_specs=[pl.BlockSpec((rows, d), lambda j: (0, 0)),
                  pl.BlockSpec((d, tn), lambda j: (0, j)),
                  pl.BlockSpec((1, tn), lambda j: (0, j))],
        out_specs=pl.BlockSpec((rows, tn), lambda j: (0, j)),
        out_shape=jax.ShapeDtypeStruct((rows, n_out), F32),
        compiler_params=_params("arbitrary"),
        name="mod",
    )(c_all, w_mod, b_mod)


def _chunk_masks(tm, chunk):
    r = lax.broadcasted_iota(jnp.int32, (tm, tm), 0)
    c = lax.broadcasted_iota(jnp.int32, (tm, tm), 1)
    same = (r // chunk) == (c // chunk)
    prefix = jnp.where(same & (c <= r), 1.0, 0.0).astype(BF16)
    suffix = jnp.where(same & (c >= r), 1.0, 0.0).astype(BF16)
    return prefix, suffix


def _inproj_kernel(x_ref, sh_ref, sc_ref, ng_ref, wqk_ref, wv_ref, wo_ref, wg_ref, wgt_ref,
                   wgqk_ref, wgv_ref, wgr_ref, wlr_ref, conv_ref, gb_ref, gbc_ref, w2_ref, ab_ref,
                   mq_ref, mk_ref, mv_ref, mo_ref, gtok_ref, gmaj_ref,
                   gq_ref, gk_ref, gv_ref, gr_ref, bf_ref, bb_ref, mkt_ref, gvt_ref, *, row_len):
    tm = x_ref.shape[1]
    x = x_ref[0]
    h = _rms(x) * ng_ref[...] * (1.0 + sc_ref[0]) + sh_ref[0]
    hb = h.astype(BF16)

    u = _dot(hb, wqk_ref[...])
    pos = lax.broadcasted_iota(jnp.int32, (tm, 1), 0) % row_len
    up = jnp.where(pos == 0, 0.0, pltpu.roll(u, 1, 0))
    dn = jnp.where(pos == row_len - 1, 0.0, pltpu.roll(u, tm - 1, 0))
    cw = conv_ref[...]
    y = cw[0:1] * up + cw[1:2] * u + cw[2:3] * dn
    y = y * _sigmoid(y)
    mq_ref[0] = y[:, :ML_WIDTH].astype(BF16)
    mk = y[:, ML_WIDTH:] * ML_HEAD_DIM ** -0.5
    mk_ref[0] = mk.astype(BF16)
    mkt = mk.T.astype(BF16)
    for hh in range(ML_HEADS):
        for ci in range(tm // ML_CHUNK):
            mkt_ref[0, hh, ci] = mkt[hh * ML_HEAD_DIM:(hh + 1) * ML_HEAD_DIM, ci * ML_CHUNK:(ci + 1) * ML_CHUNK]
    mv_ref[0] = _dot(hb, wv_ref[...]).astype(BF16)
    mo_ref[0] = _dot(hb, wo_ref[...]).astype(BF16)

    pre128, suf128 = _chunk_masks(min(tm, CUMSUM_ROWS), ML_CHUNK)
    g = _dot(hb, wg_ref[...]) + gb_ref[...]
    col = lax.broadcasted_iota(jnp.int32, (1, LANES), 1)
    is_f = ((col % 8) >= 4) & (col < N_GATES)
    is_bwd = col >= 8
    lf = jnp.where(is_f, _log_sigmoid(g), 0.0)
    cum = jnp.where(is_bwd, _block_rows(suf128, lf), _block_rows(pre128, lf))
    gtok_ref[0] = jnp.where(is_f, cum, g)[:, :N_GATES]

    gt = _dot_nt(wgt_ref[...], hb) + gbc_ref[...]
    row = lax.broadcasted_iota(jnp.int32, (N_GATES, 1), 0)
    is_f_r = (row % 8) >= 4
    lft = jnp.where(is_f_r, _log_sigmoid(gt), 0.0)
    cum_t = jnp.where(row >= 8, _block_cols(lft, pre128), _block_cols(lft, suf128))
    gmaj_ref[0] = jnp.where(is_f_r, cum_t, gt)

    gqk = _dot(hb, wgqk_ref[...])
    def per_head(ref, a):
        for hh in range(GLA_HEADS):
            ref[0, hh] = a[:, hh * GLA_DK:(hh + 1) * GLA_DK]

    per_head(gq_ref, (gqk[:, :GLA_KEY_WIDTH] * GLA_DK ** -0.5).astype(BF16))
    per_head(gk_ref, gqk[:, GLA_KEY_WIDTH:].astype(BF16))
    gv = _dot(hb, wgv_ref[...])
    gv_ref[0] = gv.astype(BF16)
    gvt = gv.T.astype(BF16)
    for hh in range(GLA_HEADS):
        for ci in range(tm // GLA_CHUNK):
            gvt_ref[0, hh, ci] = gvt[hh * GLA_DV:(hh + 1) * GLA_DV, ci * GLA_CHUNK:(ci + 1) * GLA_CHUNK]
    gr_ref[0] = _dot(hb, wgr_ref[...]).astype(BF16)
    glr = _dot(hb, wlr_ref[...]).astype(BF16)
    alpha = _dot(glr, w2_ref[...]) + ab_ref[...]
    la = _log_sigmoid(alpha) * (1.0 / GLA_TAU)
    pre64, suf64 = _chunk_masks(min(tm, CUMSUM_ROWS), GLA_CHUNK)
    per_head(bf_ref, _block_rows(pre64, la[:, :GLA_KEY_WIDTH]))
    per_head(bb_ref, _block_rows(suf64, la[:, GLA_KEY_WIDTH:]))


def _inproj(x, sh, sc, ng, wts, row_len):
    bz, n, d = x.shape
    tm = min(INPROJ_TM, n)
    assert n % tm == 0 and tm % row_len == 0 and tm % ML_CHUNK == 0
    grid = (bz, n // tm)
    per_b = (lambda b, i: (b, 0, 0)) if sh.shape[0] == bz else (lambda b, i: (0, 0, 0))

    def full(a):
        return pl.BlockSpec(a.shape, lambda b, i, nd=a.ndim: (0,) * nd)

    def tok(width):
        return pl.BlockSpec((1, tm, width), lambda b, i: (b, i, 0))

    def out(width, dtype):
        return jax.ShapeDtypeStruct((bz, n, width), dtype)

    in_specs = [tok(d), pl.BlockSpec((1, 1, d), per_b), pl.BlockSpec((1, 1, d), per_b), full(ng)]
    in_specs += [full(w) for w in wts]
    out_specs = [tok(ML_WIDTH)] * 4 + [tok(N_GATES), pl.BlockSpec((1, N_GATES, tm), lambda b, i: (b, 0, i))]
    key = pl.BlockSpec((1, GLA_HEADS, tm, GLA_DK), lambda b, i: (b, 0, i, 0))
    out_specs += [key] * 2 + [tok(GLA_WIDTH)] * 2 + [key] * 2
    out_shape = [out(ML_WIDTH, BF16)] * 4 + [out(N_GATES, F32),
                                              jax.ShapeDtypeStruct((bz, N_GATES, n), F32)]
    keys = lambda dtype: jax.ShapeDtypeStruct((bz, GLA_HEADS, n, GLA_DK), dtype)
    out_shape += [keys(BF16)] * 2 + [out(GLA_WIDTH, BF16)] * 2 + [keys(F32)] * 2
    for heads, dim, chunk in ((ML_HEADS, ML_HEAD_DIM, ML_CHUNK), (GLA_HEADS, GLA_DV, GLA_CHUNK)):
        out_specs.append(pl.BlockSpec((1, heads, tm // chunk, dim, chunk), lambda b, i: (b, 0, i, 0, 0)))
        out_shape.append(jax.ShapeDtypeStruct((bz, heads, n // chunk, dim, chunk), BF16))
    return pl.pallas_call(
        functools.partial(_inproj_kernel, row_len=row_len),
        grid=grid, in_specs=in_specs, out_specs=out_specs, out_shape=out_shape,
        compiler_params=_params("parallel", "arbitrary"),
        name="inproj",
    )(x, sh, sc, ng, *wts)


def _bmm(a, b):
    return lax.dot_general(a, b, (((2,), (1,)), ((0,), (0,))), preferred_element_type=F32)


def _bmm_nt(a, b):
    return lax.dot_general(a, b, (((2,), (2,)), ((0,), (0,))), preferred_element_type=F32)


def _by_direction(x, fwd, bwd):
    half = x.shape[0] // 2
    return jnp.concatenate([fwd(x[:half]), bwd(x[half:])], axis=0)


def _ml_chunks(q, k, kt, v, gt, gm, masks, state, want_out):
    ct, nv, m = state
    ll = q.shape[1]
    ig_col = _by_direction(gt, lambda g: g[:, :, 0:1], lambda g: g[:, :, 2:3])
    b_col = _by_direction(gt, lambda g: g[:, :, 1:2], lambda g: g[:, :, 3:4])
    ig_row = _by_direction(gm, lambda g: g[:, 0:1, :], lambda g: g[:, 2:3, :])
    b_row = _by_direction(gm, lambda g: g[:, 1:2, :], lambda g: g[:, 3:4, :])
    b_end = _by_direction(b_row, lambda b: b[:, :, ll - 1:ll], lambda b: b[:, :, 0:1])
    h = None
    if want_out:
        d_log = _by_direction(b_col - b_row + ig_row,
                              lambda x: jnp.where(masks[0], x, NEG_INF),
                              lambda x: jnp.where(masks[1], x, NEG_INF))
        inter_log = b_col + m
        m_t = jnp.maximum(inter_log, jnp.max(d_log, axis=2, keepdims=True))
        scores = _bmm_nt(q, k) * jnp.exp(d_log - m_t)
        inter = jnp.exp(inter_log - m_t)
        qf = q.astype(F32)
        num = _bmm(scores.astype(BF16), v) + inter * _bmm(q, ct.astype(BF16))
        den = jnp.sum(scores, axis=2, keepdims=True) + inter * jnp.sum(qf * nv, axis=2, keepdims=True)
        h = num / jnp.maximum(jnp.abs(den), jnp.exp(-m_t))
    m_loc = jnp.max(b_end - b_row + ig_row, axis=2, keepdims=True)
    w_col = jnp.exp(b_end - b_col + ig_col - m_loc)
    c_loc = _bmm(kt, (w_col * v.astype(F32)).astype(BF16))
    n_loc = jnp.sum(w_col * k.astype(F32), axis=1, keepdims=True)
    m_new = jnp.maximum(b_end + m, m_loc)
    a = jnp.exp(b_end + m - m_new)
    bb = jnp.exp(m_loc - m_new)
    return h, (a * ct + bb * c_loc, a * nv + bb * n_loc, m_new)


def _mlstm_kernel(q_ref, k_ref, kt_ref, v_ref, gt_ref, gm_ref,
                  qc_ref, kc_ref, ktc_ref, vc_ref, gtc_ref, gmc_ref, hf_ref, hb_ref):
    ll = ML_CHUNK
    nc = q_ref.shape[1] // ll
    ncc = qc_ref.shape[1] // ll
    r = lax.broadcasted_iota(jnp.int32, (ll, ll), 0)
    c = lax.broadcasted_iota(jnp.int32, (ll, ll), 1)
    masks = (c <= r, c >= r)

    heads = gt_ref.shape[1]
    head_cols = [slice(hh * ML_HEAD_DIM, (hh + 1) * ML_HEAD_DIM) for hh in range(heads)]

    def rows_of(ci):
        return pl.ds(pl.multiple_of(ci * ll, ll), ll)

    def load(refs, cf, cb):
        qr, kr, ktr, vr, gtr, gmr = refs
        sites = [(hh, ci) for ci in (cf, cb) for hh in range(heads)]
        seq = lambda r: jnp.stack([r[0, rows_of(ci), head_cols[hh]] for hh, ci in sites])
        return (seq(qr), seq(kr), jnp.stack([ktr[0, hh, ci] for hh, ci in sites]), seq(vr),
                jnp.stack([gtr[0, hh, rows_of(ci), :] for hh, ci in sites]),
                jnp.stack([gmr[0, hh, ci] for hh, ci in sites]))

    chains = 2 * heads
    zero = (jnp.zeros((chains, ML_HEAD_DIM, ML_HEAD_DIM), F32), jnp.zeros((chains, 1, ML_HEAD_DIM), F32),
            jnp.zeros((chains, 1, 1), F32))
    ctx_refs = (qc_ref, kc_ref, ktc_ref, vc_ref, gtc_ref, gmc_ref)
    lat_refs = (q_ref, k_ref, kt_ref, v_ref, gt_ref, gm_ref)

    def ctx_body(i, state):
        _, state = _ml_chunks(*load(ctx_refs, i, ncc - 1 - i), masks, state, False)
        return state

    def lat_body(i, state):
        cb = nc - 1 - i
        h, state = _ml_chunks(*load(lat_refs, i, cb), masks, state, True)
        for hh in range(heads):
            hf_ref[0, rows_of(i), head_cols[hh]] = h[hh]
            hb_ref[0, rows_of(cb), head_cols[hh]] = h[heads + hh]
        return state

    state = lax.fori_loop(0, ncc, ctx_body, zero)
    lax.fori_loop(0, nc, lat_body, state)


def _mlstm(q, k, kt, v, gt, gm, qc, kc, ktc, vc, gtc, gmc):
    bz, n, _ = q.shape
    ncx = qc.shape[1]
    hps = ML_HEADS_PER_STEP

    def seq(nn):
        return pl.BlockSpec((1, nn, hps * ML_HEAD_DIM), lambda b, h: (b, 0, h))

    def ktr(nn):
        return pl.BlockSpec((1, hps, nn // ML_CHUNK, ML_HEAD_DIM, ML_CHUNK), lambda b, h: (b, h, 0, 0, 0))

    def gtok(nn):
        return pl.BlockSpec((1, hps, nn, 4), lambda b, h: (b, h, 0, 0))

    def gmaj(nn):
        return pl.BlockSpec((1, hps, nn // ML_CHUNK, 4, ML_CHUNK), lambda b, h: (b, h, 0, 0, 0))

    def specs(nn):
        return [seq(nn), seq(nn), ktr(nn), seq(nn), gtok(nn), gmaj(nn)]

    return pl.pallas_call(
        _mlstm_kernel,
        grid=(bz, ML_HEADS // hps),
        in_specs=specs(n) + specs(ncx),
        out_specs=[seq(n)] * 2,
        out_shape=[jax.ShapeDtypeStruct((bz, n, ML_WIDTH), F32)] * 2,
        compiler_params=_params("parallel", "arbitrary"),
        name="mlstm",
    )(q, k, kt, v, gt, gm, qc, kc, ktc, vc, gtc, gmc)


def _gla_chunks(q, k, v, vt, b, masks, st, want_out):
    ll = q.shape[1]
    qf, kf = q.astype(F32), k.astype(F32)
    ref = _by_direction(b, lambda x: x[:, ll // 2:ll // 2 + 1], lambda x: x[:, ll // 2 - 1:ll // 2])
    b_end = _by_direction(b, lambda x: x[:, ll - 1:ll], lambda x: x[:, 0:1])
    o = None
    if want_out:
        att = _bmm_nt((qf * jnp.exp(b - ref)).astype(BF16), (kf * jnp.exp(ref - b)).astype(BF16))
        att = _by_direction(att, lambda x: jnp.where(masks[0], x, 0.0), lambda x: jnp.where(masks[1], x, 0.0))
        o = _bmm(att.astype(BF16), v) + _bmm_nt((qf * jnp.exp(b)).astype(BF16), st.astype(BF16))
    s_loc = _bmm(vt, (kf * jnp.exp(b_end - b)).astype(BF16))
    return o, jnp.exp(b_end) * st + s_loc


def _gla_kernel(q_ref, k_ref, v_ref, vt_ref, bf_ref, bb_ref,
                qc_ref, kc_ref, vc_ref, vtc_ref, bfc_ref, bbc_ref, of_ref, ob_ref):
    ll = GLA_CHUNK
    nc = q_ref.shape[2] // ll
    ncc = qc_ref.shape[2] // ll
    r = lax.broadcasted_iota(jnp.int32, (ll, ll), 0)
    c = lax.broadcasted_iota(jnp.int32, (ll, ll), 1)
    masks = (c <= r, c >= r)

    heads = q_ref.shape[1]

    def rows_of(ci):
        return pl.ds(pl.multiple_of(ci * ll, ll), ll)

    head_cols = [slice(hh * GLA_DV, (hh + 1) * GLA_DV) for hh in range(heads)]

    def load(refs, cf, cb):
        qr, kr, vr, vtr, bfr, bbr = refs
        sites = [(hh, ci) for ci in (cf, cb) for hh in range(heads)]
        per_head = lambda r: jnp.stack([r[0, hh, rows_of(ci), :] for hh, ci in sites])
        return (per_head(qr), per_head(kr),
                jnp.stack([vr[0, rows_of(ci), head_cols[hh]] for hh, ci in sites]),
                jnp.stack([vtr[0, hh, ci] for hh, ci in sites]),
                jnp.stack([bfr[0, hh, rows_of(cf), :] for hh in range(heads)]
                          + [bbr[0, hh, rows_of(cb), :] for hh in range(heads)]))

    zero = jnp.zeros((2 * heads, GLA_DV, GLA_DK), F32)
    ctx_refs = (qc_ref, kc_ref, vc_ref, vtc_ref, bfc_ref, bbc_ref)
    lat_refs = (q_ref, k_ref, v_ref, vt_ref, bf_ref, bb_ref)

    def ctx_body(i, st):
        _, st = _gla_chunks(*load(ctx_refs, i, ncc - 1 - i), masks, st, False)
        return st

    def lat_body(i, st):
        cb = nc - 1 - i
        o, st = _gla_chunks(*load(lat_refs, i, cb), masks, st, True)
        for hh in range(heads):
            of_ref[0, rows_of(i), head_cols[hh]] = o[hh]
            ob_ref[0, rows_of(cb), head_cols[hh]] = o[heads + hh]
        return st

    st = lax.fori_loop(0, ncc, ctx_body, zero)
    lax.fori_loop(0, nc, lat_body, st)


def _gla(q, k, v, vt, bf, bb, qc, kc, vc, vtc, bfc, bbc):
    bz, _, n, _ = q.shape
    ncx = qc.shape[2]
    hps = GLA_HEADS_PER_STEP

    def key(nn):
        return pl.BlockSpec((1, hps, nn, GLA_DK), lambda b, h: (b, h, 0, 0))

    def val(nn):
        return pl.BlockSpec((1, nn, hps * GLA_DV), lambda b, h: (b, 0, h))

    def valt(nn):
        return pl.BlockSpec((1, hps, nn // GLA_CHUNK, GLA_DV, GLA_CHUNK), lambda b, h: (b, h, 0, 0, 0))

    def specs(nn):
        return [key(nn), key(nn), val(nn), valt(nn), key(nn), key(nn)]

    return pl.pallas_call(
        _gla_kernel,
        grid=(bz, GLA_HEADS // hps),
        in_specs=specs(n) + specs(ncx),
        out_specs=[val(n)] * 2,
        out_shape=[jax.ShapeDtypeStruct((bz, n, GLA_WIDTH), F32)] * 2,
        compiler_params=_params("parallel", "arbitrary"),
        name="gla",
    )(q, k, v, vt, bf, bb, qc, kc, vc, vtc, bfc, bbc)


def _head_rms(h, g, width):
    parts = [_rms(h[:, i:i + width]) for i in range(0, h.shape[1], width)]
    return jnp.concatenate(parts, axis=1) * g


def _oddeven_merge(lo, hi, r):
    step = r * 2
    if step < hi - lo:
        yield from _oddeven_merge(lo, hi, step)
        yield from _oddeven_merge(lo + r, hi, step)
        yield from [(i, i + r) for i in range(lo + r, hi - r, step)]
    else:
        yield (lo, lo + r)


def _oddeven_merge_sort(lo, hi):
    if hi - lo >= 1:
        mid = lo + (hi - lo) // 2
        yield from _oddeven_merge_sort(lo, mid)
        yield from _oddeven_merge_sort(mid + 1, hi)
        yield from _oddeven_merge(lo, hi, 1)


def _exchange(v, i, j):
    v[i], v[j] = jnp.maximum(v[i], v[j]), jnp.minimum(v[i], v[j])


def _sorted_top16(tiles):
    v = list(tiles)
    n = len(v)
    for i, j in _oddeven_merge_sort(0, n - 1):
        _exchange(v, i, j)
    for shift in (4, 2, 1):
        other = [pltpu.roll(x, shift, 0) for x in v]
        v = [jnp.maximum(v[j], other[n - 1 - j]) for j in range(n)]
        for d in (8, 4, 2, 1):
            for j in range(n):
                if not j & d:
                    _exchange(v, j, j + d)
    sub = lax.broadcasted_iota(jnp.int32, v[0].shape, 0)
    halves = []
    for lo in (0, SUBLANES):
        rows = v[lo]
        for r in range(1, SUBLANES):
            rows = jnp.where(sub == r, v[lo + r], rows)
        halves.append(rows)
    repeated = jnp.zeros_like(v[0])
    for r in range(n - 1):
        repeated = jnp.where(v[r] == v[r + 1], 1.0, repeated)
    return jnp.concatenate(halves, axis=0), repeated[0:1]


def _largest16(loads):
    slot = lax.broadcasted_iota(jnp.int32, (PEER_TOPK, LANES), 0)

    def body(r, carry):
        out = []
        for (m_prev, vals), load in zip(carry, loads):
            s = load()
            m = jnp.max(jnp.where(s < m_prev, s, NEG_INF), axis=0, keepdims=True)
            out.append((m, jnp.where(slot == r, m, vals)))
        return tuple(out)

    init = tuple((jnp.full((1, LANES), jnp.inf, F32), jnp.zeros((PEER_TOPK, LANES), F32))
                 for _ in loads)
    return [vals for _, vals in lax.fori_loop(0, PEER_TOPK, body, init)]


def _count(mask):
    return jnp.sum(jnp.where(mask, 1.0, 0.0), axis=0, keepdims=True)


def _top16(s):
    nk, tl = s.shape
    slot = lax.broadcasted_iota(jnp.int32, (PEER_TOPK, tl), 0)
    key_id = lax.broadcasted_iota(jnp.int32, (nk, tl), 0).astype(F32)

    def body(r, carry):
        s, rank, vals = carry
        m = jnp.max(s, axis=0, keepdims=True)
        first = jnp.min(jnp.where(s == m, key_id, float(nk)), axis=0, keepdims=True)
        sel = key_id == first
        rank = jnp.where(sel, jnp.asarray(r).astype(F32), rank)
        s = jnp.where(sel, NEG_INF, s)
        vals = jnp.where(slot == r, m, vals)
        return s, rank, vals

    init = (s, jnp.full((nk, tl), float(PEER_TOPK), F32), jnp.zeros((PEER_TOPK, tl), F32))
    _, rank, vals = lax.fori_loop(0, PEER_TOPK, body, init)
    return vals, rank


def _pair_candidates(v1, v2):
    blocks = [v1[0:1] + v2[0:8], v1[0:1] + v2[8:16]]
    blocks += [v1[a:a + 1] + v2[0:8] for a in range(1, 8)]
    blocks += [v1[8:16] + v2[0:1]]
    return jnp.concatenate(blocks, axis=0)


def _pick16(cand):
    i = lax.broadcasted_iota(jnp.int32, cand.shape, 0)
    blk, rr = i // 8, i % 8
    flat = jnp.where(blk == 0, rr, jnp.where(blk == 1, 8 + rr,
                     jnp.where(blk <= 8, (blk - 1) * 16 + rr, (8 + rr) * 16))).astype(F32)

    def body(_, carry):
        cand, picked = carry
        m = jnp.max(cand, axis=0, keepdims=True)
        first = jnp.min(jnp.where(cand == m, flat, 1e9), axis=0, keepdims=True)
        sel = flat == first
        return jnp.where(sel, NEG_INF, cand), jnp.where(sel, 1.0, picked)

    _, picked = lax.fori_loop(0, PEER_TOPK, body, (cand, jnp.zeros_like(cand)))
    return picked


def _first_key_thresholds(picked, v1):
    inf = jnp.inf
    lo = jnp.where(picked[0:8] > 0.0, v1[0:1], inf)
    for a in range(1, 8):
        lo = jnp.minimum(lo, jnp.where(picked[8 * (a + 1):8 * (a + 2)] > 0.0, v1[a:a + 1], inf))
    tail = jnp.min(jnp.where(picked[72:80] > 0.0, v1[8:16], inf), axis=0, keepdims=True)
    row = lax.broadcasted_iota(jnp.int32, lo.shape, 0)
    lo = jnp.minimum(lo, jnp.where(row == 0, tail, inf))
    hi = jnp.where(picked[8:16] > 0.0, v1[0:1], inf)
    return jnp.concatenate([lo, hi], axis=0)


def _pair_counts(cand, picked):
    z = jnp.sum(picked * jnp.exp(cand - cand[0:1]), axis=0, keepdims=True)
    n_rows = [jnp.sum(picked[0:16], axis=0, keepdims=True)]
    n_rows += [jnp.sum(picked[8 * (a + 1):8 * (a + 2)], axis=0, keepdims=True) for a in range(1, 8)]
    return jnp.concatenate(n_rows + [picked[72:80]], axis=0), z


def _mix_kernel(x_ref, mhf_ref, mhb_ref, ghf_ref, ghb_ref, mo_ref, gr_ref, mlg_ref, glag_ref,
                wout_ref, g1_ref, sh2_ref, sc2_ref, n2g_ref, wq_ref, keys_ref,
                x1_ref, h2t_ref, e1_ref, l_ref, e2_ref, rk_ref, q_scr, s_scr):
    ml = _head_rms(mhf_ref[0] + mhb_ref[0], mlg_ref[...], ML_HEAD_DIM)
    ml = _sigmoid(mo_ref[0].astype(F32)) * ml
    gl = _head_rms(ghf_ref[0] + ghb_ref[0], glag_ref[...], GLA_DV)
    gr = gr_ref[0].astype(F32)
    gl = gr * _sigmoid(gr) * gl
    mix = jnp.concatenate([ml, gl], axis=1).astype(BF16)
    x1 = x_ref[0] + g1_ref[0] * _dot(mix, wout_ref[...])
    x1_ref[0] = x1
    h2 = _rms(x1) * n2g_ref[...] * (1.0 + sc2_ref[0]) + sh2_ref[0]
    h2t_ref[...] = h2.T.astype(BF16)
    qall = _dot(h2.astype(BF16), wq_ref[...]).astype(BF16)
    for j in range(2 * PEER_HEADS):
        q_scr[j] = qall[:, j * PEER_HALF:(j + 1) * PEER_HALF]

    tm = x_ref.shape[1]
    topk = float(PEER_TOPK)
    lane_chunks = [slice(i, i + LANES) for i in range(0, tm, LANES)]

    def write_tables(p, lanes, s1, s2, v1, v2, z, lim, rank2):
        e1_ref[p, :, lanes] = jnp.exp(s1 - v1[0:1]) * (1.0 / z)
        l_ref[p, :, lanes] = lim
        e2 = jnp.exp(s2 - v2[0:1]).astype(BF16)
        e2_ref[p, :, :, lanes] = e2.reshape(PEER_NKEYS // 16, 16, LANES)
        rk_ref[p, :, :, lanes] = rank2.astype(BF16).reshape(PEER_NKEYS // 16, 16, LANES)

    def head_body(p, carry):
        for hf in range(2):
            s_scr[hf] = _dot_nt(keys_ref[2 * p + hf], q_scr[2 * p + hf])
        excess = jnp.zeros((1, LANES), F32)
        vals = []
        for hf in range(2):
            for lanes in lane_chunks:
                tiles = [s_scr[hf, i:i + SUBLANES, lanes] for i in range(0, PEER_NKEYS, SUBLANES)]
                top, repeated = _sorted_top16(tiles)
                vals.append(top)
                excess = jnp.maximum(excess, repeated)
        v1s, v2s = vals[:len(lane_chunks)], vals[len(lane_chunks):]
        cands = [_pair_candidates(v1, v2) for v1, v2 in zip(v1s, v2s)]
        taus = _largest16([functools.partial(lambda c: c, c) for c in cands])
        for lanes, v1, v2, cand, tau in zip(lane_chunks, v1s, v2s, cands, taus):
            s1, s2 = s_scr[0, :, lanes], s_scr[1, :, lanes]
            picked = jnp.where(cand >= tau[PEER_TOPK - 1:PEER_TOPK], 1.0, 0.0)
            _, z = _pair_counts(cand, picked)
            theta = _first_key_thresholds(picked, v1)
            lim = jnp.zeros_like(s1)
            rank2 = jnp.zeros_like(s2)
            for i in range(PEER_TOPK):
                lim = jnp.where(s1 >= theta[i:i + 1], float(i + 1), lim)
                rank2 = jnp.where(s2 < v2[i:i + 1], float(i + 1), rank2)
            write_tables(p, lanes, s1, s2, v1, v2, z, lim, rank2)
            n_marked = (_count(s1 >= v1[PEER_TOPK - 1:PEER_TOPK]) + _count(rank2 < topk)
                        + jnp.sum(picked, axis=0, keepdims=True))
            excess = jnp.maximum(excess, n_marked - 3.0 * topk)

        @pl.when(jnp.max(excess) > 0.0)
        def _():
            for lanes in lane_chunks:
                s1, s2 = s_scr[0, :, lanes], s_scr[1, :, lanes]
                v1, rank1 = _top16(s1)
                v2, rank2 = _top16(s2)
                cand = _pair_candidates(v1, v2)
                n, z = _pair_counts(cand, _pick16(cand))
                lim = jnp.zeros_like(rank1)
                for a in range(PEER_TOPK):
                    lim = jnp.where(rank1 == float(a), n[a:a + 1], lim)
                write_tables(p, lanes, s1, s2, v1, v2, z, lim, rank2)

        return carry

    lax.fori_loop(0, PEER_HEADS, head_body, 0)


def _mix(x, mhf, mhb, ghf, ghb, mo, gr, mlg, glag, wout, g1, sh2, sc2, n2g, wq, keys):
    bz, n, d = x.shape
    tm = min(MIX_TM, n)
    assert n % tm == 0 and tm % LANES == 0
    nt = n // tm
    t_all = bz * n

    def tok(width):
        return pl.BlockSpec((1, tm, width), lambda b, i: (b, i, 0))

    def full(a):
        return pl.BlockSpec(a.shape, lambda b, i, nd=a.ndim: (0,) * nd)

    vec = pl.BlockSpec((1, 1, d), lambda b, i: (b, 0, 0))
    tab1 = pl.BlockSpec((PEER_HEADS, PEER_NKEYS, tm), lambda b, i: (0, 0, b * nt + i))
    tab2 = pl.BlockSpec((PEER_HEADS, PEER_NKEYS // 16, 16, tm), lambda b, i: (0, 0, 0, b * nt + i))
    tab1_shape = jax.ShapeDtypeStruct((PEER_HEADS, PEER_NKEYS, t_all), F32)
    tab2_shape = jax.ShapeDtypeStruct((PEER_HEADS, PEER_NKEYS // 16, 16, t_all), BF16)
    return pl.pallas_call(
        _mix_kernel,
        grid=(bz, nt),
        in_specs=[tok(d)] + [tok(ML_WIDTH)] * 6 + [full(mlg), full(glag), full(wout), vec, vec, vec,
                                                    full(n2g), full(wq), full(keys)],
        out_specs=[tok(d), pl.BlockSpec((d, tm), lambda b, i: (0, b * nt + i)), tab1, tab1, tab2, tab2],
        out_shape=[jax.ShapeDtypeStruct((bz, n, d), F32), jax.ShapeDtypeStruct((d, t_all), BF16),
                   tab1_shape, tab1_shape, tab2_shape, tab2_shape],
        scratch_shapes=[pltpu.VMEM((2 * PEER_HEADS, tm, PEER_HALF), BF16),
                        pltpu.VMEM((2, PEER_NKEYS, tm), F32)],
        compiler_params=_params("parallel", "arbitrary"),
        name="mix",
    )(x, mhf, mhb, ghf, ghb, mo, gr, mlg, glag, wout, g1, sh2, sc2, n2g, wq, keys)


def _peer_kernel(h2t_ref, u_ref, vt_ref, e1_ref, l_ref, e2_ref, rk_ref, x1_ref, g2_ref, nfg_ref,
                 out_ref, acc_ref):
    j = pl.program_id(2)
    te, tm = u_ref.shape[0], h2t_ref.shape[1]
    rows_per_step = te // PEER_NKEYS
    tiles = PEER_NKEYS // 16

    @pl.when(j == 0)
    def _():
        acc_ref[...] = jnp.zeros_like(acc_ref)

    act = _dot(u_ref[...], h2t_ref[...])
    act = act.astype(BF16)
    act = (0.5 * act * (1.0 + lax.erf(act * 2.0 ** -0.5))).reshape(rows_per_step, tiles, 16, tm)
    parts = []
    for i in range(rows_per_step):
        w = None
        for p in range(PEER_HEADS):
            e1 = jnp.broadcast_to(e1_ref[p, i:i + 1, :], (16, tm)).astype(BF16)[None]
            lim = jnp.broadcast_to(l_ref[p, i:i + 1, :], (16, tm)).astype(BF16)[None]
            term = jnp.where(rk_ref[p] < lim, e2_ref[p] * e1, jnp.zeros((), BF16))
            w = term if w is None else w + term
        parts.append(w * act[i])
    gated = jnp.concatenate(parts, axis=0).reshape(te, tm)
    acc_ref[...] += _dot(vt_ref[...], gated)

    @pl.when(j == pl.num_programs(2) - 1)
    def _():
        xf = x1_ref[0] + g2_ref[0] * acc_ref[...].T
        out_ref[0] = _rms(xf) * nfg_ref[...]


def _peer(h2t, u, vt, e1, lim, e2, rk, x1, g2, nfg):
    bz, n, d = x1.shape
    n_exp = u.shape[0]
    tm = min(PEER_TM, n)
    te = PEER_TE
    assert n % tm == 0 and n_exp % te == 0 and te % PEER_NKEYS == 0
    nt = n // tm
    tab1 = pl.BlockSpec((PEER_HEADS, te // PEER_NKEYS, tm), lambda b, i, j: (0, j, b * nt + i))
    tab2 = pl.BlockSpec((PEER_HEADS, PEER_NKEYS // 16, 16, tm), lambda b, i, j: (0, 0, 0, b * nt + i))
    tok = pl.BlockSpec((1, tm, d), lambda b, i, j: (b, i, 0))
    return pl.pallas_call(
        _peer_kernel,
        grid=(bz, nt, n_exp // te),
        in_specs=[pl.BlockSpec((d, tm), lambda b, i, j: (0, b * nt + i)),
                  pl.BlockSpec((te, d), lambda b, i, j: (j, 0)),
                  pl.BlockSpec((d, te), lambda b, i, j: (0, j)),
                  tab1, tab1, tab2, tab2, tok,
                  pl.BlockSpec((1, 1, d), lambda b, i, j: (b, 0, 0)),
                  pl.BlockSpec((1, d), lambda b, i, j: (0, 0))],
        out_specs=tok,
        out_shape=jax.ShapeDtypeStruct((bz, n, d), F32),
        scratch_shapes=[pltpu.VMEM((d, tm), F32)],
        compiler_params=_params("parallel", "parallel", "arbitrary"),
        name="peer",
    )(h2t, u, vt, e1, lim, e2, rk, x1, g2, nfg)


def _inproj_weights(w_in, conv_w, gate_b, lr_w2, alpha_b):
    widths = (ML_WIDTH, ML_WIDTH, ML_WIDTH, ML_WIDTH, N_GATES,
              GLA_KEY_WIDTH, GLA_KEY_WIDTH, GLA_WIDTH, GLA_WIDTH, 2 * GLA_RANK)
    offs = [0]
    for w in widths:
        offs.append(offs[-1] + w)
    wb = w_in.astype(BF16)
    col = lambda a, b: wb[:, offs[a]:offs[b]]
    pad_lanes = lambda a: jnp.pad(a, ((0, 0), (0, LANES - a.shape[1])))
    w2 = jnp.zeros((LANES, 2 * GLA_KEY_WIDTH), F32)
    w2 = w2.at[:GLA_RANK, :GLA_KEY_WIDTH].set(lr_w2[0])
    w2 = w2.at[GLA_RANK:2 * GLA_RANK, GLA_KEY_WIDTH:].set(lr_w2[1])
    return (col(0, 2), col(2, 3), col(3, 4), pad_lanes(col(4, 5)), col(4, 5).T,
            col(5, 7), col(7, 8), col(8, 9), pad_lanes(col(9, 10)),
            conv_w, pad_lanes(gate_b[None, :]), gate_b[:, None],
            w2.astype(BF16), alpha_b.reshape(1, 2 * GLA_KEY_WIDTH))


def _per_head_gates(gtok, gmaj):
    bz, n, _ = gtok.shape
    gt = gtok.reshape(bz, n, 2, 2, ML_HEADS).transpose(0, 4, 1, 2, 3).reshape(bz, ML_HEADS, n, 4)
    gm = gmaj.reshape(bz, 2, 2, ML_HEADS, n // ML_CHUNK, ML_CHUNK)
    gm = gm.transpose(0, 3, 4, 1, 2, 5).reshape(bz, ML_HEADS, n // ML_CHUNK, 4, ML_CHUNK)
    return gt, gm


def _token_mix_inputs(x, sh, sc, ng, wts, row_len):
    mq, mk, mv, mo, gtok, gmaj, gq, gk, gv, gr, bf, bb, mkt, gvt = _inproj(x, sh, sc, ng, wts, row_len)
    gt, gm = _per_head_gates(gtok, gmaj)
    return (mq, mk, mkt, mv, gt, gm), (gq, gk, gv, gvt, bf, bb), mo, gr


def kernel(x, c, ctx, c_ctx, w_mod, b_mod, norm1_g, w_in, ml_conv_w, ml_gate_b, ml_norm_g,
           gla_lr_w2, gla_alpha_b, gla_norm_g, w_out, norm2_g, peer_wq, peer_keys, peer_u,
           peer_v, norm_f_g):
    assert w_mod.shape[0] == 1, "single trunk layer"
    bz, n, d = x.shape
    c_all = jnp.concatenate([c, c_ctx[None, :]], axis=0)
    c_all = jnp.pad(c_all, ((0, (-c_all.shape[0]) % SUBLANES), (0, 0)))
    mod = _modulation(c_all, w_mod[0], b_mod[0][None, :])
    sh1, sc1, g1, sh2, sc2, g2 = [m[:, None, :] for m in jnp.split(mod[:bz], 6, axis=1)]
    mod_c = mod[bz]
    csh1, csc1 = mod_c[None, None, :d], mod_c[None, None, d:2 * d]

    ng1 = norm1_g[0][None, :]
    wts = _inproj_weights(w_in[0], ml_conv_w[0], ml_gate_b[0], gla_lr_w2[0], gla_alpha_b[0])
    ml_lat, gla_lat, mo, gr = _token_mix_inputs(x, sh1, sc1, ng1, wts, GRID_W)
    ml_ctx, gla_ctx, _, _ = _token_mix_inputs(ctx, csh1, csc1, ng1, wts, ctx.shape[1])
    mhf, mhb = _mlstm(*ml_lat, *ml_ctx)
    ghf, ghb = _gla(*gla_lat, *gla_ctx)

    keys = peer_keys[0].reshape(2 * PEER_HEADS, PEER_NKEYS, PEER_HALF).astype(BF16)
    x1, h2t, e1, lim, e2, rk = _mix(
        x, mhf, mhb, ghf, ghb, mo, gr, ml_norm_g[0][None, :], gla_norm_g[0][None, :],
        w_out[0].astype(BF16), g1, sh2, sc2, norm2_g[0][None, :], peer_wq[0].astype(BF16), keys)
    return _peer(h2t, peer_u[0].astype(BF16), peer_v[0].astype(BF16).T, e1, lim, e2, rk,
                 x1, g2, norm_f_g[None, :])
```

```python
import functools

import jax
import jax.numpy as jnp
from jax import lax
from jax.experimental import pallas as pl
from jax.experimental.pallas import tpu as pltpu

F32 = jnp.float32
BF16 = jnp.bfloat16

EPS = 1e-6
D_MODEL = 1024
GRID_W = 64

ML_HEADS = 4
ML_HEAD_DIM = 128
ML_WIDTH = ML_HEADS * ML_HEAD_DIM
ML_CHUNK = 128
GLA_HEADS = 4
GLA_DK = 64
GLA_DV = 128
GLA_KEY_WIDTH = GLA_HEADS * GLA_DK
GLA_WIDTH = GLA_HEADS * GLA_DV
GLA_RANK = 16
GLA_TAU = 16.0
GLA_CHUNK = 64
N_GATES = 4 * ML_HEADS

PEER_HEADS = 8
PEER_NKEYS = 128
PEER_TOPK = 16
PEER_HALF = 128

LANES = 128
SUBLANES = 8
VMEM_LIMIT = 56 * 1024 * 1024

INPROJ_TM = 512
CUMSUM_ROWS = 256
ML_HEADS_PER_STEP = 4
GLA_HEADS_PER_STEP = 2
MIX_TM = 512
PEER_TM = 512
PEER_TE = 2048

NEG_INF = float("-inf")


def _params(*sem):
    return pltpu.CompilerParams(dimension_semantics=sem, vmem_limit_bytes=VMEM_LIMIT)


def _dot(a, b):
    return jnp.dot(a, b, preferred_element_type=F32)


def _dot_nt(a, b):
    return lax.dot_general(a, b, (((1,), (1,)), ((), ())), preferred_element_type=F32)


def _split2(x):
    hi = x.astype(BF16)
    return hi, (x - hi.astype(F32)).astype(BF16)


def _dot_exact_rhs(a01, x):
    hi, lo = _split2(x)
    return _dot(a01, hi) + _dot(a01, lo)


def _dot_exact_lhs(x, a01):
    hi, lo = _split2(x)
    return _dot(hi, a01) + _dot(lo, a01)


def _block_rows(mask, x):
    r = mask.shape[0]
    return jnp.concatenate([_dot_exact_rhs(mask, x[i:i + r]) for i in range(0, x.shape[0], r)], axis=0)


def _block_cols(x, mask):
    r = mask.shape[0]
    return jnp.concatenate([_dot_exact_lhs(x[:, i:i + r], mask) for i in range(0, x.shape[1], r)], axis=1)


def _sigmoid(x):
    return 1.0 / (1.0 + jnp.exp(-x))


def _log_sigmoid(x):
    return jnp.minimum(x, 0.0) - jnp.log(1.0 + jnp.exp(-jnp.abs(x)))


def _rms(x):
    return x * lax.rsqrt(jnp.mean(x * x, axis=-1, keepdims=True) + EPS)


def _mod_kernel(c_ref, w_ref, b_ref, o_ref):
    cond = c_ref[...]
    cond = cond * _sigmoid(cond)
    ch, cl = cond.astype(BF16), (cond - cond.astype(BF16).astype(F32)).astype(BF16)
    w = w_ref[...]
    wh = w.astype(BF16)
    wl = (w - wh.astype(F32)).astype(BF16)
    o_ref[...] = _dot(ch, wh) + _dot(ch, wl) + _dot(cl, wh) + b_ref[...]


def _modulation(c_all, w_mod, b_mod):
    rows, d = c_all.shape
    n_out = w_mod.shape[1]
    tn = 512
    return pl.pallas_call(
        _mod_kernel,
        grid=(n_out // tn,),
        in_specs=[pl.BlockSpec((rows, d), lambda j: (0, 0)),
                  pl.BlockSpec((d, tn), lambda j: (0, j)),
                  pl.BlockSpec((1, tn), lambda j: (0, j))],
        out_specs=pl.BlockSpec((rows, tn), lambda j: (0, j)),
        out_shape=jax.ShapeDtypeStruct((rows, n_out), F32),
        compiler_params=_params("arbitrary"),
        name="mod",
    )(c_all, w_mod, b_mod)


def _chunk_masks(tm, chunk):
    r = lax.broadcasted_iota(jnp.int32, (tm, tm), 0)
    c = lax.broadcasted_iota(jnp.int32, (tm, tm), 1)
    same = (r // chunk) == (c // chunk)
    prefix = jnp.where(same & (c <= r), 1.0, 0.0).astype(BF16)
    suffix = jnp.where(same & (c >= r), 1.0, 0.0).astype(BF16)
    return prefix, suffix


def _inproj_kernel(x_ref, sh_ref, sc_ref, ng_ref, wqk_ref, wv_ref, wo_ref, wg_ref, wgt_ref,
                   wgqk_ref, wgv_ref, wgr_ref, wlr_ref, conv_ref, gb_ref, gbc_ref, w2_ref, ab_ref,
                   mq_ref, mk_ref, mv_ref, mo_ref, gtok_ref, gmaj_ref,
                   gq_ref, gk_ref, gv_ref, gr_ref, bf_ref, bb_ref, mkt_ref, gvt_ref, *, row_len):
    tm = x_ref.shape[1]
    x = x_ref[0]
    h = _rms(x) * ng_ref[...] * (1.0 + sc_ref[0]) + sh_ref[0]
    hb = h.astype(BF16)

    u = _dot(hb, wqk_ref[...])
    pos = lax.broadcasted_iota(jnp.int32, (tm, 1), 0) % row_len
    up = jnp.where(pos == 0, 0.0, pltpu.roll(u, 1, 0))
    dn = jnp.where(pos == row_len - 1, 0.0, pltpu.roll(u, tm - 1, 0))
    cw = conv_ref[...]
    y = cw[0:1] * up + cw[1:2] * u + cw[2:3] * dn
    y = y * _sigmoid(y)
    mq_ref[0] = y[:, :ML_WIDTH].astype(BF16)
    mk = y[:, ML_WIDTH:] * ML_HEAD_DIM ** -0.5
    mk_ref[0] = mk.astype(BF16)
    mkt = mk.T.astype(BF16)
    for hh in range(ML_HEADS):
        for ci in range(tm // ML_CHUNK):
            mkt_ref[0, hh, ci] = mkt[hh * ML_HEAD_DIM:(hh + 1) * ML_HEAD_DIM, ci * ML_CHUNK:(ci + 1) * ML_CHUNK]
    mv_ref[0] = _dot(hb, wv_ref[...]).astype(BF16)
    mo_ref[0] = _dot(hb, wo_ref[...]).astype(BF16)

    pre128, suf128 = _chunk_masks(min(tm, CUMSUM_ROWS), ML_CHUNK)
    g = _dot(hb, wg_ref[...]) + gb_ref[...]
    col = lax.broadcasted_iota(jnp.int32, (1, LANES), 1)
    is_f = ((col % 8) >= 4) & (col < N_GATES)
    is_bwd = col >= 8
    lf = jnp.where(is_f, _log_sigmoid(g), 0.0)
    cum = jnp.where(is_bwd, _block_rows(suf128, lf), _block_rows(pre128, lf))
    gtok_ref[0] = jnp.where(is_f, cum, g)[:, :N_GATES]

    gt = _dot_nt(wgt_ref[...], hb) + gbc_ref[...]
    row = lax.broadcasted_iota(jnp.int32, (N_GATES, 1), 0)
    is_f_r = (row % 8) >= 4
    lft = jnp.where(is_f_r, _log_sigmoid(gt), 0.0)
    cum_t = jnp.where(row >= 8, _block_cols(lft, pre128), _block_cols(lft, suf128))
    gmaj_ref[0] = jnp.where(is_f_r, cum_t, gt)

    gqk = _dot(hb, wgqk_ref[...])
    def per_head(ref, a):
        for hh in range(GLA_HEADS):
            ref[0, hh] = a[:, hh * GLA_DK:(hh + 1) * GLA_DK]

    per_head(gq_ref, (gqk[:, :GLA_KEY_WIDTH] * GLA_DK ** -0.5).astype(BF16))
    per_head(gk_ref, gqk[:, GLA_KEY_WIDTH:].astype(BF16))
    gv = _dot(hb, wgv_ref[...])
    gv_ref[0] = gv.astype(BF16)
    gvt = gv.T.astype(BF16)
    for hh in range(GLA_HEADS):
        for ci in range(tm // GLA_CHUNK):
            gvt_ref[0, hh, ci] = gvt[hh * GLA_DV:(hh + 1) * GLA_DV, ci * GLA_CHUNK:(ci + 1) * GLA_CHUNK]
    gr_ref[0] = _dot(hb, wgr_ref[...]).astype(BF16)
    glr = _dot(hb, wlr_ref[...]).astype(BF16)
    alpha = _dot(glr, w2_ref[...]) + ab_ref[...]
    la = _log_sigmoid(alpha) * (1.0 / GLA_TAU)
    pre64, suf64 = _chunk_masks(min(tm, CUMSUM_ROWS), GLA_CHUNK)
    per_head(bf_ref, _block_rows(pre64, la[:, :GLA_KEY_WIDTH]))
    per_head(bb_ref, _block_rows(suf64, la[:, GLA_KEY_WIDTH:]))


def _inproj(x, sh, sc, ng, wts, row_len):
    bz, n, d = x.shape
    tm = min(INPROJ_TM, n)
    assert n % tm == 0 and tm % row_len == 0 and tm % ML_CHUNK == 0
    grid = (bz, n // tm)
    per_b = (lambda b, i: (b, 0, 0)) if sh.shape[0] == bz else (lambda b, i: (0, 0, 0))

    def full(a):
        return pl.BlockSpec(a.shape, lambda b, i, nd=a.ndim: (0,) * nd)

    def tok(width):
        return pl.BlockSpec((1, tm, width), lambda b, i: (b, i, 0))

    def out(width, dtype):
        return jax.ShapeDtypeStruct((bz, n, width), dtype)

    in_specs = [tok(d), pl.BlockSpec((1, 1, d), per_b), pl.BlockSpec((1, 1, d), per_b), full(ng)]
    in_specs += [full(w) for w in wts]
    out_specs = [tok(ML_WIDTH)] * 4 + [tok(N_GATES), pl.BlockSpec((1, N_GATES, tm), lambda b, i: (b, 0, i))]
    key = pl.BlockSpec((1, GLA_HEADS, tm, GLA_DK), lambda b, i: (b, 0, i, 0))
    out_specs += [key] * 2 + [tok(GLA_WIDTH)] * 2 + [key] * 2
    out_shape = [out(ML_WIDTH, BF16)] * 4 + [out(N_GATES, F32),
                                              jax.ShapeDtypeStruct((bz, N_GATES, n), F32)]
    keys = lambda dtype: jax.ShapeDtypeStruct((bz, GLA_HEADS, n, GLA_DK), dtype)
    out_shape += [keys(BF16)] * 2 + [out(GLA_WIDTH, BF16)] * 2 + [keys(F32)] * 2
    for heads, dim, chunk in ((ML_HEADS, ML_HEAD_DIM, ML_CHUNK), (GLA_HEADS, GLA_DV, GLA_CHUNK)):
        out_specs.append(pl.BlockSpec((1, heads, tm // chunk, dim, chunk), lambda b, i: (b, 0, i, 0, 0)))
        out_shape.append(jax.ShapeDtypeStruct((bz, heads, n // chunk, dim, chunk), BF16))
    return pl.pallas_call(
        functools.partial(_inproj_kernel, row_len=row_len),
        grid=grid, in_specs=in_specs, out_specs=out_specs, out_shape=out_shape,
        compiler_params=_params("parallel", "arbitrary"),
        name="inproj",
    )(x, sh, sc, ng, *wts)


def _bmm(a, b):
    return lax.dot_general(a, b, (((2,), (1,)), ((0,), (0,))), preferred_element_type=F32)


def _bmm_nt(a, b):
    return lax.dot_general(a, b, (((2,), (2,)), ((0,), (0,))), preferred_element_type=F32)


def _by_direction(x, fwd, bwd):
    half = x.shape[0] // 2
    return jnp.concatenate([fwd(x[:half]), bwd(x[half:])], axis=0)


def _ml_chunks(q, k, kt, v, gt, gm, masks, state, want_out):
    ct, nv, m = state
    ll = q.shape[1]
    ig_col = _by_direction(gt, lambda g: g[:, :, 0:1], lambda g: g[:, :, 2:3])
    b_col = _by_direction(gt, lambda g: g[:, :, 1:2], lambda g: g[:, :, 3:4])
    ig_row = _by_direction(gm, lambda g: g[:, 0:1, :], lambda g: g[:, 2:3, :])
    b_row = _by_direction(gm, lambda g: g[:, 1:2, :], lambda g: g[:, 3:4, :])
    b_end = _by_direction(b_row, lambda b: b[:, :, ll - 1:ll], lambda b: b[:, :, 0:1])
    h = None
    if want_out:
        d_log = _by_direction(b_col - b_row + ig_row,
                              lambda x: jnp.where(masks[0], x, NEG_INF),
                              lambda x: jnp.where(masks[1], x, NEG_INF))
        inter_log = b_col + m
        m_t = jnp.maximum(inter_log, jnp.max(d_log, axis=2, keepdims=True))
        scores = _bmm_nt(q, k) * jnp.exp(d_log - m_t)
        inter = jnp.exp(inter_log - m_t)
        qf = q.astype(F32)
        num = _bmm(scores.astype(BF16), v) + inter * _bmm(q, ct.astype(BF16))
        den = jnp.sum(scores, axis=2, keepdims=True) + inter * jnp.sum(qf * nv, axis=2, keepdims=True)
        h = num / jnp.maximum(jnp.abs(den), jnp.exp(-m_t))
    m_loc = jnp.max(b_end - b_row + ig_row, axis=2, keepdims=True)
    w_col = jnp.exp(b_end - b_col + ig_col - m_loc)
    c_loc = _bmm(kt, (w_col * v.astype(F32)).astype(BF16))
    n_loc = jnp.sum(w_col * k.astype(F32), axis=1, keepdims=True)
    m_new = jnp.maximum(b_end + m, m_loc)
    a = jnp.exp(b_end + m - m_new)
    bb = jnp.exp(m_loc - m_new)
    return h, (a * ct + bb * c_loc, a * nv + bb * n_loc, m_new)


def _mlstm_kernel(q_ref, k_ref, kt_ref, v_ref, gt_ref, gm_ref,
                  qc_ref, kc_ref, ktc_ref, vc_ref, gtc_ref, gmc_ref, hf_ref, hb_ref):
    ll = ML_CHUNK
    nc = q_ref.shape[1] // ll
    ncc = qc_ref.shape[1] // ll
    r = lax.broadcasted_iota(jnp.int32, (ll, ll), 0)
    c = lax.broadcasted_iota(jnp.int32, (ll, ll), 1)
    masks = (c <= r, c >= r)

    heads = gt_ref.shape[1]
    head_cols = [slice(hh * ML_HEAD_DIM, (hh + 1) * ML_HEAD_DIM) for hh in range(heads)]

    def rows_of(ci):
        return pl.ds(pl.multiple_of(ci * ll, ll), ll)

    def load(refs, cf, cb):
        qr, kr, ktr, vr, gtr, gmr = refs
        sites = [(hh, ci) for ci in (cf, cb) for hh in range(heads)]
        seq = lambda r: jnp.stack([r[0, rows_of(ci), head_cols[hh]] for hh, ci in sites])
        return (seq(qr), seq(kr), jnp.stack([ktr[0, hh, ci] for hh, ci in sites]), seq(vr),
                jnp.stack([gtr[0, hh, rows_of(ci), :] for hh, ci in sites]),
                jnp.stack([gmr[0, hh, ci] for hh, ci in sites]))

    chains = 2 * heads
    zero = (jnp.zeros((chains, ML_HEAD_DIM, ML_HEAD_DIM), F32), jnp.zeros((chains, 1, ML_HEAD_DIM), F32),
            jnp.zeros((chains, 1, 1), F32))
    ctx_refs = (qc_ref, kc_ref, ktc_ref, vc_ref, gtc_ref, gmc_ref)
    lat_refs = (q_ref, k_ref, kt_ref, v_ref, gt_ref, gm_ref)

    def ctx_body(i, state):
        _, state = _ml_chunks(*load(ctx_refs, i, ncc - 1 - i), masks, state, False)
        return state

    def lat_body(i, state):
        cb = nc - 1 - i
        h, state = _ml_chunks(*load(lat_refs, i, cb), masks, state, True)
        for hh in range(heads):
            hf_ref[0, rows_of(i), head_cols[hh]] = h[hh]
            hb_ref[0, rows_of(cb), head_cols[hh]] = h[heads + hh]
        return state

    state = lax.fori_loop(0, ncc, ctx_body, zero)
    lax.fori_loop(0, nc, lat_body, state)


def _mlstm(q, k, kt, v, gt, gm, qc, kc, ktc, vc, gtc, gmc):
    bz, n, _ = q.shape
    ncx = qc.shape[1]
    hps = ML_HEADS_PER_STEP

    def seq(nn):
        return pl.BlockSpec((1, nn, hps * ML_HEAD_DIM), lambda b, h: (b, 0, h))

    def ktr(nn):
        return pl.BlockSpec((1, hps, nn // ML_CHUNK, ML_HEAD_DIM, ML_CHUNK), lambda b, h: (b, h, 0, 0, 0))

    def gtok(nn):
        return pl.BlockSpec((1, hps, nn, 4), lambda b, h: (b, h, 0, 0))

    def gmaj(nn):
        return pl.BlockSpec((1, hps, nn // ML_CHUNK, 4, ML_CHUNK), lambda b, h: (b, h, 0, 0, 0))

    def specs(nn):
        return [seq(nn), seq(nn), ktr(nn), seq(nn), gtok(nn), gmaj(nn)]

    return pl.pallas_call(
        _mlstm_kernel,
        grid=(bz, ML_HEADS // hps),
        in_specs=specs(n) + specs(ncx),
        out_specs=[seq(n)] * 2,
        out_shape=[jax.ShapeDtypeStruct((bz, n, ML_WIDTH), F32)] * 2,
        compiler_params=_params("parallel", "arbitrary"),
        name="mlstm",
    )(q, k, kt, v, gt, gm, qc, kc, ktc, vc, gtc, gmc)


def _gla_chunks(q, k, v, vt, b, masks, st, want_out):
    ll = q.shape[1]
    qf, kf = q.astype(F32), k.astype(F32)
    ref = _by_direction(b, lambda x: x[:, ll // 2:ll // 2 + 1], lambda x: x[:, ll // 2 - 1:ll // 2])
    b_end = _by_direction(b, lambda x: x[:, ll - 1:ll], lambda x: x[:, 0:1])
    o = None
    if want_out:
        att = _bmm_nt((qf * jnp.exp(b - ref)).astype(BF16), (kf * jnp.exp(ref - b)).astype(BF16))
        att = _by_direction(att, lambda x: jnp.where(masks[0], x, 0.0), lambda x: jnp.where(masks[1], x, 0.0))
        o = _bmm(att.astype(BF16), v) + _bmm_nt((qf * jnp.exp(b)).astype(BF16), st.astype(BF16))
    s_loc = _bmm(vt, (kf * jnp.exp(b_end - b)).astype(BF16))
    return o, jnp.exp(b_end) * st + s_loc


def _gla_kernel(q_ref, k_ref, v_ref, vt_ref, bf_ref, bb_ref,
                qc_ref, kc_ref, vc_ref, vtc_ref, bfc_ref, bbc_ref, of_ref, ob_ref):
    ll = GLA_CHUNK
    nc = q_ref.shape[2] // ll
    ncc = qc_ref.shape[2] // ll
    r = lax.broadcasted_iota(jnp.int32, (ll, ll), 0)
    c = lax.broadcasted_iota(jnp.int32, (ll, ll), 1)
    masks = (c <= r, c >= r)

    heads = q_ref.shape[1]

    def rows_of(ci):
        return pl.ds(pl.multiple_of(ci * ll, ll), ll)

    head_cols = [slice(hh * GLA_DV, (hh + 1) * GLA_DV) for hh in range(heads)]

    def load(refs, cf, cb):
        qr, kr, vr, vtr, bfr, bbr = refs
        sites = [(hh, ci) for ci in (cf, cb) for hh in range(heads)]
        per_head = lambda r: jnp.stack([r[0, hh, rows_of(ci), :] for hh, ci in sites])
        return (per_head(qr), per_head(kr),
                jnp.stack([vr[0, rows_of(ci), head_cols[hh]] for hh, ci in sites]),
                jnp.stack([vtr[0, hh, ci] for hh, ci in sites]),
                jnp.stack([bfr[0, hh, rows_of(cf), :] for hh in range(heads)]
                          + [bbr[0, hh, rows_of(cb), :] for hh in range(heads)]))

    zero = jnp.zeros((2 * heads, GLA_DV, GLA_DK), F32)
    ctx_refs = (qc_ref, kc_ref, vc_ref, vtc_ref, bfc_ref, bbc_ref)
    lat_refs = (q_ref, k_ref, v_ref, vt_ref, bf_ref, bb_ref)

    def ctx_body(i, st):
        _, st = _gla_chunks(*load(ctx_refs, i, ncc - 1 - i), masks, st, False)
        return st

    def lat_body(i, st):
        cb = nc - 1 - i
        o, st = _gla_chunks(*load(lat_refs, i, cb), masks, st, True)
        for hh in range(heads):
            of_ref[0, rows_of(i), head_cols[hh]] = o[hh]
            ob_ref[0, rows_of(cb), head_cols[hh]] = o[heads + hh]
        return st

    st = lax.fori_loop(0, ncc, ctx_body, zero)
    lax.fori_loop(0, nc, lat_body, st)


def _gla(q, k, v, vt, bf, bb, qc, kc, vc, vtc, bfc, bbc):
    bz, _, n, _ = q.shape
    ncx = qc.shape[2]
    hps = GLA_HEADS_PER_STEP

    def key(nn):
        return pl.BlockSpec((1, hps, nn, GLA_DK), lambda b, h: (b, h, 0, 0))

    def val(nn):
        return pl.BlockSpec((1, nn, hps * GLA_DV), lambda b, h: (b, 0, h))

    def valt(nn):
        return pl.BlockSpec((1, hps, nn // GLA_CHUNK, GLA_DV, GLA_CHUNK), lambda b, h: (b, h, 0, 0, 0))

    def specs(nn):
        return [key(nn), key(nn), val(nn), valt(nn), key(nn), key(nn)]

    return pl.pallas_call(
        _gla_kernel,
        grid=(bz, GLA_HEADS // hps),
        in_specs=specs(n) + specs(ncx),
        out_specs=[val(n)] * 2,
        out_shape=[jax.ShapeDtypeStruct((bz, n, GLA_WIDTH), F32)] * 2,
        compiler_params=_params("parallel", "arbitrary"),
        name="gla",
    )(q, k, v, vt, bf, bb, qc, kc, vc, vtc, bfc, bbc)


def _head_rms(h, g, width):
    parts = [_rms(h[:, i:i + width]) for i in range(0, h.shape[1], width)]
    return jnp.concatenate(parts, axis=1) * g


def _oddeven_merge(lo, hi, r):
    step = r * 2
    if step < hi - lo:
        yield from _oddeven_merge(lo, hi, step)
        yield from _oddeven_merge(lo + r, hi, step)
        yield from [(i, i + r) for i in range(lo + r, hi - r, step)]
    else:
        yield (lo, lo + r)


def _oddeven_merge_sort(lo, hi):
    if hi - lo >= 1:
        mid = lo + (hi - lo) // 2
        yield from _oddeven_merge_sort(lo, mid)
        yield from _oddeven_merge_sort(mid + 1, hi)
        yield from _oddeven_merge(lo, hi, 1)


def _exchange(v, i, j):
    v[i], v[j] = jnp.maximum(v[i], v[j]), jnp.minimum(v[i], v[j])


def _merge_top16(v, shift):
    other = [pltpu.roll(x, shift, 0) for x in v]
    out = []
    for j in range(PEER_TOPK):
        a = v[j] if j < len(v) else None
        b = other[PEER_TOPK - 1 - j] if PEER_TOPK - 1 - j < len(v) else None
        out.append(a if b is None else b if a is None else jnp.maximum(a, b))
    return out


def _sort_bitonic16(v):
    v = list(v)
    for d in (8, 4, 2, 1):
        for j in range(PEER_TOPK):
            if not j & d:
                _exchange(v, j, j + d)
    return v


def _sixteenth_largest(blocks):
    v = list(blocks)
    assert SUBLANES <= len(v) <= PEER_TOPK
    for i, j in _oddeven_merge_sort(0, PEER_TOPK - 1):
        if j < len(v):
            _exchange(v, i, j)
    v = _sort_bitonic16(_merge_top16(v, 4))
    v = _sort_bitonic16(_merge_top16(v, 2))
    v = _merge_top16(v, 1)
    return functools.reduce(jnp.minimum, v)[0:1]


def _sorted_top16(tiles):
    v = list(tiles)
    n = len(v)
    for i, j in _oddeven_merge_sort(0, n - 1):
        _exchange(v, i, j)
    for shift in (4, 2, 1):
        v = _sort_bitonic16(_merge_top16(v, shift))
    sub = lax.broadcasted_iota(jnp.int32, v[0].shape, 0)
    halves = []
    for lo in (0, SUBLANES):
        rows = v[lo]
        for r in range(1, SUBLANES):
            rows = jnp.where(sub == r, v[lo + r], rows)
        halves.append(rows)
    repeated = jnp.zeros_like(v[0])
    for r in range(n - 1):
        repeated = jnp.where(v[r] == v[r + 1], 1.0, repeated)
    return jnp.concatenate(halves, axis=0), repeated[0:1]


def _prefix_count(x, thr, test):
    r = [thr[i:i + 1] for i in range(PEER_TOPK)]
    sel = jnp.where
    t8 = test(x, r[7])
    t4 = test(x, sel(t8, r[11], r[3]))
    t2 = test(x, sel(t8, sel(t4, r[13], r[9]), sel(t4, r[5], r[1])))
    t1 = test(x, sel(t8, sel(t4, sel(t2, r[14], r[12]), sel(t2, r[10], r[8])),
                     sel(t4, sel(t2, r[6], r[4]), sel(t2, r[2], r[0]))))
    count = sel(t8, 8.0, 0.0) + sel(t4, 4.0, 0.0) + sel(t2, 2.0, 0.0) + sel(t1, 1.0, 0.0)
    return sel(test(x, r[15]), float(PEER_TOPK), count)


def _count(mask):
    return jnp.sum(jnp.where(mask, 1.0, 0.0), axis=0, keepdims=True)


def _top16(s):
    nk, tl = s.shape
    slot = lax.broadcasted_iota(jnp.int32, (PEER_TOPK, tl), 0)
    key_id = lax.broadcasted_iota(jnp.int32, (nk, tl), 0).astype(F32)

    def body(r, carry):
        s, rank, vals = carry
        m = jnp.max(s, axis=0, keepdims=True)
        first = jnp.min(jnp.where(s == m, key_id, float(nk)), axis=0, keepdims=True)
        sel = key_id == first
        rank = jnp.where(sel, jnp.asarray(r).astype(F32), rank)
        s = jnp.where(sel, NEG_INF, s)
        vals = jnp.where(slot == r, m, vals)
        return s, rank, vals

    init = (s, jnp.full((nk, tl), float(PEER_TOPK), F32), jnp.zeros((PEER_TOPK, tl), F32))
    _, rank, vals = lax.fori_loop(0, PEER_TOPK, body, init)
    return vals, rank


def _pair_candidates(v1, v2):
    blocks = [v1[0:1] + v2[0:8], v1[0:1] + v2[8:16]]
    blocks += [v1[a:a + 1] + v2[0:8] for a in range(1, 8)]
    blocks += [v1[8:16] + v2[0:1]]
    return jnp.concatenate(blocks, axis=0)


def _blocks(cand):
    return [cand[i:i + SUBLANES] for i in range(0, cand.shape[0], SUBLANES)]


def _pick16(cand):
    i = lax.broadcasted_iota(jnp.int32, cand.shape, 0)
    blk, rr = i // 8, i % 8
    flat = jnp.where(blk == 0, rr, jnp.where(blk == 1, 8 + rr,
                     jnp.where(blk <= 8, (blk - 1) * 16 + rr, (8 + rr) * 16))).astype(F32)

    def body(_, carry):
        cand, picked = carry
        m = jnp.max(cand, axis=0, keepdims=True)
        first = jnp.min(jnp.where(cand == m, flat, 1e9), axis=0, keepdims=True)
        sel = flat == first
        return jnp.where(sel, NEG_INF, cand), jnp.where(sel, 1.0, picked)

    _, picked = lax.fori_loop(0, PEER_TOPK, body, (cand, jnp.zeros_like(cand)))
    return picked


def _first_key_thresholds(picked, v1):
    inf = jnp.inf
    lo = jnp.where(picked[0:8] > 0.0, v1[0:1], inf)
    for a in range(1, 8):
        lo = jnp.minimum(lo, jnp.where(picked[8 * (a + 1):8 * (a + 2)] > 0.0, v1[a:a + 1], inf))
    tail = jnp.min(jnp.where(picked[72:80] > 0.0, v1[8:16], inf), axis=0, keepdims=True)
    row = lax.broadcasted_iota(jnp.int32, lo.shape, 0)
    lo = jnp.minimum(lo, jnp.where(row == 0, tail, inf))
    hi = jnp.where(picked[8:16] > 0.0, v1[0:1], inf)
    return jnp.concatenate([lo, hi], axis=0)


def _pair_counts(cand, picked):
    z = jnp.sum(picked * jnp.exp(cand - cand[0:1]), axis=0, keepdims=True)
    n_rows = [jnp.sum(picked[0:16], axis=0, keepdims=True)]
    n_rows += [jnp.sum(picked[8 * (a + 1):8 * (a + 2)], axis=0, keepdims=True) for a in range(1, 8)]
    return jnp.concatenate(n_rows + [picked[72:80]], axis=0), z


def _mix_kernel(x_ref, mhf_ref, mhb_ref, ghf_ref, ghb_ref, mo_ref, gr_ref, mlg_ref, glag_ref,
                wout_ref, g1_ref, sh2_ref, sc2_ref, n2g_ref, wq_ref, keys_ref,
                x1_ref, h2t_ref, e1_ref, l_ref, e2_ref, rk_ref, q_scr, s_scr):
    ml = _head_rms(mhf_ref[0] + mhb_ref[0], mlg_ref[...], ML_HEAD_DIM)
    ml = _sigmoid(mo_ref[0].astype(F32)) * ml
    gl = _head_rms(ghf_ref[0] + ghb_ref[0], glag_ref[...], GLA_DV)
    gr = gr_ref[0].astype(F32)
    gl = gr * _sigmoid(gr) * gl
    mix = jnp.concatenate([ml, gl], axis=1).astype(BF16)
    x1 = x_ref[0] + g1_ref[0] * _dot(mix, wout_ref[...])
    x1_ref[0] = x1
    h2 = _rms(x1) * n2g_ref[...] * (1.0 + sc2_ref[0]) + sh2_ref[0]
    h2t_ref[...] = h2.T.astype(BF16)
    qall = _dot(h2.astype(BF16), wq_ref[...]).astype(BF16)
    for j in range(2 * PEER_HEADS):
        q_scr[j] = qall[:, j * PEER_HALF:(j + 1) * PEER_HALF]

    tm = x_ref.shape[1]
    topk = float(PEER_TOPK)
    lane_chunks = [slice(i, i + LANES) for i in range(0, tm, LANES)]

    def write_tables(p, lanes, s1, s2, v1, v2, z, lim, rank2):
        e1_ref[p, :, lanes] = jnp.exp(s1 - v1[0:1]) * (1.0 / z)
        l_ref[p, :, lanes] = lim
        e2 = jnp.exp(s2 - v2[0:1]).astype(BF16)
        e2_ref[p, :, :, lanes] = e2.reshape(PEER_NKEYS // 16, 16, LANES)
        rk_ref[p, :, :, lanes] = rank2.astype(BF16).reshape(PEER_NKEYS // 16, 16, LANES)

    def head_body(p, carry):
        for hf in range(2):
            s_scr[hf] = _dot_nt(keys_ref[2 * p + hf], q_scr[2 * p + hf])
        excess = jnp.zeros((1, LANES), F32)
        vals = []
        for hf in range(2):
            for lanes in lane_chunks:
                tiles = [s_scr[hf, i:i + SUBLANES, lanes] for i in range(0, PEER_NKEYS, SUBLANES)]
                top, repeated = _sorted_top16(tiles)
                vals.append(top)
                excess = jnp.maximum(excess, repeated)
        v1s, v2s = vals[:len(lane_chunks)], vals[len(lane_chunks):]
        cands = [_pair_candidates(v1, v2) for v1, v2 in zip(v1s, v2s)]
        taus = [_sixteenth_largest(_blocks(c)) for c in cands]
        for lanes, v1, v2, cand, tau in zip(lane_chunks, v1s, v2s, cands, taus):
            s1, s2 = s_scr[0, :, lanes], s_scr[1, :, lanes]
            picked = jnp.where(cand >= tau, 1.0, 0.0)
            _, z = _pair_counts(cand, picked)
            theta = _first_key_thresholds(picked, v1)
            lim = _prefix_count(s1, theta, lambda x, t: x >= t)
            rank2 = _prefix_count(s2, v2, lambda x, t: x < t)
            write_tables(p, lanes, s1, s2, v1, v2, z, lim, rank2)
            n_marked = (_count(s1 >= v1[PEER_TOPK - 1:PEER_TOPK]) + _count(rank2 < topk)
                        + jnp.sum(picked, axis=0, keepdims=True))
            excess = jnp.maximum(excess, n_marked - 3.0 * topk)

        @pl.when(jnp.max(excess) > 0.0)
        def _():
            for lanes in lane_chunks:
                s1, s2 = s_scr[0, :, lanes], s_scr[1, :, lanes]
                v1, rank1 = _top16(s1)
                v2, rank2 = _top16(s2)
                cand = _pair_candidates(v1, v2)
                n, z = _pair_counts(cand, _pick16(cand))
                lim = jnp.zeros_like(rank1)
                for a in range(PEER_TOPK):
                    lim = jnp.where(rank1 == float(a), n[a:a + 1], lim)
                write_tables(p, lanes, s1, s2, v1, v2, z, lim, rank2)

        return carry

    lax.fori_loop(0, PEER_HEADS, head_body, 0)


def _mix(x, mhf, mhb, ghf, ghb, mo, gr, mlg, glag, wout, g1, sh2, sc2, n2g, wq, keys):
    bz, n, d = x.shape
    tm = min(MIX_TM, n)
    assert n % tm == 0 and tm % LANES == 0
    nt = n // tm
    t_all = bz * n

    def tok(width):
        return pl.BlockSpec((1, tm, width), lambda b, i: (b, i, 0))

    def full(a):
        return pl.BlockSpec(a.shape, lambda b, i, nd=a.ndim: (0,) * nd)

    vec = pl.BlockSpec((1, 1, d), lambda b, i: (b, 0, 0))
    tab1 = pl.BlockSpec((PEER_HEADS, PEER_NKEYS, tm), lambda b, i: (0, 0, b * nt + i))
    tab2 = pl.BlockSpec((PEER_HEADS, PEER_NKEYS // 16, 16, tm), lambda b, i: (0, 0, 0, b * nt + i))
    tab1_shape = jax.ShapeDtypeStruct((PEER_HEADS, PEER_NKEYS, t_all), F32)
    tab2_shape = jax.ShapeDtypeStruct((PEER_HEADS, PEER_NKEYS // 16, 16, t_all), BF16)
    return pl.pallas_call(
        _mix_kernel,
        grid=(bz, nt),
        in_specs=[tok(d)] + [tok(ML_WIDTH)] * 6 + [full(mlg), full(glag), full(wout), vec, vec, vec,
                                                    full(n2g), full(wq), full(keys)],
        out_specs=[tok(d), pl.BlockSpec((d, tm), lambda b, i: (0, b * nt + i)), tab1, tab1, tab2, tab2],
        out_shape=[jax.ShapeDtypeStruct((bz, n, d), F32), jax.ShapeDtypeStruct((d, t_all), BF16),
                   tab1_shape, tab1_shape, tab2_shape, tab2_shape],
        scratch_shapes=[pltpu.VMEM((2 * PEER_HEADS, tm, PEER_HALF), BF16),
                        pltpu.VMEM((2, PEER_NKEYS, tm), F32)],
        compiler_params=_params("parallel", "arbitrary"),
        name="mix",
    )(x, mhf, mhb, ghf, ghb, mo, gr, mlg, glag, wout, g1, sh2, sc2, n2g, wq, keys)


def _peer_kernel(h2t_ref, u_ref, vt_ref, e1_ref, l_ref, e2_ref, rk_ref, x1_ref, g2_ref, nfg_ref,
                 out_ref, acc_ref):
    j = pl.program_id(2)
    te, tm = u_ref.shape[0], h2t_ref.shape[1]
    rows_per_step = te // PEER_NKEYS
    tiles = PEER_NKEYS // 16

    @pl.when(j == 0)
    def _():
        acc_ref[...] = jnp.zeros_like(acc_ref)

    act = _dot(u_ref[...], h2t_ref[...])
    act = act.astype(BF16)
    act = (0.5 * act * (1.0 + lax.erf(act * 2.0 ** -0.5))).reshape(rows_per_step, tiles, 16, tm)
    parts = []
    for i in range(rows_per_step):
        w = None
        for p in range(PEER_HEADS):
            e1 = jnp.broadcast_to(e1_ref[p, i:i + 1, :], (16, tm)).astype(BF16)[None]
            lim = jnp.broadcast_to(l_ref[p, i:i + 1, :], (16, tm)).astype(BF16)[None]
            term = jnp.where(rk_ref[p] < lim, e2_ref[p] * e1, jnp.zeros((), BF16))
            w = term if w is None else w + term
        parts.append(w * act[i])
    gated = jnp.concatenate(parts, axis=0).reshape(te, tm)
    acc_ref[...] += _dot(vt_ref[...], gated)

    @pl.when(j == pl.num_programs(2) - 1)
    def _():
        xf = x1_ref[0] + g2_ref[0] * acc_ref[...].T
        out_ref[0] = _rms(xf) * nfg_ref[...]


def _peer(h2t, u, vt, e1, lim, e2, rk, x1, g2, nfg):
    bz, n, d = x1.shape
    n_exp = u.shape[0]
    tm = min(PEER_TM, n)
    te = PEER_TE
    assert n % tm == 0 and n_exp % te == 0 and te % PEER_NKEYS == 0
    nt = n // tm
    tab1 = pl.BlockSpec((PEER_HEADS, te // PEER_NKEYS, tm), lambda b, i, j: (0, j, b * nt + i))
    tab2 = pl.BlockSpec((PEER_HEADS, PEER_NKEYS // 16, 16, tm), lambda b, i, j: (0, 0, 0, b * nt + i))
    tok = pl.BlockSpec((1, tm, d), lambda b, i, j: (b, i, 0))
    return pl.pallas_call(
        _peer_kernel,
        grid=(bz, nt, n_exp // te),
        in_specs=[pl.BlockSpec((d, tm), lambda b, i, j: (0, b * nt + i)),
                  pl.BlockSpec((te, d), lambda b, i, j: (j, 0)),
                  pl.BlockSpec((d, te), lambda b, i, j: (0, j)),
                  tab1, tab1, tab2, tab2, tok,
                  pl.BlockSpec((1, 1, d), lambda b, i, j: (b, 0, 0)),
                  pl.BlockSpec((1, d), lambda b, i, j: (0, 0))],
        out_specs=tok,
        out_shape=jax.ShapeDtypeStruct((bz, n, d), F32),
        scratch_shapes=[pltpu.VMEM((d, tm), F32)],
        compiler_params=_params("parallel", "parallel", "arbitrary"),
        name="peer",
    )(h2t, u, vt, e1, lim, e2, rk, x1, g2, nfg)


def _inproj_weights(w_in, conv_w, gate_b, lr_w2, alpha_b):
    widths = (ML_WIDTH, ML_WIDTH, ML_WIDTH, ML_WIDTH, N_GATES,
              GLA_KEY_WIDTH, GLA_KEY_WIDTH, GLA_WIDTH, GLA_WIDTH, 2 * GLA_RANK)
    offs = [0]
    for w in widths:
        offs.append(offs[-1] + w)
    wb = w_in.astype(BF16)
    col = lambda a, b: wb[:, offs[a]:offs[b]]
    pad_lanes = lambda a: jnp.pad(a, ((0, 0), (0, LANES - a.shape[1])))
    w2 = jnp.zeros((LANES, 2 * GLA_KEY_WIDTH), F32)
    w2 = w2.at[:GLA_RANK, :GLA_KEY_WIDTH].set(lr_w2[0])
    w2 = w2.at[GLA_RANK:2 * GLA_RANK, GLA_KEY_WIDTH:].set(lr_w2[1])
    return (col(0, 2), col(2, 3), col(3, 4), pad_lanes(col(4, 5)), col(4, 5).T,
            col(5, 7), col(7, 8), col(8, 9), pad_lanes(col(9, 10)),
            conv_w, pad_lanes(gate_b[None, :]), gate_b[:, None],
            w2.astype(BF16), alpha_b.reshape(1, 2 * GLA_KEY_WIDTH))


def _per_head_gates(gtok, gmaj):
    bz, n, _ = gtok.shape
    gt = gtok.reshape(bz, n, 2, 2, ML_HEADS).transpose(0, 4, 1, 2, 3).reshape(bz, ML_HEADS, n, 4)
    gm = gmaj.reshape(bz, 2, 2, ML_HEADS, n // ML_CHUNK, ML_CHUNK)
    gm = gm.transpose(0, 3, 4, 1, 2, 5).reshape(bz, ML_HEADS, n // ML_CHUNK, 4, ML_CHUNK)
    return gt, gm


def _token_mix_inputs(x, sh, sc, ng, wts, row_len):
    mq, mk, mv, mo, gtok, gmaj, gq, gk, gv, gr, bf, bb, mkt, gvt = _inproj(x, sh, sc, ng, wts, row_len)
    gt, gm = _per_head_gates(gtok, gmaj)
    return (mq, mk, mkt, mv, gt, gm), (gq, gk, gv, gvt, bf, bb), mo, gr


def kernel(x, c, ctx, c_ctx, w_mod, b_mod, norm1_g, w_in, ml_conv_w, ml_gate_b, ml_norm_g,
           gla_lr_w2, gla_alpha_b, gla_norm_g, w_out, norm2_g, peer_wq, peer_keys, peer_u,
           peer_v, norm_f_g):
    assert w_mod.shape[0] == 1, "single trunk layer"
    bz, n, d = x.shape
    c_all = jnp.concatenate([c, c_ctx[None, :]], axis=0)
    c_all = jnp.pad(c_all, ((0, (-c_all.shape[0]) % SUBLANES), (0, 0)))
    mod = _modulation(c_all, w_mod[0], b_mod[0][None, :])
    sh1, sc1, g1, sh2, sc2, g2 = [m[:, None, :] for m in jnp.split(mod[:bz], 6, axis=1)]
    mod_c = mod[bz]
    csh1, csc1 = mod_c[None, None, :d], mod_c[None, None, d:2 * d]

    ng1 = norm1_g[0][None, :]
    wts = _inproj_weights(w_in[0], ml_conv_w[0], ml_gate_b[0], gla_lr_w2[0], gla_alpha_b[0])
    ml_lat, gla_lat, mo, gr = _token_mix_inputs(x, sh1, sc1, ng1, wts, GRID_W)
    ml_ctx, gla_ctx, _, _ = _token_mix_inputs(ctx, csh1, csc1, ng1, wts, ctx.shape[1])
    mhf, mhb = _mlstm(*ml_lat, *ml_ctx)
    ghf, ghb = _gla(*gla_lat, *gla_ctx)

    keys = peer_keys[0].reshape(2 * PEER_HEADS, PEER_NKEYS, PEER_HALF).astype(BF16)
    x1, h2t, e1, lim, e2, rk = _mix(
        x, mhf, mhb, ghf, ghb, mo, gr, ml_norm_g[0][None, :], gla_norm_g[0][None, :],
        w_out[0].astype(BF16), g1, sh2, sc2, norm2_g[0][None, :], peer_wq[0].astype(BF16), keys)
    return _peer(h2t, peer_u[0].astype(BF16), peer_v[0].astype(BF16).T, e1, lim, e2, rk,
                 x1, g2, norm_f_g[None, :])
```

```python
import functools

import jax
import jax.numpy as jnp
from jax import lax
from jax.experimental import pallas as pl
from jax.experimental.pallas import tpu as pltpu

F32 = jnp.float32
BF16 = jnp.bfloat16

EPS = 1e-6
D_MODEL = 1024
GRID_W = 64

ML_HEADS = 4
ML_HEAD_DIM = 128
ML_WIDTH = ML_HEADS * ML_HEAD_DIM
ML_CHUNK = 128
GLA_HEADS = 4
GLA_DK = 64
GLA_DV = 128
GLA_KEY_WIDTH = GLA_HEADS * GLA_DK
GLA_WIDTH = GLA_HEADS * GLA_DV
GLA_RANK = 16
GLA_TAU = 16.0
GLA_CHUNK = 64
N_GATES = 4 * ML_HEADS

PEER_HEADS = 8
PEER_NKEYS = 128
PEER_TOPK = 16
PEER_HALF = 128

LANES = 128
SUBLANES = 8
VMEM_LIMIT = 56 * 1024 * 1024

INPROJ_TM = 512
CUMSUM_ROWS = 256
ML_HEADS_PER_STEP = 4
GLA_HEADS_PER_STEP = 4
MIX_TM = 512
PEER_TM = 512
PEER_TE = 2048

NEG_INF = float("-inf")


def _params(*sem):
    return pltpu.CompilerParams(dimension_semantics=sem, vmem_limit_bytes=VMEM_LIMIT)


def _dot(a, b):
    return jnp.dot(a, b, preferred_element_type=F32)


def _dot_nt(a, b):
    return lax.dot_general(a, b, (((1,), (1,)), ((), ())), preferred_element_type=F32)


def _split2(x):
    hi = x.astype(BF16)
    return hi, (x - hi.astype(F32)).astype(BF16)


def _dot_exact_rhs(a01, x):
    hi, lo = _split2(x)
    return _dot(a01, hi) + _dot(a01, lo)


def _dot_exact_lhs(x, a01):
    hi, lo = _split2(x)
    return _dot(hi, a01) + _dot(lo, a01)


def _block_rows(mask, x):
    r = mask.shape[0]
    return jnp.concatenate([_dot_exact_rhs(mask, x[i:i + r]) for i in range(0, x.shape[0], r)], axis=0)


def _block_cols(x, mask):
    r = mask.shape[0]
    return jnp.concatenate([_dot_exact_lhs(x[:, i:i + r], mask) for i in range(0, x.shape[1], r)], axis=1)


def _sigmoid(x):
    return 1.0 / (1.0 + jnp.exp(-x))


def _log_sigmoid(x):
    return jnp.minimum(x, 0.0) - jnp.log(1.0 + jnp.exp(-jnp.abs(x)))


def _rms(x):
    return x * lax.rsqrt(jnp.mean(x * x, axis=-1, keepdims=True) + EPS)


def _mod_kernel(c_ref, w_ref, b_ref, o_ref):
    cond = c_ref[...]
    cond = cond * _sigmoid(cond)
    ch, cl = cond.astype(BF16), (cond - cond.astype(BF16).astype(F32)).astype(BF16)
    w = w_ref[...]
    wh = w.astype(BF16)
    wl = (w - wh.astype(F32)).astype(BF16)
    o_ref[...] = _dot(ch, wh) + _dot(ch, wl) + _dot(cl, wh) + b_ref[...]


def _modulation(c_all, w_mod, b_mod):
    rows, d = c_all.shape
    n_out = w_mod.shape[1]
    tn = 512
    return pl.pallas_call(
        _mod_kernel,
        grid=(n_out // tn,),
        in_specs=[pl.BlockSpec((rows, d), lambda j: (0, 0)),
                  pl.BlockSpec((d, tn), lambda j: (0, j)),
                  pl.BlockSpec((1, tn), lambda j: (0, j))],
        out_specs=pl.BlockSpec((rows, tn), lambda j: (0, j)),
        out_shape=jax.ShapeDtypeStruct((rows, n_out), F32),
        compiler_params=_params("arbitrary"),
        name="mod",
    )(c_all, w_mod, b_mod)


def _chunk_masks(tm, chunk):
    r = lax.broadcasted_iota(jnp.int32, (tm, tm), 0)
    c = lax.broadcasted_iota(jnp.int32, (tm, tm), 1)
    same = (r // chunk) == (c // chunk)
    prefix = jnp.where(same & (c <= r), 1.0, 0.0).astype(BF16)
    suffix = jnp.where(same & (c >= r), 1.0, 0.0).astype(BF16)
    return prefix, suffix


def _inproj_kernel(x_ref, sh_ref, sc_ref, ng_ref, wqk_ref, wv_ref, wo_ref, wg_ref, wgt_ref,
                   wgqk_ref, wgv_ref, wgr_ref, wlr_ref, conv_ref, gb_ref, gbc_ref, w2_ref, ab_ref,
                   mq_ref, mk_ref, mv_ref, mo_ref, gtok_ref, gmaj_ref,
                   gq_ref, gk_ref, gv_ref, gr_ref, bf_ref, bb_ref, mkt_ref, gvt_ref, *, row_len):
    tm = x_ref.shape[1]
    x = x_ref[0]
    h = _rms(x) * ng_ref[...] * (1.0 + sc_ref[0]) + sh_ref[0]
    hb = h.astype(BF16)

    u = _dot(hb, wqk_ref[...])
    pos = lax.broadcasted_iota(jnp.int32, (tm, 1), 0) % row_len
    up = jnp.where(pos == 0, 0.0, pltpu.roll(u, 1, 0))
    dn = jnp.where(pos == row_len - 1, 0.0, pltpu.roll(u, tm - 1, 0))
    cw = conv_ref[...]
    y = cw[0:1] * up + cw[1:2] * u + cw[2:3] * dn
    y = y * _sigmoid(y)
    mq_ref[0] = y[:, :ML_WIDTH].astype(BF16)
    mk = y[:, ML_WIDTH:] * ML_HEAD_DIM ** -0.5
    mk_ref[0] = mk.astype(BF16)
    mkt = mk.T.astype(BF16)
    for hh in range(ML_HEADS):
        for ci in range(tm // ML_CHUNK):
            mkt_ref[0, hh, ci] = mkt[hh * ML_HEAD_DIM:(hh + 1) * ML_HEAD_DIM, ci * ML_CHUNK:(ci + 1) * ML_CHUNK]
    mv_ref[0] = _dot(hb, wv_ref[...]).astype(BF16)
    mo_ref[0] = _dot(hb, wo_ref[...]).astype(BF16)

    pre128, suf128 = _chunk_masks(min(tm, CUMSUM_ROWS), ML_CHUNK)
    g = _dot(hb, wg_ref[...]) + gb_ref[...]
    col = lax.broadcasted_iota(jnp.int32, (1, LANES), 1)
    is_f = ((col % 8) >= 4) & (col < N_GATES)
    is_bwd = col >= 8
    lf = jnp.where(is_f, _log_sigmoid(g), 0.0)
    cum = jnp.where(is_bwd, _block_rows(suf128, lf), _block_rows(pre128, lf))
    gtok_ref[0] = jnp.where(is_f, cum, g)[:, :N_GATES]

    gt = _dot_nt(wgt_ref[...], hb) + gbc_ref[...]
    row = lax.broadcasted_iota(jnp.int32, (N_GATES, 1), 0)
    is_f_r = (row % 8) >= 4
    lft = jnp.where(is_f_r, _log_sigmoid(gt), 0.0)
    cum_t = jnp.where(row >= 8, _block_cols(lft, pre128), _block_cols(lft, suf128))
    gmaj_ref[0] = jnp.where(is_f_r, cum_t, gt)

    gqk = _dot(hb, wgqk_ref[...])
    gq_ref[0] = (gqk[:, :GLA_KEY_WIDTH] * GLA_DK ** -0.5).astype(BF16)
    gk_ref[0] = gqk[:, GLA_KEY_WIDTH:].astype(BF16)
    gv = _dot(hb, wgv_ref[...])
    gv_ref[0] = gv.astype(BF16)
    gvt = gv.T.astype(BF16)
    for hh in range(GLA_HEADS):
        for ci in range(tm // GLA_CHUNK):
            gvt_ref[0, hh, ci] = gvt[hh * GLA_DV:(hh + 1) * GLA_DV, ci * GLA_CHUNK:(ci + 1) * GLA_CHUNK]
    gr_ref[0] = _dot(hb, wgr_ref[...]).astype(BF16)
    glr = _dot(hb, wlr_ref[...]).astype(BF16)
    alpha = _dot(glr, w2_ref[...]) + ab_ref[...]
    la = _log_sigmoid(alpha) * (1.0 / GLA_TAU)
    pre64, suf64 = _chunk_masks(min(tm, CUMSUM_ROWS), GLA_CHUNK)
    bf_ref[0] = _block_rows(pre64, la[:, :GLA_KEY_WIDTH])
    bb_ref[0] = _block_rows(suf64, la[:, GLA_KEY_WIDTH:])


def _inproj(x, sh, sc, ng, wts, row_len):
    bz, n, d = x.shape
    tm = min(INPROJ_TM, n)
    assert n % tm == 0 and tm % row_len == 0 and tm % ML_CHUNK == 0
    grid = (bz, n // tm)
    per_b = (lambda b, i: (b, 0, 0)) if sh.shape[0] == bz else (lambda b, i: (0, 0, 0))

    def full(a):
        return pl.BlockSpec(a.shape, lambda b, i, nd=a.ndim: (0,) * nd)

    def tok(width):
        return pl.BlockSpec((1, tm, width), lambda b, i: (b, i, 0))

    def out(width, dtype):
        return jax.ShapeDtypeStruct((bz, n, width), dtype)

    in_specs = [tok(d), pl.BlockSpec((1, 1, d), per_b), pl.BlockSpec((1, 1, d), per_b), full(ng)]
    in_specs += [full(w) for w in wts]
    out_specs = [tok(ML_WIDTH)] * 4 + [tok(N_GATES), pl.BlockSpec((1, N_GATES, tm), lambda b, i: (b, 0, i))]
    out_specs += [tok(GLA_KEY_WIDTH)] * 2 + [tok(GLA_WIDTH)] * 2 + [tok(GLA_KEY_WIDTH)] * 2
    out_shape = [out(ML_WIDTH, BF16)] * 4 + [out(N_GATES, F32),
                                              jax.ShapeDtypeStruct((bz, N_GATES, n), F32)]
    out_shape += ([out(GLA_KEY_WIDTH, BF16)] * 2 + [out(GLA_WIDTH, BF16)] * 2
                  + [out(GLA_KEY_WIDTH, F32)] * 2)
    for heads, dim, chunk in ((ML_HEADS, ML_HEAD_DIM, ML_CHUNK), (GLA_HEADS, GLA_DV, GLA_CHUNK)):
        out_specs.append(pl.BlockSpec((1, heads, tm // chunk, dim, chunk), lambda b, i: (b, 0, i, 0, 0)))
        out_shape.append(jax.ShapeDtypeStruct((bz, heads, n // chunk, dim, chunk), BF16))
    return pl.pallas_call(
        functools.partial(_inproj_kernel, row_len=row_len),
        grid=grid, in_specs=in_specs, out_specs=out_specs, out_shape=out_shape,
        compiler_params=_params("parallel", "arbitrary"),
        name="inproj",
    )(x, sh, sc, ng, *wts)


def _bmm(a, b):
    return lax.dot_general(a, b, (((2,), (1,)), ((0,), (0,))), preferred_element_type=F32)


def _bmm_nt(a, b):
    return lax.dot_general(a, b, (((2,), (2,)), ((0,), (0,))), preferred_element_type=F32)


def _by_direction(x, fwd, bwd):
    half = x.shape[0] // 2
    return jnp.concatenate([fwd(x[:half]), bwd(x[half:])], axis=0)


def _ml_chunks(q, k, kt, v, gt, gm, masks, state, want_out):
    ct, nv, m = state
    ll = q.shape[1]
    ig_col = _by_direction(gt, lambda g: g[:, :, 0:1], lambda g: g[:, :, 2:3])
    b_col = _by_direction(gt, lambda g: g[:, :, 1:2], lambda g: g[:, :, 3:4])
    ig_row = _by_direction(gm, lambda g: g[:, 0:1, :], lambda g: g[:, 2:3, :])
    b_row = _by_direction(gm, lambda g: g[:, 1:2, :], lambda g: g[:, 3:4, :])
    b_end = _by_direction(b_row, lambda b: b[:, :, ll - 1:ll], lambda b: b[:, :, 0:1])
    h = None
    if want_out:
        d_log = _by_direction(b_col - b_row + ig_row,
                              lambda x: jnp.where(masks[0], x, NEG_INF),
                              lambda x: jnp.where(masks[1], x, NEG_INF))
        inter_log = b_col + m
        m_t = jnp.maximum(inter_log, jnp.max(d_log, axis=2, keepdims=True))
        scores = _bmm_nt(q, k) * jnp.exp(d_log - m_t)
        inter = jnp.exp(inter_log - m_t)
        qf = q.astype(F32)
        num = _bmm(scores.astype(BF16), v) + inter * _bmm(q, ct.astype(BF16))
        den = jnp.sum(scores, axis=2, keepdims=True) + inter * jnp.sum(qf * nv, axis=2, keepdims=True)
        h = num / jnp.maximum(jnp.abs(den), jnp.exp(-m_t))
    m_loc = jnp.max(b_end - b_row + ig_row, axis=2, keepdims=True)
    w_col = jnp.exp(b_end - b_col + ig_col - m_loc)
    c_loc = _bmm(kt, (w_col * v.astype(F32)).astype(BF16))
    n_loc = jnp.sum(w_col * k.astype(F32), axis=1, keepdims=True)
    m_new = jnp.maximum(b_end + m, m_loc)
    a = jnp.exp(b_end + m - m_new)
    bb = jnp.exp(m_loc - m_new)
    return h, (a * ct + bb * c_loc, a * nv + bb * n_loc, m_new)


def _mlstm_kernel(q_ref, k_ref, kt_ref, v_ref, gt_ref, gm_ref,
                  qc_ref, kc_ref, ktc_ref, vc_ref, gtc_ref, gmc_ref, hf_ref, hb_ref):
    ll = ML_CHUNK
    nc = q_ref.shape[1] // ll
    ncc = qc_ref.shape[1] // ll
    r = lax.broadcasted_iota(jnp.int32, (ll, ll), 0)
    c = lax.broadcasted_iota(jnp.int32, (ll, ll), 1)
    masks = (c <= r, c >= r)

    heads = gt_ref.shape[1]
    head_cols = [slice(hh * ML_HEAD_DIM, (hh + 1) * ML_HEAD_DIM) for hh in range(heads)]

    def rows_of(ci):
        return pl.ds(pl.multiple_of(ci * ll, ll), ll)

    def load(refs, cf, cb):
        qr, kr, ktr, vr, gtr, gmr = refs
        sites = [(hh, ci) for ci in (cf, cb) for hh in range(heads)]
        seq = lambda r: jnp.stack([r[0, rows_of(ci), head_cols[hh]] for hh, ci in sites])
        return (seq(qr), seq(kr), jnp.stack([ktr[0, hh, ci] for hh, ci in sites]), seq(vr),
                jnp.stack([gtr[0, hh, rows_of(ci), :] for hh, ci in sites]),
                jnp.stack([gmr[0, hh, ci] for hh, ci in sites]))

    chains = 2 * heads
    zero = (jnp.zeros((chains, ML_HEAD_DIM, ML_HEAD_DIM), F32), jnp.zeros((chains, 1, ML_HEAD_DIM), F32),
            jnp.zeros((chains, 1, 1), F32))
    ctx_refs = (qc_ref, kc_ref, ktc_ref, vc_ref, gtc_ref, gmc_ref)
    lat_refs = (q_ref, k_ref, kt_ref, v_ref, gt_ref, gm_ref)

    def ctx_body(i, state):
        _, state = _ml_chunks(*load(ctx_refs, i, ncc - 1 - i), masks, state, False)
        return state

    def lat_body(i, state):
        cb = nc - 1 - i
        h, state = _ml_chunks(*load(lat_refs, i, cb), masks, state, True)
        for hh in range(heads):
            hf_ref[0, rows_of(i), head_cols[hh]] = h[hh]
            hb_ref[0, rows_of(cb), head_cols[hh]] = h[heads + hh]
        return state

    state = lax.fori_loop(0, ncc, ctx_body, zero)
    lax.fori_loop(0, nc, lat_body, state)


def _mlstm(q, k, kt, v, gt, gm, qc, kc, ktc, vc, gtc, gmc):
    bz, n, _ = q.shape
    ncx = qc.shape[1]
    hps = ML_HEADS_PER_STEP

    def seq(nn):
        return pl.BlockSpec((1, nn, hps * ML_HEAD_DIM), lambda b, h: (b, 0, h))

    def ktr(nn):
        return pl.BlockSpec((1, hps, nn // ML_CHUNK, ML_HEAD_DIM, ML_CHUNK), lambda b, h: (b, h, 0, 0, 0))

    def gtok(nn):
        return pl.BlockSpec((1, hps, nn, 4), lambda b, h: (b, h, 0, 0))

    def gmaj(nn):
        return pl.BlockSpec((1, hps, nn // ML_CHUNK, 4, ML_CHUNK), lambda b, h: (b, h, 0, 0, 0))

    def specs(nn):
        return [seq(nn), seq(nn), ktr(nn), seq(nn), gtok(nn), gmaj(nn)]

    return pl.pallas_call(
        _mlstm_kernel,
        grid=(bz, ML_HEADS // hps),
        in_specs=specs(n) + specs(ncx),
        out_specs=[seq(n)] * 2,
        out_shape=[jax.ShapeDtypeStruct((bz, n, ML_WIDTH), F32)] * 2,
        compiler_params=_params("parallel", "arbitrary"),
        name="mlstm",
    )(q, k, kt, v, gt, gm, qc, kc, ktc, vc, gtc, gmc)


def _gla_chunks(q, k, v, vt, b, masks, st, want_out):
    ll = q.shape[1]
    qf, kf = q.astype(F32), k.astype(F32)
    ref = _by_direction(b, lambda x: x[:, ll // 2:ll // 2 + 1], lambda x: x[:, ll // 2 - 1:ll // 2])
    b_end = _by_direction(b, lambda x: x[:, ll - 1:ll], lambda x: x[:, 0:1])
    o = None
    if want_out:
        att = _bmm_nt((qf * jnp.exp(b - ref)).astype(BF16), (kf * jnp.exp(ref - b)).astype(BF16))
        att = _by_direction(att, lambda x: jnp.where(masks[0], x, 0.0), lambda x: jnp.where(masks[1], x, 0.0))
        o = _bmm(att.astype(BF16), v) + _bmm_nt((qf * jnp.exp(b)).astype(BF16), st.astype(BF16))
    s_loc = _bmm(vt, (kf * jnp.exp(b_end - b)).astype(BF16))
    return o, jnp.exp(b_end) * st + s_loc


def _gla_kernel(q_ref, k_ref, v_ref, vt_ref, bf_ref, bb_ref,
                qc_ref, kc_ref, vc_ref, vtc_ref, bfc_ref, bbc_ref, of_ref, ob_ref):
    ll = GLA_CHUNK
    nc = q_ref.shape[1] // ll
    ncc = qc_ref.shape[1] // ll
    r = lax.broadcasted_iota(jnp.int32, (ll, ll), 0)
    c = lax.broadcasted_iota(jnp.int32, (ll, ll), 1)
    masks = (c <= r, c >= r)

    heads = vt_ref.shape[1]

    def rows_of(ci):
        return pl.ds(pl.multiple_of(ci * ll, ll), ll)

    head_cols = [slice(hh * GLA_DV, (hh + 1) * GLA_DV) for hh in range(heads)]
    key_cols = [slice(hh * GLA_DK, (hh + 1) * GLA_DK) for hh in range(heads)]

    def load(refs, cf, cb):
        qr, kr, vr, vtr, bfr, bbr = refs
        sites = [(hh, ci) for ci in (cf, cb) for hh in range(heads)]
        per_head = lambda r: jnp.stack([r[0, rows_of(ci), key_cols[hh]] for hh, ci in sites])
        return (per_head(qr), per_head(kr),
                jnp.stack([vr[0, rows_of(ci), head_cols[hh]] for hh, ci in sites]),
                jnp.stack([vtr[0, hh, ci] for hh, ci in sites]),
                jnp.stack([bfr[0, rows_of(cf), key_cols[hh]] for hh in range(heads)]
                          + [bbr[0, rows_of(cb), key_cols[hh]] for hh in range(heads)]))

    zero = jnp.zeros((2 * heads, GLA_DV, GLA_DK), F32)
    ctx_refs = (qc_ref, kc_ref, vc_ref, vtc_ref, bfc_ref, bbc_ref)
    lat_refs = (q_ref, k_ref, v_ref, vt_ref, bf_ref, bb_ref)

    def ctx_body(i, st):
        _, st = _gla_chunks(*load(ctx_refs, i, ncc - 1 - i), masks, st, False)
        return st

    def lat_body(i, st):
        cb = nc - 1 - i
        o, st = _gla_chunks(*load(lat_refs, i, cb), masks, st, True)
        for hh in range(heads):
            of_ref[0, rows_of(i), head_cols[hh]] = o[hh]
            ob_ref[0, rows_of(cb), head_cols[hh]] = o[heads + hh]
        return st

    st = lax.fori_loop(0, ncc, ctx_body, zero)
    lax.fori_loop(0, nc, lat_body, st, unroll=2)


def _gla(q, k, v, vt, bf, bb, qc, kc, vc, vtc, bfc, bbc):
    bz, n, _ = q.shape
    ncx = qc.shape[1]
    hps = GLA_HEADS_PER_STEP

    def key(nn):
        return pl.BlockSpec((1, nn, hps * GLA_DK), lambda b, h: (b, 0, h))

    def val(nn):
        return pl.BlockSpec((1, nn, hps * GLA_DV), lambda b, h: (b, 0, h))

    def valt(nn):
        return pl.BlockSpec((1, hps, nn // GLA_CHUNK, GLA_DV, GLA_CHUNK), lambda b, h: (b, h, 0, 0, 0))

    def specs(nn):
        return [key(nn), key(nn), val(nn), valt(nn), key(nn), key(nn)]

    return pl.pallas_call(
        _gla_kernel,
        grid=(bz, GLA_HEADS // hps),
        in_specs=specs(n) + specs(ncx),
        out_specs=[val(n)] * 2,
        out_shape=[jax.ShapeDtypeStruct((bz, n, GLA_WIDTH), F32)] * 2,
        compiler_params=_params("parallel", "arbitrary"),
        name="gla",
    )(q, k, v, vt, bf, bb, qc, kc, vc, vtc, bfc, bbc)


def _head_rms(h, g, width):
    parts = [_rms(h[:, i:i + width]) for i in range(0, h.shape[1], width)]
    return jnp.concatenate(parts, axis=1) * g


def _oddeven_merge(lo, hi, r):
    step = r * 2
    if step < hi - lo:
        yield from _oddeven_merge(lo, hi, step)
        yield from _oddeven_merge(lo + r, hi, step)
        yield from [(i, i + r) for i in range(lo + r, hi - r, step)]
    else:
        yield (lo, lo + r)


def _oddeven_merge_sort(lo, hi):
    if hi - lo >= 1:
        mid = lo + (hi - lo) // 2
        yield from _oddeven_merge_sort(lo, mid)
        yield from _oddeven_merge_sort(mid + 1, hi)
        yield from _oddeven_merge(lo, hi, 1)


def _exchange(v, i, j):
    v[i], v[j] = jnp.maximum(v[i], v[j]), jnp.minimum(v[i], v[j])


def _merge_top16(v, shift):
    other = [pltpu.roll(x, shift, 0) for x in v]
    out = []
    for j in range(PEER_TOPK):
        a = v[j] if j < len(v) else None
        b = other[PEER_TOPK - 1 - j] if PEER_TOPK - 1 - j < len(v) else None
        out.append(a if b is None else b if a is None else jnp.maximum(a, b))
    return out


def _sort_bitonic16(v):
    v = list(v)
    for d in (8, 4, 2, 1):
        for j in range(PEER_TOPK):
            if not j & d:
                _exchange(v, j, j + d)
    return v


def _sixteenth_largest(blocks):
    v = list(blocks)
    assert SUBLANES <= len(v) <= PEER_TOPK
    for i, j in _oddeven_merge_sort(0, PEER_TOPK - 1):
        if j < len(v):
            _exchange(v, i, j)
    v = _sort_bitonic16(_merge_top16(v, 4))
    v = _sort_bitonic16(_merge_top16(v, 2))
    v = _merge_top16(v, 1)
    return functools.reduce(jnp.minimum, v)[0:1]


def _sorted_top16(tiles):
    v = list(tiles)
    n = len(v)
    for i, j in _oddeven_merge_sort(0, n - 1):
        _exchange(v, i, j)
    for shift in (4, 2, 1):
        v = _sort_bitonic16(_merge_top16(v, shift))
    sub = lax.broadcasted_iota(jnp.int32, v[0].shape, 0)
    halves = []
    for lo in (0, SUBLANES):
        rows = v[lo]
        for r in range(1, SUBLANES):
            rows = jnp.where(sub == r, v[lo + r], rows)
        halves.append(rows)
    repeated = jnp.zeros_like(v[0])
    for r in range(n - 1):
        repeated = jnp.where(v[r] == v[r + 1], 1.0, repeated)
    return jnp.concatenate(halves, axis=0), repeated[0:1]


def _prefix_count(x, thr, test):
    r = [thr[i:i + 1] for i in range(PEER_TOPK)]
    sel = jnp.where
    t8 = test(x, r[7])
    t4 = test(x, sel(t8, r[11], r[3]))
    t2 = test(x, sel(t8, sel(t4, r[13], r[9]), sel(t4, r[5], r[1])))
    t1 = test(x, sel(t8, sel(t4, sel(t2, r[14], r[12]), sel(t2, r[10], r[8])),
                     sel(t4, sel(t2, r[6], r[4]), sel(t2, r[2], r[0]))))
    count = sel(t8, 8.0, 0.0) + sel(t4, 4.0, 0.0) + sel(t2, 2.0, 0.0) + sel(t1, 1.0, 0.0)
    return sel(test(x, r[15]), float(PEER_TOPK), count)


def _count(mask):
    return jnp.sum(jnp.where(mask, 1.0, 0.0), axis=0, keepdims=True)


def _top16(s):
    nk, tl = s.shape
    slot = lax.broadcasted_iota(jnp.int32, (PEER_TOPK, tl), 0)
    key_id = lax.broadcasted_iota(jnp.int32, (nk, tl), 0).astype(F32)

    def body(r, carry):
        s, rank, vals = carry
        m = jnp.max(s, axis=0, keepdims=True)
        first = jnp.min(jnp.where(s == m, key_id, float(nk)), axis=0, keepdims=True)
        sel = key_id == first
        rank = jnp.where(sel, jnp.asarray(r).astype(F32), rank)
        s = jnp.where(sel, NEG_INF, s)
        vals = jnp.where(slot == r, m, vals)
        return s, rank, vals

    init = (s, jnp.full((nk, tl), float(PEER_TOPK), F32), jnp.zeros((PEER_TOPK, tl), F32))
    _, rank, vals = lax.fori_loop(0, PEER_TOPK, body, init)
    return vals, rank


def _pair_candidates(v1, v2):
    blocks = [v1[0:1] + v2[0:8], v1[0:1] + v2[8:16]]
    blocks += [v1[a:a + 1] + v2[0:8] for a in range(1, 8)]
    blocks += [v1[8:16] + v2[0:1]]
    return jnp.concatenate(blocks, axis=0)


def _blocks(cand):
    return [cand[i:i + SUBLANES] for i in range(0, cand.shape[0], SUBLANES)]


def _pick16(cand):
    i = lax.broadcasted_iota(jnp.int32, cand.shape, 0)
    blk, rr = i // 8, i % 8
    flat = jnp.where(blk == 0, rr, jnp.where(blk == 1, 8 + rr,
                     jnp.where(blk <= 8, (blk - 1) * 16 + rr, (8 + rr) * 16))).astype(F32)

    def body(_, carry):
        cand, picked = carry
        m = jnp.max(cand, axis=0, keepdims=True)
        first = jnp.min(jnp.where(cand == m, flat, 1e9), axis=0, keepdims=True)
        sel = flat == first
        return jnp.where(sel, NEG_INF, cand), jnp.where(sel, 1.0, picked)

    _, picked = lax.fori_loop(0, PEER_TOPK, body, (cand, jnp.zeros_like(cand)))
    return picked


def _first_key_thresholds(picked, v1):
    inf = jnp.inf
    lo = jnp.where(picked[0:8] > 0.0, v1[0:1], inf)
    for a in range(1, 8):
        lo = jnp.minimum(lo, jnp.where(picked[8 * (a + 1):8 * (a + 2)] > 0.0, v1[a:a + 1], inf))
    tail = jnp.min(jnp.where(picked[72:80] > 0.0, v1[8:16], inf), axis=0, keepdims=True)
    row = lax.broadcasted_iota(jnp.int32, lo.shape, 0)
    lo = jnp.minimum(lo, jnp.where(row == 0, tail, inf))
    hi = jnp.where(picked[8:16] > 0.0, v1[0:1], inf)
    return jnp.concatenate([lo, hi], axis=0)


def _pair_counts(cand, picked):
    z = jnp.sum(picked * jnp.exp(cand - cand[0:1]), axis=0, keepdims=True)
    n_rows = [jnp.sum(picked[0:16], axis=0, keepdims=True)]
    n_rows += [jnp.sum(picked[8 * (a + 1):8 * (a + 2)], axis=0, keepdims=True) for a in range(1, 8)]
    return jnp.concatenate(n_rows + [picked[72:80]], axis=0), z


def _mix_kernel(x_ref, mhf_ref, mhb_ref, ghf_ref, ghb_ref, mo_ref, gr_ref, mlg_ref, glag_ref,
                wout_ref, g1_ref, sh2_ref, sc2_ref, n2g_ref, wq_ref, keys_ref,
                x1_ref, h2t_ref, e1_ref, l_ref, e2_ref, rk_ref, q_scr, s_scr):
    ml = _head_rms(mhf_ref[0] + mhb_ref[0], mlg_ref[...], ML_HEAD_DIM)
    ml = _sigmoid(mo_ref[0].astype(F32)) * ml
    gl = _head_rms(ghf_ref[0] + ghb_ref[0], glag_ref[...], GLA_DV)
    gr = gr_ref[0].astype(F32)
    gl = gr * _sigmoid(gr) * gl
    mix = jnp.concatenate([ml, gl], axis=1).astype(BF16)
    x1 = x_ref[0] + g1_ref[0] * _dot(mix, wout_ref[...])
    x1_ref[0] = x1
    h2 = _rms(x1) * n2g_ref[...] * (1.0 + sc2_ref[0]) + sh2_ref[0]
    h2t_ref[...] = h2.T.astype(BF16)
    qall = _dot(h2.astype(BF16), wq_ref[...]).astype(BF16)
    for j in range(2 * PEER_HEADS):
        q_scr[j] = qall[:, j * PEER_HALF:(j + 1) * PEER_HALF]

    tm = x_ref.shape[1]
    topk = float(PEER_TOPK)
    lane_chunks = [slice(i, i + LANES) for i in range(0, tm, LANES)]

    def write_tables(p, lanes, s1, s2, v1, v2, z, lim, rank2):
        e1_ref[p, :, lanes] = jnp.exp(s1 - v1[0:1]) * (1.0 / z)
        l_ref[p, :, lanes] = lim
        e2 = jnp.exp(s2 - v2[0:1]).astype(BF16)
        e2_ref[p, :, :, lanes] = e2.reshape(PEER_NKEYS // 16, 16, LANES)
        rk_ref[p, :, :, lanes] = rank2.astype(BF16).reshape(PEER_NKEYS // 16, 16, LANES)

    def head_body(p, carry):
        for hf in range(2):
            s_scr[hf] = _dot_nt(keys_ref[2 * p + hf], q_scr[2 * p + hf])
        excess = jnp.zeros((1, LANES), F32)
        vals = []
        for hf in range(2):
            for lanes in lane_chunks:
                tiles = [s_scr[hf, i:i + SUBLANES, lanes] for i in range(0, PEER_NKEYS, SUBLANES)]
                top, repeated = _sorted_top16(tiles)
                vals.append(top)
                excess = jnp.maximum(excess, repeated)
        v1s, v2s = vals[:len(lane_chunks)], vals[len(lane_chunks):]
        cands = [_pair_candidates(v1, v2) for v1, v2 in zip(v1s, v2s)]
        taus = [_sixteenth_largest(_blocks(c)) for c in cands]
        for lanes, v1, v2, cand, tau in zip(lane_chunks, v1s, v2s, cands, taus):
            s1, s2 = s_scr[0, :, lanes], s_scr[1, :, lanes]
            picked = jnp.where(cand >= tau, 1.0, 0.0)
            _, z = _pair_counts(cand, picked)
            theta = _first_key_thresholds(picked, v1)
            lim = _prefix_count(s1, theta, lambda x, t: x >= t)
            rank2 = _prefix_count(s2, v2, lambda x, t: x < t)
            write_tables(p, lanes, s1, s2, v1, v2, z, lim, rank2)
            n_marked = (_count(s1 >= v1[PEER_TOPK - 1:PEER_TOPK]) + _count(rank2 < topk)
                        + jnp.sum(picked, axis=0, keepdims=True))
            excess = jnp.maximum(excess, n_marked - 3.0 * topk)

        @pl.when(jnp.max(excess) > 0.0)
        def _():
            for lanes in lane_chunks:
                s1, s2 = s_scr[0, :, lanes], s_scr[1, :, lanes]
                v1, rank1 = _top16(s1)
                v2, rank2 = _top16(s2)
                cand = _pair_candidates(v1, v2)
                n, z = _pair_counts(cand, _pick16(cand))
                lim = jnp.zeros_like(rank1)
                for a in range(PEER_TOPK):
                    lim = jnp.where(rank1 == float(a), n[a:a + 1], lim)
                write_tables(p, lanes, s1, s2, v1, v2, z, lim, rank2)

        return carry

    lax.fori_loop(0, PEER_HEADS, head_body, 0)


def _mix(x, mhf, mhb, ghf, ghb, mo, gr, mlg, glag, wout, g1, sh2, sc2, n2g, wq, keys):
    bz, n, d = x.shape
    tm = min(MIX_TM, n)
    assert n % tm == 0 and tm % LANES == 0
    nt = n // tm
    t_all = bz * n

    def tok(width):
        return pl.BlockSpec((1, tm, width), lambda b, i: (b, i, 0))

    def full(a):
        return pl.BlockSpec(a.shape, lambda b, i, nd=a.ndim: (0,) * nd)

    vec = pl.BlockSpec((1, 1, d), lambda b, i: (b, 0, 0))
    tab1 = pl.BlockSpec((PEER_HEADS, PEER_NKEYS, tm), lambda b, i: (0, 0, b * nt + i))
    tab2 = pl.BlockSpec((PEER_HEADS, PEER_NKEYS // 16, 16, tm), lambda b, i: (0, 0, 0, b * nt + i))
    tab1_shape = jax.ShapeDtypeStruct((PEER_HEADS, PEER_NKEYS, t_all), F32)
    tab2_shape = jax.ShapeDtypeStruct((PEER_HEADS, PEER_NKEYS // 16, 16, t_all), BF16)
    return pl.pallas_call(
        _mix_kernel,
        grid=(bz, nt),
        in_specs=[tok(d)] + [tok(ML_WIDTH)] * 6 + [full(mlg), full(glag), full(wout), vec, vec, vec,
                                                    full(n2g), full(wq), full(keys)],
        out_specs=[tok(d), pl.BlockSpec((d, tm), lambda b, i: (0, b * nt + i)), tab1, tab1, tab2, tab2],
        out_shape=[jax.ShapeDtypeStruct((bz, n, d), F32), jax.ShapeDtypeStruct((d, t_all), BF16),
                   tab1_shape, tab1_shape, tab2_shape, tab2_shape],
        scratch_shapes=[pltpu.VMEM((2 * PEER_HEADS, tm, PEER_HALF), BF16),
                        pltpu.VMEM((2, PEER_NKEYS, tm), F32)],
        compiler_params=_params("parallel", "arbitrary"),
        name="mix",
    )(x, mhf, mhb, ghf, ghb, mo, gr, mlg, glag, wout, g1, sh2, sc2, n2g, wq, keys)


def _peer_kernel(h2t_ref, u_ref, vt_ref, e1_ref, l_ref, e2_ref, rk_ref, x1_ref, g2_ref, nfg_ref,
                 out_ref, acc_ref):
    j = pl.program_id(2)
    te, tm = u_ref.shape[0], h2t_ref.shape[1]
    rows_per_step = te // PEER_NKEYS
    tiles = PEER_NKEYS // 16

    @pl.when(j == 0)
    def _():
        acc_ref[...] = jnp.zeros_like(acc_ref)

    act = _dot(u_ref[...], h2t_ref[...])
    act = act.astype(BF16)
    act = (0.5 * act * (1.0 + lax.erf(act * 2.0 ** -0.5))).reshape(rows_per_step, tiles, 16, tm)
    parts = []
    for i in range(rows_per_step):
        w = None
        for p in range(PEER_HEADS):
            e1 = jnp.broadcast_to(e1_ref[p, i:i + 1, :], (16, tm)).astype(BF16)[None]
            lim = jnp.broadcast_to(l_ref[p, i:i + 1, :], (16, tm)).astype(BF16)[None]
            term = jnp.where(rk_ref[p] < lim, e2_ref[p] * e1, jnp.zeros((), BF16))
            w = term if w is None else w + term
        parts.append(w * act[i])
    gated = jnp.concatenate(parts, axis=0).reshape(te, tm)
    acc_ref[...] += _dot(vt_ref[...], gated)

    @pl.when(j == pl.num_programs(2) - 1)
    def _():
        xf = x1_ref[0] + g2_ref[0] * acc_ref[...].T
        out_ref[0] = _rms(xf) * nfg_ref[...]


def _peer(h2t, u, vt, e1, lim, e2, rk, x1, g2, nfg):
    bz, n, d = x1.shape
    n_exp = u.shape[0]
    tm = min(PEER_TM, n)
    te = PEER_TE
    assert n % tm == 0 and n_exp % te == 0 and te % PEER_NKEYS == 0
    nt = n // tm
    tab1 = pl.BlockSpec((PEER_HEADS, te // PEER_NKEYS, tm), lambda b, i, j: (0, j, b * nt + i))
    tab2 = pl.BlockSpec((PEER_HEADS, PEER_NKEYS // 16, 16, tm), lambda b, i, j: (0, 0, 0, b * nt + i))
    tok = pl.BlockSpec((1, tm, d), lambda b, i, j: (b, i, 0))
    return pl.pallas_call(
        _peer_kernel,
        grid=(bz, nt, n_exp // te),
        in_specs=[pl.BlockSpec((d, tm), lambda b, i, j: (0, b * nt + i)),
                  pl.BlockSpec((te, d), lambda b, i, j: (j, 0)),
                  pl.BlockSpec((d, te), lambda b, i, j: (0, j)),
                  tab1, tab1, tab2, tab2, tok,
                  pl.BlockSpec((1, 1, d), lambda b, i, j: (b, 0, 0)),
                  pl.BlockSpec((1, d), lambda b, i, j: (0, 0))],
        out_specs=tok,
        out_shape=jax.ShapeDtypeStruct((bz, n, d), F32),
        scratch_shapes=[pltpu.VMEM((d, tm), F32)],
        compiler_params=_params("parallel", "parallel", "arbitrary"),
        name="peer",
    )(h2t, u, vt, e1, lim, e2, rk, x1, g2, nfg)


def _inproj_weights(w_in, conv_w, gate_b, lr_w2, alpha_b):
    widths = (ML_WIDTH, ML_WIDTH, ML_WIDTH, ML_WIDTH, N_GATES,
              GLA_KEY_WIDTH, GLA_KEY_WIDTH, GLA_WIDTH, GLA_WIDTH, 2 * GLA_RANK)
    offs = [0]
    for w in widths:
        offs.append(offs[-1] + w)
    wb = w_in.astype(BF16)
    col = lambda a, b: wb[:, offs[a]:offs[b]]
    pad_lanes = lambda a: jnp.pad(a, ((0, 0), (0, LANES - a.shape[1])))
    w2 = jnp.zeros((LANES, 2 * GLA_KEY_WIDTH), F32)
    w2 = w2.at[:GLA_RANK, :GLA_KEY_WIDTH].set(lr_w2[0])
    w2 = w2.at[GLA_RANK:2 * GLA_RANK, GLA_KEY_WIDTH:].set(lr_w2[1])
    return (col(0, 2), col(2, 3), col(3, 4), pad_lanes(col(4, 5)), col(4, 5).T,
            col(5, 7), col(7, 8), col(8, 9), pad_lanes(col(9, 10)),
            conv_w, pad_lanes(gate_b[None, :]), gate_b[:, None],
            w2.astype(BF16), alpha_b.reshape(1, 2 * GLA_KEY_WIDTH))


def _per_head_gates(gtok, gmaj):
    bz, n, _ = gtok.shape
    gt = gtok.reshape(bz, n, 2, 2, ML_HEADS).transpose(0, 4, 1, 2, 3).reshape(bz, ML_HEADS, n, 4)
    gm = gmaj.reshape(bz, 2, 2, ML_HEADS, n // ML_CHUNK, ML_CHUNK)
    gm = gm.transpose(0, 3, 4, 1, 2, 5).reshape(bz, ML_HEADS, n // ML_CHUNK, 4, ML_CHUNK)
    return gt, gm


def _token_mix_inputs(x, sh, sc, ng, wts, row_len):
    mq, mk, mv, mo, gtok, gmaj, gq, gk, gv, gr, bf, bb, mkt, gvt = _inproj(x, sh, sc, ng, wts, row_len)
    gt, gm = _per_head_gates(gtok, gmaj)
    return (mq, mk, mkt, mv, gt, gm), (gq, gk, gv, gvt, bf, bb), mo, gr


def kernel(x, c, ctx, c_ctx, w_mod, b_mod, norm1_g, w_in, ml_conv_w, ml_gate_b, ml_norm_g,
           gla_lr_w2, gla_alpha_b, gla_norm_g, w_out, norm2_g, peer_wq, peer_keys, peer_u,
           peer_v, norm_f_g):
    assert w_mod.shape[0] == 1, "single trunk layer"
    bz, n, d = x.shape
    c_all = jnp.concatenate([c, c_ctx[None, :]], axis=0)
    c_all = jnp.pad(c_all, ((0, (-c_all.shape[0]) % SUBLANES), (0, 0)))
    mod = _modulation(c_all, w_mod[0], b_mod[0][None, :])
    sh1, sc1, g1, sh2, sc2, g2 = [m[:, None, :] for m in jnp.split(mod[:bz], 6, axis=1)]
    mod_c = mod[bz]
    csh1, csc1 = mod_c[None, None, :d], mod_c[None, None, d:2 * d]

    ng1 = norm1_g[0][None, :]
    wts = _inproj_weights(w_in[0], ml_conv_w[0], ml_gate_b[0], gla_lr_w2[0], gla_alpha_b[0])
    ml_lat, gla_lat, mo, gr = _token_mix_inputs(x, sh1, sc1, ng1, wts, GRID_W)
    ml_ctx, gla_ctx, _, _ = _token_mix_inputs(ctx, csh1, csc1, ng1, wts, ctx.shape[1])
    mhf, mhb = _mlstm(*ml_lat, *ml_ctx)
    ghf, ghb = _gla(*gla_lat, *gla_ctx)

    keys = peer_keys[0].reshape(2 * PEER_HEADS, PEER_NKEYS, PEER_HALF).astype(BF16)
    x1, h2t, e1, lim, e2, rk = _mix(
        x, mhf, mhb, ghf, ghb, mo, gr, ml_norm_g[0][None, :], gla_norm_g[0][None, :],
        w_out[0].astype(BF16), g1, sh2, sc2, norm2_g[0][None, :], peer_wq[0].astype(BF16), keys)
    return _peer(h2t, peer_u[0].astype(BF16), peer_v[0].astype(BF16).T, e1, lim, e2, rk,
                 x1, g2, norm_f_g[None, :])
```

```python
import functools

import jax
import jax.numpy as jnp
from jax import lax
from jax.experimental import pallas as pl
from jax.experimental.pallas import tpu as pltpu

F32 = jnp.float32
BF16 = jnp.bfloat16

EPS = 1e-6
D_MODEL = 1024
GRID_W = 64

ML_HEADS = 4
ML_HEAD_DIM = 128
ML_WIDTH = ML_HEADS * ML_HEAD_DIM
ML_CHUNK = 128
GLA_HEADS = 4
GLA_DK = 64
GLA_DV = 128
GLA_KEY_WIDTH = GLA_HEADS * GLA_DK
GLA_WIDTH = GLA_HEADS * GLA_DV
GLA_RANK = 16
GLA_TAU = 16.0
GLA_CHUNK = 64
N_GATES = 4 * ML_HEADS

PEER_HEADS = 8
PEER_NKEYS = 128
PEER_TOPK = 16
PEER_HALF = 128

LANES = 128
SUBLANES = 8
BF16_ROWS = 2 * SUBLANES
VMEM_LIMIT = 56 * 1024 * 1024

INPROJ_TM = 512
CUMSUM_ROWS = 256
ML_HEADS_PER_STEP = 4
GLA_HEADS_PER_STEP = 4
MIX_TM = 512
PEER_TM = 512
PEER_TE = 2048

NEG_INF = float("-inf")


def _params(*sem):
    return pltpu.CompilerParams(dimension_semantics=sem, vmem_limit_bytes=VMEM_LIMIT)


def _dot(a, b):
    return jnp.dot(a, b, preferred_element_type=F32)


def _dot_nt(a, b):
    return lax.dot_general(a, b, (((1,), (1,)), ((), ())), preferred_element_type=F32)


def _split2(x):
    hi = x.astype(BF16)
    return hi, (x - hi.astype(F32)).astype(BF16)


def _dot_exact_rhs(a01, x):
    hi, lo = _split2(x)
    return _dot(a01, hi) + _dot(a01, lo)


def _dot_exact_lhs(x, a01):
    hi, lo = _split2(x)
    return _dot(hi, a01) + _dot(lo, a01)


def _block_rows(mask, x):
    r = mask.shape[0]
    return jnp.concatenate([_dot_exact_rhs(mask, x[i:i + r]) for i in range(0, x.shape[0], r)], axis=0)


def _block_cols(x, mask):
    r = mask.shape[0]
    return jnp.concatenate([_dot_exact_lhs(x[:, i:i + r], mask) for i in range(0, x.shape[1], r)], axis=1)


def _sigmoid(x):
    return 1.0 / (1.0 + jnp.exp(-x))


def _log_sigmoid(x):
    return jnp.minimum(x, 0.0) - jnp.log(1.0 + jnp.exp(-jnp.abs(x)))


def _rms(x):
    return x * lax.rsqrt(jnp.mean(x * x, axis=-1, keepdims=True) + EPS)


def _mod_kernel(c_ref, w_ref, b_ref, o_ref):
    cond = c_ref[...]
    cond = cond * _sigmoid(cond)
    ch, cl = cond.astype(BF16), (cond - cond.astype(BF16).astype(F32)).astype(BF16)
    w = w_ref[...]
    wh = w.astype(BF16)
    wl = (w - wh.astype(F32)).astype(BF16)
    o_ref[...] = _dot(ch, wh) + _dot(ch, wl) + _dot(cl, wh) + b_ref[...]


def _modulation(c_all, w_mod, b_mod):
    rows, d = c_all.shape
    n_out = w_mod.shape[1]
    tn = 512
    return pl.pallas_call(
        _mod_kernel,
        grid=(n_out // tn,),
        in_specs=[pl.BlockSpec((rows, d), lambda j: (0, 0)),
                  pl.BlockSpec((d, tn), lambda j: (0, j)),
                  pl.BlockSpec((1, tn), lambda j: (0, j))],
        out_specs=pl.BlockSpec((rows, tn), lambda j: (0, j)),
        out_shape=jax.ShapeDtypeStruct((rows, n_out), F32),
        compiler_params=_params("arbitrary"),
        name="mod",
    )(c_all, w_mod, b_mod)


def _chunk_masks(tm, chunk):
    r = lax.broadcasted_iota(jnp.int32, (tm, tm), 0)
    c = lax.broadcasted_iota(jnp.int32, (tm, tm), 1)
    same = (r // chunk) == (c // chunk)
    prefix = jnp.where(same & (c <= r), 1.0, 0.0).astype(BF16)
    suffix = jnp.where(same & (c >= r), 1.0, 0.0).astype(BF16)
    return prefix, suffix


def _inproj_kernel(x_ref, sh_ref, sc_ref, ng_ref, wqk_ref, wv_ref, wo_ref, wg_ref, wgt_ref,
                   wgqk_ref, wgv_ref, wgr_ref, wlr_ref, conv_ref, gb_ref, gbc_ref, w2_ref, ab_ref,
                   mq_ref, mk_ref, mv_ref, mo_ref, gtok_ref, gmaj_ref,
                   gq_ref, gk_ref, gv_ref, gr_ref, bf_ref, bb_ref, mkt_ref, gvt_ref, *, row_len):
    tm = x_ref.shape[1]
    x = x_ref[0]
    h = _rms(x) * ng_ref[...] * (1.0 + sc_ref[0]) + sh_ref[0]
    hb = h.astype(BF16)

    u = _dot(hb, wqk_ref[...])
    pos = lax.broadcasted_iota(jnp.int32, (tm, 1), 0) % row_len
    up = jnp.where(pos == 0, 0.0, pltpu.roll(u, 1, 0))
    dn = jnp.where(pos == row_len - 1, 0.0, pltpu.roll(u, tm - 1, 0))
    cw = conv_ref[...]
    y = cw[0:1] * up + cw[1:2] * u + cw[2:3] * dn
    y = y * _sigmoid(y)
    mq_ref[0] = y[:, :ML_WIDTH].astype(BF16)
    mk = y[:, ML_WIDTH:] * ML_HEAD_DIM ** -0.5
    mk_ref[0] = mk.astype(BF16)
    mkt = mk.T.astype(BF16)
    for hh in range(ML_HEADS):
        for ci in range(tm // ML_CHUNK):
            mkt_ref[0, hh, ci] = mkt[hh * ML_HEAD_DIM:(hh + 1) * ML_HEAD_DIM, ci * ML_CHUNK:(ci + 1) * ML_CHUNK]
    mv_ref[0] = _dot(hb, wv_ref[...]).astype(BF16)
    mo_ref[0] = _dot(hb, wo_ref[...]).astype(BF16)

    pre128, suf128 = _chunk_masks(min(tm, CUMSUM_ROWS), ML_CHUNK)
    g = _dot(hb, wg_ref[...]) + gb_ref[...]
    col = lax.broadcasted_iota(jnp.int32, (1, LANES), 1)
    is_f = ((col % 8) >= 4) & (col < N_GATES)
    is_bwd = col >= 8
    lf = jnp.where(is_f, _log_sigmoid(g), 0.0)
    cum = jnp.where(is_bwd, _block_rows(suf128, lf), _block_rows(pre128, lf))
    gtok_ref[0] = jnp.where(is_f, cum, g)[:, :N_GATES]

    gt = _dot_nt(wgt_ref[...], hb) + gbc_ref[...]
    row = lax.broadcasted_iota(jnp.int32, (N_GATES, 1), 0)
    is_f_r = (row % 8) >= 4
    lft = jnp.where(is_f_r, _log_sigmoid(gt), 0.0)
    cum_t = jnp.where(row >= 8, _block_cols(lft, pre128), _block_cols(lft, suf128))
    gmaj_ref[0] = jnp.where(is_f_r, cum_t, gt)

    gqk = _dot(hb, wgqk_ref[...])
    gq_ref[0] = (gqk[:, :GLA_KEY_WIDTH] * GLA_DK ** -0.5).astype(BF16)
    gk_ref[0] = gqk[:, GLA_KEY_WIDTH:].astype(BF16)
    gv = _dot(hb, wgv_ref[...])
    gv_ref[0] = gv.astype(BF16)
    gvt = gv.T.astype(BF16)
    for hh in range(GLA_HEADS):
        for ci in range(tm // GLA_CHUNK):
            gvt_ref[0, hh, ci] = gvt[hh * GLA_DV:(hh + 1) * GLA_DV, ci * GLA_CHUNK:(ci + 1) * GLA_CHUNK]
    gr_ref[0] = _dot(hb, wgr_ref[...]).astype(BF16)
    glr = _dot(hb, wlr_ref[...]).astype(BF16)
    alpha = _dot(glr, w2_ref[...]) + ab_ref[...]
    la = _log_sigmoid(alpha) * (1.0 / GLA_TAU)
    pre64, suf64 = _chunk_masks(min(tm, CUMSUM_ROWS), GLA_CHUNK)
    bf_ref[0] = _block_rows(pre64, la[:, :GLA_KEY_WIDTH])
    bb_ref[0] = _block_rows(suf64, la[:, GLA_KEY_WIDTH:])


def _inproj(x, sh, sc, ng, wts, row_len):
    bz, n, d = x.shape
    tm = min(INPROJ_TM, n)
    assert n % tm == 0 and tm % row_len == 0 and tm % ML_CHUNK == 0
    grid = (bz, n // tm)
    per_b = (lambda b, i: (b, 0, 0)) if sh.shape[0] == bz else (lambda b, i: (0, 0, 0))

    def full(a):
        return pl.BlockSpec(a.shape, lambda b, i, nd=a.ndim: (0,) * nd)

    def tok(width):
        return pl.BlockSpec((1, tm, width), lambda b, i: (b, i, 0))

    def out(width, dtype):
        return jax.ShapeDtypeStruct((bz, n, width), dtype)

    in_specs = [tok(d), pl.BlockSpec((1, 1, d), per_b), pl.BlockSpec((1, 1, d), per_b), full(ng)]
    in_specs += [full(w) for w in wts]
    out_specs = [tok(ML_WIDTH)] * 4 + [tok(N_GATES), pl.BlockSpec((1, N_GATES, tm), lambda b, i: (b, 0, i))]
    out_specs += [tok(GLA_KEY_WIDTH)] * 2 + [tok(GLA_WIDTH)] * 2 + [tok(GLA_KEY_WIDTH)] * 2
    out_shape = [out(ML_WIDTH, BF16)] * 4 + [out(N_GATES, F32),
                                              jax.ShapeDtypeStruct((bz, N_GATES, n), F32)]
    out_shape += ([out(GLA_KEY_WIDTH, BF16)] * 2 + [out(GLA_WIDTH, BF16)] * 2
                  + [out(GLA_KEY_WIDTH, F32)] * 2)
    for heads, dim, chunk in ((ML_HEADS, ML_HEAD_DIM, ML_CHUNK), (GLA_HEADS, GLA_DV, GLA_CHUNK)):
        out_specs.append(pl.BlockSpec((1, heads, tm // chunk, dim, chunk), lambda b, i: (b, 0, i, 0, 0)))
        out_shape.append(jax.ShapeDtypeStruct((bz, heads, n // chunk, dim, chunk), BF16))
    return pl.pallas_call(
        functools.partial(_inproj_kernel, row_len=row_len),
        grid=grid, in_specs=in_specs, out_specs=out_specs, out_shape=out_shape,
        compiler_params=_params("parallel", "arbitrary"),
        name="inproj",
    )(x, sh, sc, ng, *wts)


def _bmm(a, b):
    return lax.dot_general(a, b, (((2,), (1,)), ((0,), (0,))), preferred_element_type=F32)


def _bmm_nt(a, b):
    return lax.dot_general(a, b, (((2,), (2,)), ((0,), (0,))), preferred_element_type=F32)


def _by_direction(x, fwd, bwd):
    half = x.shape[0] // 2
    return jnp.concatenate([fwd(x[:half]), bwd(x[half:])], axis=0)


def _ml_chunks(q, k, kt, v, gt, gm, masks, state, want_out):
    ct, nv, m = state
    ll = q.shape[1]
    ig_col = _by_direction(gt, lambda g: g[:, :, 0:1], lambda g: g[:, :, 2:3])
    b_col = _by_direction(gt, lambda g: g[:, :, 1:2], lambda g: g[:, :, 3:4])
    ig_row = _by_direction(gm, lambda g: g[:, 0:1, :], lambda g: g[:, 2:3, :])
    b_row = _by_direction(gm, lambda g: g[:, 1:2, :], lambda g: g[:, 3:4, :])
    b_end = _by_direction(b_row, lambda b: b[:, :, ll - 1:ll], lambda b: b[:, :, 0:1])
    h = None
    if want_out:
        d_log = _by_direction(b_col - b_row + ig_row,
                              lambda x: jnp.where(masks[0], x, NEG_INF),
                              lambda x: jnp.where(masks[1], x, NEG_INF))
        inter_log = b_col + m
        m_t = jnp.maximum(inter_log, jnp.max(d_log, axis=2, keepdims=True))
        scores = _bmm_nt(q, k) * jnp.exp(d_log - m_t)
        inter = jnp.exp(inter_log - m_t)
        qf = q.astype(F32)
        num = _bmm(scores.astype(BF16), v) + inter * _bmm(q, ct.astype(BF16))
        den = jnp.sum(scores, axis=2, keepdims=True) + inter * jnp.sum(qf * nv, axis=2, keepdims=True)
        h = num / jnp.maximum(jnp.abs(den), jnp.exp(-m_t))
    m_loc = jnp.max(b_end - b_row + ig_row, axis=2, keepdims=True)
    w = jnp.broadcast_to(jnp.exp(b_end - b_col + ig_col - m_loc), v.shape)
    c_loc = _bmm(kt, (w * v.astype(F32)).astype(BF16))
    n_loc = jnp.sum(w * k.astype(F32), axis=1, keepdims=True)
    m_new = jnp.maximum(b_end + m, m_loc)
    a = jnp.exp(b_end + m - m_new)
    bb = jnp.exp(m_loc - m_new)
    return h, (a * ct + bb * c_loc, a * nv + bb * n_loc, m_new)


def _mlstm_kernel(q_ref, k_ref, kt_ref, v_ref, gt_ref, gm_ref,
                  qc_ref, kc_ref, ktc_ref, vc_ref, gtc_ref, gmc_ref, hf_ref, hb_ref):
    ll = ML_CHUNK
    nc = q_ref.shape[1] // ll
    ncc = qc_ref.shape[1] // ll
    r = lax.broadcasted_iota(jnp.int32, (ll, ll), 0)
    c = lax.broadcasted_iota(jnp.int32, (ll, ll), 1)
    masks = (c <= r, c >= r)

    heads = gt_ref.shape[1]
    head_cols = [slice(hh * ML_HEAD_DIM, (hh + 1) * ML_HEAD_DIM) for hh in range(heads)]

    def rows_of(ci):
        return pl.ds(pl.multiple_of(ci * ll, ll), ll)

    def load(refs, cf, cb):
        qr, kr, ktr, vr, gtr, gmr = refs
        sites = [(hh, ci) for ci in (cf, cb) for hh in range(heads)]
        seq = lambda r: jnp.stack([r[0, rows_of(ci), head_cols[hh]] for hh, ci in sites])
        return (seq(qr), seq(kr), jnp.stack([ktr[0, hh, ci] for hh, ci in sites]), seq(vr),
                jnp.stack([gtr[0, hh, rows_of(ci), :] for hh, ci in sites]),
                jnp.stack([gmr[0, hh, ci] for hh, ci in sites]))

    chains = 2 * heads
    zero = (jnp.zeros((chains, ML_HEAD_DIM, ML_HEAD_DIM), F32), jnp.zeros((chains, 1, ML_HEAD_DIM), F32),
            jnp.zeros((chains, 1, 1), F32))
    ctx_refs = (qc_ref, kc_ref, ktc_ref, vc_ref, gtc_ref, gmc_ref)
    lat_refs = (q_ref, k_ref, kt_ref, v_ref, gt_ref, gm_ref)

    def ctx_body(i, state):
        _, state = _ml_chunks(*load(ctx_refs, i, ncc - 1 - i), masks, state, False)
        return state

    def lat_body(i, state):
        cb = nc - 1 - i
        h, state = _ml_chunks(*load(lat_refs, i, cb), masks, state, True)
        for hh in range(heads):
            hf_ref[0, rows_of(i), head_cols[hh]] = h[hh]
            hb_ref[0, rows_of(cb), head_cols[hh]] = h[heads + hh]
        return state

    state = lax.fori_loop(0, ncc, ctx_body, zero)
    lax.fori_loop(0, nc, lat_body, state)


def _mlstm(q, k, kt, v, gt, gm, qc, kc, ktc, vc, gtc, gmc):
    bz, n, _ = q.shape
    ncx = qc.shape[1]
    hps = ML_HEADS_PER_STEP

    def seq(nn):
        return pl.BlockSpec((1, nn, hps * ML_HEAD_DIM), lambda b, h: (b, 0, h))

    def ktr(nn):
        return pl.BlockSpec((1, hps, nn // ML_CHUNK, ML_HEAD_DIM, ML_CHUNK), lambda b, h: (b, h, 0, 0, 0))

    def gtok(nn):
        return pl.BlockSpec((1, hps, nn, 4), lambda b, h: (b, h, 0, 0))

    def gmaj(nn):
        return pl.BlockSpec((1, hps, nn // ML_CHUNK, 4, ML_CHUNK), lambda b, h: (b, h, 0, 0, 0))

    def specs(nn):
        return [seq(nn), seq(nn), ktr(nn), seq(nn), gtok(nn), gmaj(nn)]

    return pl.pallas_call(
        _mlstm_kernel,
        grid=(bz, ML_HEADS // hps),
        in_specs=specs(n) + specs(ncx),
        out_specs=[seq(n)] * 2,
        out_shape=[jax.ShapeDtypeStruct((bz, n, ML_WIDTH), F32)] * 2,
        compiler_params=_params("parallel", "arbitrary"),
        name="mlstm",
    )(q, k, kt, v, gt, gm, qc, kc, ktc, vc, gtc, gmc)


def _gla_chunks(q, k, v, vt, b, masks, st, want_out):
    ll = q.shape[1]
    qf, kf = q.astype(F32), k.astype(F32)
    ref = _by_direction(b, lambda x: x[:, ll // 2:ll // 2 + 1], lambda x: x[:, ll // 2 - 1:ll // 2])
    b_end = _by_direction(b, lambda x: x[:, ll - 1:ll], lambda x: x[:, 0:1])
    o = None
    if want_out:
        att = _bmm_nt((qf * jnp.exp(b - ref)).astype(BF16), (kf * jnp.exp(ref - b)).astype(BF16))
        att = _by_direction(att, lambda x: jnp.where(masks[0], x, 0.0), lambda x: jnp.where(masks[1], x, 0.0))
        o = _bmm(att.astype(BF16), v) + _bmm_nt((qf * jnp.exp(b)).astype(BF16), st.astype(BF16))
    s_loc = _bmm(vt, (kf * jnp.exp(b_end - b)).astype(BF16))
    return o, jnp.exp(b_end) * st + s_loc


def _gla_kernel(q_ref, k_ref, v_ref, vt_ref, bf_ref, bb_ref,
                qc_ref, kc_ref, vc_ref, vtc_ref, bfc_ref, bbc_ref, of_ref, ob_ref):
    ll = GLA_CHUNK
    nc = q_ref.shape[1] // ll
    ncc = qc_ref.shape[1] // ll
    r = lax.broadcasted_iota(jnp.int32, (ll, ll), 0)
    c = lax.broadcasted_iota(jnp.int32, (ll, ll), 1)
    masks = (c <= r, c >= r)

    heads = vt_ref.shape[1]

    def rows_of(ci):
        return pl.ds(pl.multiple_of(ci * ll, ll), ll)

    head_cols = [slice(hh * GLA_DV, (hh + 1) * GLA_DV) for hh in range(heads)]
    key_cols = [slice(hh * GLA_DK, (hh + 1) * GLA_DK) for hh in range(heads)]

    def load(refs, cf, cb):
        qr, kr, vr, vtr, bfr, bbr = refs
        sites = [(hh, ci) for ci in (cf, cb) for hh in range(heads)]
        per_head = lambda r: jnp.stack([r[0, rows_of(ci), key_cols[hh]] for hh, ci in sites])
        return (per_head(qr), per_head(kr),
                jnp.stack([vr[0, rows_of(ci), head_cols[hh]] for hh, ci in sites]),
                jnp.stack([vtr[0, hh, ci] for hh, ci in sites]),
                jnp.stack([bfr[0, rows_of(cf), key_cols[hh]] for hh in range(heads)]
                          + [bbr[0, rows_of(cb), key_cols[hh]] for hh in range(heads)]))

    zero = jnp.zeros((2 * heads, GLA_DV, GLA_DK), F32)
    ctx_refs = (qc_ref, kc_ref, vc_ref, vtc_ref, bfc_ref, bbc_ref)
    lat_refs = (q_ref, k_ref, v_ref, vt_ref, bf_ref, bb_ref)

    def ctx_body(i, st):
        _, st = _gla_chunks(*load(ctx_refs, i, ncc - 1 - i), masks, st, False)
        return st

    def lat_body(i, st):
        cb = nc - 1 - i
        o, st = _gla_chunks(*load(lat_refs, i, cb), masks, st, True)
        for hh in range(heads):
            of_ref[0, rows_of(i), head_cols[hh]] = o[hh]
            ob_ref[0, rows_of(cb), head_cols[hh]] = o[heads + hh]
        return st

    st = lax.fori_loop(0, ncc, ctx_body, zero)
    lax.fori_loop(0, nc, lat_body, st, unroll=2)


def _gla(q, k, v, vt, bf, bb, qc, kc, vc, vtc, bfc, bbc):
    bz, n, _ = q.shape
    ncx = qc.shape[1]
    hps = GLA_HEADS_PER_STEP

    def key(nn):
        return pl.BlockSpec((1, nn, hps * GLA_DK), lambda b, h: (b, 0, h))

    def val(nn):
        return pl.BlockSpec((1, nn, hps * GLA_DV), lambda b, h: (b, 0, h))

    def valt(nn):
        return pl.BlockSpec((1, hps, nn // GLA_CHUNK, GLA_DV, GLA_CHUNK), lambda b, h: (b, h, 0, 0, 0))

    def specs(nn):
        return [key(nn), key(nn), val(nn), valt(nn), key(nn), key(nn)]

    return pl.pallas_call(
        _gla_kernel,
        grid=(bz, GLA_HEADS // hps),
        in_specs=specs(n) + specs(ncx),
        out_specs=[val(n)] * 2,
        out_shape=[jax.ShapeDtypeStruct((bz, n, GLA_WIDTH), F32)] * 2,
        compiler_params=_params("parallel", "arbitrary"),
        name="gla",
    )(q, k, v, vt, bf, bb, qc, kc, vc, vtc, bfc, bbc)


def _head_rms(h, g, width):
    parts = [_rms(h[:, i:i + width]) for i in range(0, h.shape[1], width)]
    return jnp.concatenate(parts, axis=1) * g


def _oddeven_merge(lo, hi, r):
    step = r * 2
    if step < hi - lo:
        yield from _oddeven_merge(lo, hi, step)
        yield from _oddeven_merge(lo + r, hi, step)
        yield from [(i, i + r) for i in range(lo + r, hi - r, step)]
    else:
        yield (lo, lo + r)


def _oddeven_merge_sort(lo, hi):
    if hi - lo >= 1:
        mid = lo + (hi - lo) // 2
        yield from _oddeven_merge_sort(lo, mid)
        yield from _oddeven_merge_sort(mid + 1, hi)
        yield from _oddeven_merge(lo, hi, 1)


def _exchange(v, i, j):
    v[i], v[j] = jnp.maximum(v[i], v[j]), jnp.minimum(v[i], v[j])


def _merge_top16(v, shift):
    other = [pltpu.roll(x, shift, 0) for x in v]
    out = []
    for j in range(PEER_TOPK):
        a = v[j] if j < len(v) else None
        b = other[PEER_TOPK - 1 - j] if PEER_TOPK - 1 - j < len(v) else None
        out.append(a if b is None else b if a is None else jnp.maximum(a, b))
    return out


def _sort_bitonic16(v):
    v = list(v)
    for d in (8, 4, 2, 1):
        for j in range(PEER_TOPK):
            if not j & d:
                _exchange(v, j, j + d)
    return v


def _sixteenth_largest(blocks):
    v = list(blocks)
    assert SUBLANES <= len(v) <= PEER_TOPK
    for i, j in _oddeven_merge_sort(0, PEER_TOPK - 1):
        if j < len(v):
            _exchange(v, i, j)
    v = _sort_bitonic16(_merge_top16(v, 4))
    v = _sort_bitonic16(_merge_top16(v, 2))
    v = _merge_top16(v, 1)
    return functools.reduce(jnp.minimum, v)[0:1]


def _sorted_top16(tiles):
    v = list(tiles)
    n = len(v)
    for i, j in _oddeven_merge_sort(0, n - 1):
        _exchange(v, i, j)
    for shift in (4, 2, 1):
        v = _sort_bitonic16(_merge_top16(v, shift))
    sub = lax.broadcasted_iota(jnp.int32, v[0].shape, 0)
    halves = []
    for lo in (0, SUBLANES):
        rows = v[lo]
        for r in range(1, SUBLANES):
            rows = jnp.where(sub == r, v[lo + r], rows)
        halves.append(rows)
    repeated = jnp.zeros_like(v[0])
    for r in range(n - 1):
        repeated = jnp.where(v[r] == v[r + 1], 1.0, repeated)
    return jnp.concatenate(halves, axis=0), repeated[0:1]


def _prefix_count(x, thr, test):
    r = [thr[i:i + 1] for i in range(PEER_TOPK)]
    sel = jnp.where
    t8 = test(x, r[7])
    t4 = test(x, sel(t8, r[11], r[3]))
    t2 = test(x, sel(t8, sel(t4, r[13], r[9]), sel(t4, r[5], r[1])))
    t1 = test(x, sel(t8, sel(t4, sel(t2, r[14], r[12]), sel(t2, r[10], r[8])),
                     sel(t4, sel(t2, r[6], r[4]), sel(t2, r[2], r[0]))))
    count = sel(t8, 8.0, 0.0) + sel(t4, 4.0, 0.0) + sel(t2, 2.0, 0.0) + sel(t1, 1.0, 0.0)
    return sel(test(x, r[15]), float(PEER_TOPK), count)


def _count(mask):
    return jnp.sum(jnp.where(mask, 1.0, 0.0), axis=0, keepdims=True)


def _top16(s):
    nk, tl = s.shape
    slot = lax.broadcasted_iota(jnp.int32, (PEER_TOPK, tl), 0)
    key_id = lax.broadcasted_iota(jnp.int32, (nk, tl), 0).astype(F32)

    def body(r, carry):
        s, rank, vals = carry
        m = jnp.max(s, axis=0, keepdims=True)
        first = jnp.min(jnp.where(s == m, key_id, float(nk)), axis=0, keepdims=True)
        sel = key_id == first
        rank = jnp.where(sel, jnp.asarray(r).astype(F32), rank)
        s = jnp.where(sel, NEG_INF, s)
        vals = jnp.where(slot == r, m, vals)
        return s, rank, vals

    init = (s, jnp.full((nk, tl), float(PEER_TOPK), F32), jnp.zeros((PEER_TOPK, tl), F32))
    _, rank, vals = lax.fori_loop(0, PEER_TOPK, body, init)
    return vals, rank


def _pair_candidates(v1, v2):
    blocks = [v1[0:1] + v2[0:8], v1[0:1] + v2[8:16]]
    blocks += [v1[a:a + 1] + v2[0:8] for a in range(1, 8)]
    blocks += [v1[8:16] + v2[0:1]]
    return jnp.concatenate(blocks, axis=0)


def _blocks(cand):
    return [cand[i:i + SUBLANES] for i in range(0, cand.shape[0], SUBLANES)]


def _pick16(cand):
    i = lax.broadcasted_iota(jnp.int32, cand.shape, 0)
    blk, rr = i // 8, i % 8
    flat = jnp.where(blk == 0, rr, jnp.where(blk == 1, 8 + rr,
                     jnp.where(blk <= 8, (blk - 1) * 16 + rr, (8 + rr) * 16))).astype(F32)

    def body(_, carry):
        cand, picked = carry
        m = jnp.max(cand, axis=0, keepdims=True)
        first = jnp.min(jnp.where(cand == m, flat, 1e9), axis=0, keepdims=True)
        sel = flat == first
        return jnp.where(sel, NEG_INF, cand), jnp.where(sel, 1.0, picked)

    _, picked = lax.fori_loop(0, PEER_TOPK, body, (cand, jnp.zeros_like(cand)))
    return picked


def _first_key_thresholds(picked, v1):
    inf = jnp.inf
    lo = jnp.where(picked[0:8] > 0.0, v1[0:1], inf)
    for a in range(1, 8):
        lo = jnp.minimum(lo, jnp.where(picked[8 * (a + 1):8 * (a + 2)] > 0.0, v1[a:a + 1], inf))
    tail = jnp.min(jnp.where(picked[72:80] > 0.0, v1[8:16], inf), axis=0, keepdims=True)
    row = lax.broadcasted_iota(jnp.int32, lo.shape, 0)
    lo = jnp.minimum(lo, jnp.where(row == 0, tail, inf))
    hi = jnp.where(picked[8:16] > 0.0, v1[0:1], inf)
    return jnp.concatenate([lo, hi], axis=0)


def _pair_counts(cand, picked):
    z = jnp.sum(picked * jnp.exp(cand - cand[0:1]), axis=0, keepdims=True)
    n_rows = [jnp.sum(picked[0:16], axis=0, keepdims=True)]
    n_rows += [jnp.sum(picked[8 * (a + 1):8 * (a + 2)], axis=0, keepdims=True) for a in range(1, 8)]
    return jnp.concatenate(n_rows + [picked[72:80]], axis=0), z


def _mix_kernel(x_ref, mhf_ref, mhb_ref, ghf_ref, ghb_ref, mo_ref, gr_ref, mlg_ref, glag_ref,
                wout_ref, g1_ref, sh2_ref, sc2_ref, n2g_ref, wq_ref, keys_ref,
                x1_ref, h2t_ref, e1_ref, l_ref, e2_ref, rk_ref, q_scr, s_scr):
    ml = _head_rms(mhf_ref[0] + mhb_ref[0], mlg_ref[...], ML_HEAD_DIM)
    ml = _sigmoid(mo_ref[0].astype(F32)) * ml
    gl = _head_rms(ghf_ref[0] + ghb_ref[0], glag_ref[...], GLA_DV)
    gr = gr_ref[0].astype(F32)
    gl = gr * _sigmoid(gr) * gl
    mix = jnp.concatenate([ml, gl], axis=1).astype(BF16)
    x1 = x_ref[0] + g1_ref[0] * _dot(mix, wout_ref[...])
    x1_ref[0] = x1
    h2 = _rms(x1) * n2g_ref[...] * (1.0 + sc2_ref[0]) + sh2_ref[0]
    h2t_ref[...] = h2.T.astype(BF16)
    qall = _dot(h2.astype(BF16), wq_ref[...]).astype(BF16)
    for j in range(2 * PEER_HEADS):
        q_scr[j] = qall[:, j * PEER_HALF:(j + 1) * PEER_HALF]

    tm = x_ref.shape[1]
    topk = float(PEER_TOPK)
    lane_chunks = [slice(i, i + LANES) for i in range(0, tm, LANES)]

    def write_tables(p, lanes, s1, s2, v1, v2, z, lim, rank2):
        e1_ref[p, :, lanes] = jnp.exp(s1 - v1[0:1]) * (1.0 / z)
        l_ref[p, :, lanes] = lim
        e2 = jnp.exp(s2 - v2[0:1]).astype(BF16)
        e2_ref[p, :, :, lanes] = e2.reshape(PEER_NKEYS // BF16_ROWS, BF16_ROWS, LANES)
        rk_ref[p, :, :, lanes] = rank2.astype(BF16).reshape(PEER_NKEYS // BF16_ROWS, BF16_ROWS, LANES)

    def head_body(p, carry):
        for hf in range(2):
            s_scr[hf] = _dot_nt(keys_ref[2 * p + hf], q_scr[2 * p + hf])
        excess = jnp.zeros((1, LANES), F32)
        vals = []
        for hf in range(2):
            for lanes in lane_chunks:
                tiles = [s_scr[hf, i:i + SUBLANES, lanes] for i in range(0, PEER_NKEYS, SUBLANES)]
                top, repeated = _sorted_top16(tiles)
                vals.append(top)
                excess = jnp.maximum(excess, repeated)
        v1s, v2s = vals[:len(lane_chunks)], vals[len(lane_chunks):]
        cands = [_pair_candidates(v1, v2) for v1, v2 in zip(v1s, v2s)]
        taus = [_sixteenth_largest(_blocks(c)) for c in cands]
        for lanes, v1, v2, cand, tau in zip(lane_chunks, v1s, v2s, cands, taus):
            s1, s2 = s_scr[0, :, lanes], s_scr[1, :, lanes]
            picked = jnp.where(cand >= tau, 1.0, 0.0)
            _, z = _pair_counts(cand, picked)
            theta = _first_key_thresholds(picked, v1)
            lim = _prefix_count(s1, theta, lambda x, t: x >= t)
            rank2 = _prefix_count(s2, v2, lambda x, t: x < t)
            write_tables(p, lanes, s1, s2, v1, v2, z, lim, rank2)
            n_marked = (_count(s1 >= v1[PEER_TOPK - 1:PEER_TOPK]) + _count(rank2 < topk)
                        + jnp.sum(picked, axis=0, keepdims=True))
            excess = jnp.maximum(excess, n_marked - 3.0 * topk)

        @pl.when(jnp.max(excess) > 0.0)
        def _():
            for lanes in lane_chunks:
                s1, s2 = s_scr[0, :, lanes], s_scr[1, :, lanes]
                v1, rank1 = _top16(s1)
                v2, rank2 = _top16(s2)
                cand = _pair_candidates(v1, v2)
                n, z = _pair_counts(cand, _pick16(cand))
                lim = jnp.zeros_like(rank1)
                for a in range(PEER_TOPK):
                    lim = jnp.where(rank1 == float(a), n[a:a + 1], lim)
                write_tables(p, lanes, s1, s2, v1, v2, z, lim, rank2)

        return carry

    lax.fori_loop(0, PEER_HEADS, head_body, 0)


def _mix(x, mhf, mhb, ghf, ghb, mo, gr, mlg, glag, wout, g1, sh2, sc2, n2g, wq, keys):
    bz, n, d = x.shape
    tm = min(MIX_TM, n)
    assert n % tm == 0 and tm % LANES == 0
    nt = n // tm
    t_all = bz * n

    def tok(width):
        return pl.BlockSpec((1, tm, width), lambda b, i: (b, i, 0))

    def full(a):
        return pl.BlockSpec(a.shape, lambda b, i, nd=a.ndim: (0,) * nd)

    vec = pl.BlockSpec((1, 1, d), lambda b, i: (b, 0, 0))
    tab1 = pl.BlockSpec((PEER_HEADS, PEER_NKEYS, tm), lambda b, i: (0, 0, b * nt + i))
    tab2 = pl.BlockSpec((PEER_HEADS, PEER_NKEYS // BF16_ROWS, BF16_ROWS, tm),
                        lambda b, i: (0, 0, 0, b * nt + i))
    tab1_shape = jax.ShapeDtypeStruct((PEER_HEADS, PEER_NKEYS, t_all), F32)
    tab2_shape = jax.ShapeDtypeStruct((PEER_HEADS, PEER_NKEYS // BF16_ROWS, BF16_ROWS, t_all), BF16)
    return pl.pallas_call(
        _mix_kernel,
        grid=(bz, nt),
        in_specs=[tok(d)] + [tok(ML_WIDTH)] * 6 + [full(mlg), full(glag), full(wout), vec, vec, vec,
                                                    full(n2g), full(wq), full(keys)],
        out_specs=[tok(d), pl.BlockSpec((d, tm), lambda b, i: (0, b * nt + i)), tab1, tab1, tab2, tab2],
        out_shape=[jax.ShapeDtypeStruct((bz, n, d), F32), jax.ShapeDtypeStruct((d, t_all), BF16),
                   tab1_shape, tab1_shape, tab2_shape, tab2_shape],
        scratch_shapes=[pltpu.VMEM((2 * PEER_HEADS, tm, PEER_HALF), BF16),
                        pltpu.VMEM((2, PEER_NKEYS, tm), F32)],
        compiler_params=_params("parallel", "arbitrary"),
        name="mix",
    )(x, mhf, mhb, ghf, ghb, mo, gr, mlg, glag, wout, g1, sh2, sc2, n2g, wq, keys)


def _peer_kernel(h2t_ref, u_ref, vt_ref, e1_ref, l_ref, e2_ref, rk_ref, x1_ref, g2_ref, nfg_ref,
                 out_ref, acc_ref):
    j = pl.program_id(2)
    te, tm = u_ref.shape[0], h2t_ref.shape[1]
    rows_per_step = te // PEER_NKEYS
    tiles = PEER_NKEYS // BF16_ROWS

    @pl.when(j == 0)
    def _():
        acc_ref[...] = jnp.zeros_like(acc_ref)

    act = _dot(u_ref[...], h2t_ref[...])
    act = act.astype(BF16)
    act = (0.5 * act * (1.0 + lax.erf(act * 2.0 ** -0.5))).reshape(rows_per_step, tiles, BF16_ROWS, tm)
    parts = []
    for i in range(rows_per_step):
        w = None
        for p in range(PEER_HEADS):
            e1 = jnp.broadcast_to(e1_ref[p, i:i + 1, :], (BF16_ROWS, tm)).astype(BF16)[None]
            lim = jnp.broadcast_to(l_ref[p, i:i + 1, :], (BF16_ROWS, tm)).astype(BF16)[None]
            term = jnp.where(rk_ref[p] < lim, e2_ref[p] * e1, jnp.zeros((), BF16))
            w = term if w is None else w + term
        parts.append(w * act[i])
    gated = jnp.concatenate(parts, axis=0).reshape(te, tm)
    acc_ref[...] += _dot(vt_ref[...], gated)

    @pl.when(j == pl.num_programs(2) - 1)
    def _():
        xf = x1_ref[0] + g2_ref[0] * acc_ref[...].T
        out_ref[0] = _rms(xf) * nfg_ref[...]


def _peer(h2t, u, vt, e1, lim, e2, rk, x1, g2, nfg):
    bz, n, d = x1.shape
    n_exp = u.shape[0]
    tm = min(PEER_TM, n)
    te = PEER_TE
    assert n % tm == 0 and n_exp % te == 0 and te % PEER_NKEYS == 0
    nt = n // tm
    tab1 = pl.BlockSpec((PEER_HEADS, te // PEER_NKEYS, tm), lambda b, i, j: (0, j, b * nt + i))
    tab2 = pl.BlockSpec((PEER_HEADS, PEER_NKEYS // BF16_ROWS, BF16_ROWS, tm),
                        lambda b, i, j: (0, 0, 0, b * nt + i))
    tok = pl.BlockSpec((1, tm, d), lambda b, i, j: (b, i, 0))
    return pl.pallas_call(
        _peer_kernel,
        grid=(bz, nt, n_exp // te),
        in_specs=[pl.BlockSpec((d, tm), lambda b, i, j: (0, b * nt + i)),
                  pl.BlockSpec((te, d), lambda b, i, j: (j, 0)),
                  pl.BlockSpec((d, te), lambda b, i, j: (0, j)),
                  tab1, tab1, tab2, tab2, tok,
                  pl.BlockSpec((1, 1, d), lambda b, i, j: (b, 0, 0)),
                  pl.BlockSpec((1, d), lambda b, i, j: (0, 0))],
        out_specs=tok,
        out_shape=jax.ShapeDtypeStruct((bz, n, d), F32),
        scratch_shapes=[pltpu.VMEM((d, tm), F32)],
        compiler_params=_params("parallel", "parallel", "arbitrary"),
        name="peer",
    )(h2t, u, vt, e1, lim, e2, rk, x1, g2, nfg)


def _inproj_weights(w_in, conv_w, gate_b, lr_w2, alpha_b):
    widths = (ML_WIDTH, ML_WIDTH, ML_WIDTH, ML_WIDTH, N_GATES,
              GLA_KEY_WIDTH, GLA_KEY_WIDTH, GLA_WIDTH, GLA_WIDTH, 2 * GLA_RANK)
    offs = [0]
    for w in widths:
        offs.append(offs[-1] + w)
    wb = w_in.astype(BF16)
    col = lambda a, b: wb[:, offs[a]:offs[b]]
    pad_lanes = lambda a: jnp.pad(a, ((0, 0), (0, LANES - a.shape[1])))
    w2 = jnp.zeros((LANES, 2 * GLA_KEY_WIDTH), F32)
    w2 = w2.at[:GLA_RANK, :GLA_KEY_WIDTH].set(lr_w2[0])
    w2 = w2.at[GLA_RANK:2 * GLA_RANK, GLA_KEY_WIDTH:].set(lr_w2[1])
    return (col(0, 2), col(2, 3), col(3, 4), pad_lanes(col(4, 5)), col(4, 5).T,
            col(5, 7), col(7, 8), col(8, 9), pad_lanes(col(9, 10)),
            conv_w, pad_lanes(gate_b[None, :]), gate_b[:, None],
            w2.astype(BF16), alpha_b.reshape(1, 2 * GLA_KEY_WIDTH))


def _per_head_gates(gtok, gmaj):
    bz, n, _ = gtok.shape
    gt = gtok.reshape(bz, n, 2, 2, ML_HEADS).transpose(0, 4, 1, 2, 3).reshape(bz, ML_HEADS, n, 4)
    gm = gmaj.reshape(bz, 2, 2, ML_HEADS, n // ML_CHUNK, ML_CHUNK)
    gm = gm.transpose(0, 3, 4, 1, 2, 5).reshape(bz, ML_HEADS, n // ML_CHUNK, 4, ML_CHUNK)
    return gt, gm


def _token_mix_inputs(x, sh, sc, ng, wts, row_len):
    mq, mk, mv, mo, gtok, gmaj, gq, gk, gv, gr, bf, bb, mkt, gvt = _inproj(x, sh, sc, ng, wts, row_len)
    gt, gm = _per_head_gates(gtok, gmaj)
    return (mq, mk, mkt, mv, gt, gm), (gq, gk, gv, gvt, bf, bb), mo, gr


def kernel(x, c, ctx, c_ctx, w_mod, b_mod, norm1_g, w_in, ml_conv_w, ml_gate_b, ml_norm_g,
           gla_lr_w2, gla_alpha_b, gla_norm_g, w_out, norm2_g, peer_wq, peer_keys, peer_u,
           peer_v, norm_f_g):
    assert w_mod.shape[0] == 1, "single trunk layer"
    bz, n, d = x.shape
    c_all = jnp.concatenate([c, c_ctx[None, :]], axis=0)
    c_all = jnp.pad(c_all, ((0, (-c_all.shape[0]) % SUBLANES), (0, 0)))
    mod = _modulation(c_all, w_mod[0], b_mod[0][None, :])
    sh1, sc1, g1, sh2, sc2, g2 = [m[:, None, :] for m in jnp.split(mod[:bz], 6, axis=1)]
    mod_c = mod[bz]
    csh1, csc1 = mod_c[None, None, :d], mod_c[None, None, d:2 * d]

    ng1 = norm1_g[0][None, :]
    wts = _inproj_weights(w_in[0], ml_conv_w[0], ml_gate_b[0], gla_lr_w2[0], gla_alpha_b[0])
    ml_lat, gla_lat, mo, gr = _token_mix_inputs(x, sh1, sc1, ng1, wts, GRID_W)
    ml_ctx, gla_ctx, _, _ = _token_mix_inputs(ctx, csh1, csc1, ng1, wts, ctx.shape[1])
    mhf, mhb = _mlstm(*ml_lat, *ml_ctx)
    ghf, ghb = _gla(*gla_lat, *gla_ctx)

    keys = peer_keys[0].reshape(2 * PEER_HEADS, PEER_NKEYS, PEER_HALF).astype(BF16)
    x1, h2t, e1, lim, e2, rk = _mix(
        x, mhf, mhb, ghf, ghb, mo, gr, ml_norm_g[0][None, :], gla_norm_g[0][None, :],
        w_out[0].astype(BF16), g1, sh2, sc2, norm2_g[0][None, :], peer_wq[0].astype(BF16), keys)
    return _peer(h2t, peer_u[0].astype(BF16), peer_v[0].astype(BF16).T, e1, lim, e2, rk,
                 x1, g2, norm_f_g[None, :])
```

```python
import functools

import jax
import jax.numpy as jnp
from jax import lax
from jax.experimental import pallas as pl
from jax.experimental.pallas import tpu as pltpu

F32 = jnp.float32
BF16 = jnp.bfloat16

EPS = 1e-6
D_MODEL = 1024
GRID_W = 64

ML_HEADS = 4
ML_HEAD_DIM = 128
ML_WIDTH = ML_HEADS * ML_HEAD_DIM
ML_CHUNK = 128
GLA_HEADS = 4
GLA_DK = 64
GLA_DV = 128
GLA_KEY_WIDTH = GLA_HEADS * GLA_DK
GLA_WIDTH = GLA_HEADS * GLA_DV
GLA_RANK = 16
GLA_TAU = 16.0
GLA_CHUNK = 64
N_GATES = 4 * ML_HEADS

PEER_HEADS = 8
PEER_NKEYS = 128
PEER_TOPK = 16
PEER_HALF = 128

LANES = 128
SUBLANES = 8
VMEM_LIMIT = 56 * 1024 * 1024

INPROJ_TM = 1024
CUMSUM_ROWS = 256
ML_HEADS_PER_STEP = 4
GLA_HEADS_PER_STEP = 4
MIX_TM = 512
PEER_TM = 512
PEER_TE = 2048

NEG_INF = float("-inf")


def _params(*sem):
    return pltpu.CompilerParams(dimension_semantics=sem, vmem_limit_bytes=VMEM_LIMIT)


def _dot(a, b):
    return jnp.dot(a, b, preferred_element_type=F32)


def _dot_nt(a, b):
    return lax.dot_general(a, b, (((1,), (1,)), ((), ())), preferred_element_type=F32)


def _split2(x):
    hi = x.astype(BF16)
    return hi, (x - hi.astype(F32)).astype(BF16)


def _dot_exact_rhs(a01, x):
    hi, lo = _split2(x)
    return _dot(a01, hi) + _dot(a01, lo)


def _dot_exact_lhs(x, a01):
    hi, lo = _split2(x)
    return _dot(hi, a01) + _dot(lo, a01)


def _block_rows(mask, x):
    r = mask.shape[0]
    return jnp.concatenate([_dot_exact_rhs(mask, x[i:i + r]) for i in range(0, x.shape[0], r)], axis=0)


def _block_cols(x, mask):
    r = mask.shape[0]
    return jnp.concatenate([_dot_exact_lhs(x[:, i:i + r], mask) for i in range(0, x.shape[1], r)], axis=1)


def _sigmoid(x):
    return 1.0 / (1.0 + jnp.exp(-x))


def _log_sigmoid(x):
    return jnp.minimum(x, 0.0) - jnp.log(1.0 + jnp.exp(-jnp.abs(x)))


def _rms(x):
    return x * lax.rsqrt(jnp.mean(x * x, axis=-1, keepdims=True) + EPS)


def _mod_kernel(c_ref, w_ref, b_ref, o_ref):
    cond = c_ref[...]
    cond = cond * _sigmoid(cond)
    ch, cl = cond.astype(BF16), (cond - cond.astype(BF16).astype(F32)).astype(BF16)
    w = w_ref[...]
    wh = w.astype(BF16)
    wl = (w - wh.astype(F32)).astype(BF16)
    o_ref[...] = _dot(ch, wh) + _dot(ch, wl) + _dot(cl, wh) + b_ref[...]


def _modulation(c_all, w_mod, b_mod):
    rows, d = c_all.shape
    n_out = w_mod.shape[1]
    tn = 512
    return pl.pallas_call(
        _mod_kernel,
        grid=(n_out // tn,),
        in_specs=[pl.BlockSpec((rows, d), lambda j: (0, 0)),
                  pl.BlockSpec((d, tn), lambda j: (0, j)),
                  pl.BlockSpec((1, tn), lambda j: (0, j))],
        out_specs=pl.BlockSpec((rows, tn), lambda j: (0, j)),
        out_shape=jax.ShapeDtypeStruct((rows, n_out), F32),
        compiler_params=_params("arbitrary"),
        name="mod",
    )(c_all, w_mod, b_mod)


def _chunk_masks(tm, chunk):
    r = lax.broadcasted_iota(jnp.int32, (tm, tm), 0)
    c = lax.broadcasted_iota(jnp.int32, (tm, tm), 1)
    same = (r // chunk) == (c // chunk)
    prefix = jnp.where(same & (c <= r), 1.0, 0.0).astype(BF16)
    suffix = jnp.where(same & (c >= r), 1.0, 0.0).astype(BF16)
    return prefix, suffix


def _inproj_kernel(x_ref, sh_ref, sc_ref, ng_ref, wqk_ref, wv_ref, wo_ref, wg_ref, wgt_ref,
                   wgqk_ref, wgv_ref, wgr_ref, wlr_ref, conv_ref, gb_ref, gbc_ref, w2_ref, ab_ref,
                   mq_ref, mk_ref, mv_ref, mo_ref, gtok_ref, gmaj_ref,
                   gq_ref, gk_ref, gv_ref, gr_ref, bf_ref, bb_ref, mkt_ref, gvt_ref, *, row_len):
    tm = x_ref.shape[1]
    x = x_ref[0]
    h = _rms(x) * ng_ref[...] * (1.0 + sc_ref[0]) + sh_ref[0]
    hb = h.astype(BF16)

    u = _dot(hb, wqk_ref[...])
    pos = lax.broadcasted_iota(jnp.int32, (tm, 1), 0) % row_len
    up = jnp.where(pos == 0, 0.0, pltpu.roll(u, 1, 0))
    dn = jnp.where(pos == row_len - 1, 0.0, pltpu.roll(u, tm - 1, 0))
    cw = conv_ref[...]
    y = cw[0:1] * up + cw[1:2] * u + cw[2:3] * dn
    y = y * _sigmoid(y)
    mq_ref[0] = y[:, :ML_WIDTH].astype(BF16)
    mk = y[:, ML_WIDTH:] * ML_HEAD_DIM ** -0.5
    mk_ref[0] = mk.astype(BF16)
    mkt = mk.T.astype(BF16)
    for hh in range(ML_HEADS):
        for ci in range(tm // ML_CHUNK):
            mkt_ref[0, hh, ci] = mkt[hh * ML_HEAD_DIM:(hh + 1) * ML_HEAD_DIM, ci * ML_CHUNK:(ci + 1) * ML_CHUNK]
    mv_ref[0] = _dot(hb, wv_ref[...]).astype(BF16)
    mo_ref[0] = _dot(hb, wo_ref[...]).astype(BF16)

    pre128, suf128 = _chunk_masks(min(tm, CUMSUM_ROWS), ML_CHUNK)
    g = _dot(hb, wg_ref[...]) + gb_ref[...]
    col = lax.broadcasted_iota(jnp.int32, (1, LANES), 1)
    is_f = ((col % 8) >= 4) & (col < N_GATES)
    is_bwd = col >= 8
    lf = jnp.where(is_f, _log_sigmoid(g), 0.0)
    cum = jnp.where(is_bwd, _block_rows(suf128, lf), _block_rows(pre128, lf))
    gtok_ref[0] = jnp.where(is_f, cum, g)[:, :N_GATES]

    gt = _dot_nt(wgt_ref[...], hb) + gbc_ref[...]
    row = lax.broadcasted_iota(jnp.int32, (N_GATES, 1), 0)
    is_f_r = (row % 8) >= 4
    lft = jnp.where(is_f_r, _log_sigmoid(gt), 0.0)
    cum_t = jnp.where(row >= 8, _block_cols(lft, pre128), _block_cols(lft, suf128))
    gmaj_ref[0] = jnp.where(is_f_r, cum_t, gt)

    gqk = _dot(hb, wgqk_ref[...])
    gq_ref[0] = (gqk[:, :GLA_KEY_WIDTH] * GLA_DK ** -0.5).astype(BF16)
    gk_ref[0] = gqk[:, GLA_KEY_WIDTH:].astype(BF16)
    gv = _dot(hb, wgv_ref[...])
    gv_ref[0] = gv.astype(BF16)
    gvt = gv.T.astype(BF16)
    for hh in range(GLA_HEADS):
        for ci in range(tm // GLA_CHUNK):
            gvt_ref[0, hh, ci] = gvt[hh * GLA_DV:(hh + 1) * GLA_DV, ci * GLA_CHUNK:(ci + 1) * GLA_CHUNK]
    gr_ref[0] = _dot(hb, wgr_ref[...]).astype(BF16)
    glr = _dot(hb, wlr_ref[...]).astype(BF16)
    alpha = _dot(glr, w2_ref[...]) + ab_ref[...]
    la = _log_sigmoid(alpha) * (1.0 / GLA_TAU)
    pre64, suf64 = _chunk_masks(min(tm, CUMSUM_ROWS), GLA_CHUNK)
    bf_ref[0] = _block_rows(pre64, la[:, :GLA_KEY_WIDTH])
    bb_ref[0] = _block_rows(suf64, la[:, GLA_KEY_WIDTH:])


def _inproj(x, sh, sc, ng, wts, row_len):
    bz, n, d = x.shape
    tm = min(INPROJ_TM, n)
    assert n % tm == 0 and tm % row_len == 0 and tm % ML_CHUNK == 0
    grid = (bz, n // tm)
    per_b = (lambda b, i: (b, 0, 0)) if sh.shape[0] == bz else (lambda b, i: (0, 0, 0))

    def full(a):
        return pl.BlockSpec(a.shape, lambda b, i, nd=a.ndim: (0,) * nd)

    def tok(width):
        return pl.BlockSpec((1, tm, width), lambda b, i: (b, i, 0))

    def out(width, dtype):
        return jax.ShapeDtypeStruct((bz, n, width), dtype)

    in_specs = [tok(d), pl.BlockSpec((1, 1, d), per_b), pl.BlockSpec((1, 1, d), per_b), full(ng)]
    in_specs += [full(w) for w in wts]
    out_specs = [tok(ML_WIDTH)] * 4 + [tok(N_GATES), pl.BlockSpec((1, N_GATES, tm), lambda b, i: (b, 0, i))]
    out_specs += [tok(GLA_KEY_WIDTH)] * 2 + [tok(GLA_WIDTH)] * 2 + [tok(GLA_KEY_WIDTH)] * 2
    out_shape = [out(ML_WIDTH, BF16)] * 4 + [out(N_GATES, F32),
                                              jax.ShapeDtypeStruct((bz, N_GATES, n), F32)]
    out_shape += ([out(GLA_KEY_WIDTH, BF16)] * 2 + [out(GLA_WIDTH, BF16)] * 2
                  + [out(GLA_KEY_WIDTH, F32)] * 2)
    for heads, dim, chunk in ((ML_HEADS, ML_HEAD_DIM, ML_CHUNK), (GLA_HEADS, GLA_DV, GLA_CHUNK)):
        out_specs.append(pl.BlockSpec((1, heads, tm // chunk, dim, chunk), lambda b, i: (b, 0, i, 0, 0)))
        out_shape.append(jax.ShapeDtypeStruct((bz, heads, n // chunk, dim, chunk), BF16))
    return pl.pallas_call(
        functools.partial(_inproj_kernel, row_len=row_len),
        grid=grid, in_specs=in_specs, out_specs=out_specs, out_shape=out_shape,
        compiler_params=_params("parallel", "arbitrary"),
        name="inproj",
    )(x, sh, sc, ng, *wts)


def _bmm(a, b):
    return lax.dot_general(a, b, (((2,), (1,)), ((0,), (0,))), preferred_element_type=F32)


def _bmm_nt(a, b):
    return lax.dot_general(a, b, (((2,), (2,)), ((0,), (0,))), preferred_element_type=F32)


def _by_direction(x, fwd, bwd):
    half = x.shape[0] // 2
    return jnp.concatenate([fwd(x[:half]), bwd(x[half:])], axis=0)


def _ml_chunks(q, k, kt, v, gt, gm, masks, state, want_out):
    ct, nv, m = state
    ll = q.shape[1]
    ig_col = _by_direction(gt, lambda g: g[:, :, 0:1], lambda g: g[:, :, 2:3])
    b_col = _by_direction(gt, lambda g: g[:, :, 1:2], lambda g: g[:, :, 3:4])
    ig_row = _by_direction(gm, lambda g: g[:, 0:1, :], lambda g: g[:, 2:3, :])
    b_row = _by_direction(gm, lambda g: g[:, 1:2, :], lambda g: g[:, 3:4, :])
    b_end = _by_direction(b_row, lambda b: b[:, :, ll - 1:ll], lambda b: b[:, :, 0:1])
    h = None
    if want_out:
        d_log = _by_direction(b_col - b_row + ig_row,
                              lambda x: jnp.where(masks[0], x, NEG_INF),
                              lambda x: jnp.where(masks[1], x, NEG_INF))
        inter_log = b_col + m
        m_t = jnp.maximum(inter_log, jnp.max(d_log, axis=2, keepdims=True))
        scores = _bmm_nt(q, k) * jnp.exp(d_log - m_t)
        inter = jnp.exp(inter_log - m_t)
        qf = q.astype(F32)
        num = _bmm(scores.astype(BF16), v) + inter * _bmm(q, ct.astype(BF16))
        den = jnp.sum(scores, axis=2, keepdims=True) + inter * jnp.sum(qf * nv, axis=2, keepdims=True)
        h = num / jnp.maximum(jnp.abs(den), jnp.exp(-m_t))
    m_loc = jnp.max(b_end - b_row + ig_row, axis=2, keepdims=True)
    w_col = jnp.exp(b_end - b_col + ig_col - m_loc)
    c_loc = _bmm(kt, (w_col * v.astype(F32)).astype(BF16))
    n_loc = jnp.sum(w_col * k.astype(F32), axis=1, keepdims=True)
    m_new = jnp.maximum(b_end + m, m_loc)
    a = jnp.exp(b_end + m - m_new)
    bb = jnp.exp(m_loc - m_new)
    return h, (a * ct + bb * c_loc, a * nv + bb * n_loc, m_new)


def _mlstm_kernel(q_ref, k_ref, kt_ref, v_ref, gt_ref, gm_ref,
                  qc_ref, kc_ref, ktc_ref, vc_ref, gtc_ref, gmc_ref, hf_ref, hb_ref):
    ll = ML_CHUNK
    nc = q_ref.shape[1] // ll
    ncc = qc_ref.shape[1] // ll
    r = lax.broadcasted_iota(jnp.int32, (ll, ll), 0)
    c = lax.broadcasted_iota(jnp.int32, (ll, ll), 1)
    masks = (c <= r, c >= r)

    heads = gt_ref.shape[1]
    head_cols = [slice(hh * ML_HEAD_DIM, (hh + 1) * ML_HEAD_DIM) for hh in range(heads)]

    def rows_of(ci):
        return pl.ds(pl.multiple_of(ci * ll, ll), ll)

    def load(refs, cf, cb):
        qr, kr, ktr, vr, gtr, gmr = refs
        sites = [(hh, ci) for ci in (cf, cb) for hh in range(heads)]
        seq = lambda r: jnp.stack([r[0, rows_of(ci), head_cols[hh]] for hh, ci in sites])
        return (seq(qr), seq(kr), jnp.stack([ktr[0, hh, ci] for hh, ci in sites]), seq(vr),
                jnp.stack([gtr[0, hh, rows_of(ci), :] for hh, ci in sites]),
                jnp.stack([gmr[0, hh, ci] for hh, ci in sites]))

    chains = 2 * heads
    zero = (jnp.zeros((chains, ML_HEAD_DIM, ML_HEAD_DIM), F32), jnp.zeros((chains, 1, ML_HEAD_DIM), F32),
            jnp.zeros((chains, 1, 1), F32))
    ctx_refs = (qc_ref, kc_ref, ktc_ref, vc_ref, gtc_ref, gmc_ref)
    lat_refs = (q_ref, k_ref, kt_ref, v_ref, gt_ref, gm_ref)

    def ctx_body(i, state):
        _, state = _ml_chunks(*load(ctx_refs, i, ncc - 1 - i), masks, state, False)
        return state

    def lat_body(i, state):
        cb = nc - 1 - i
        h, state = _ml_chunks(*load(lat_refs, i, cb), masks, state, True)
        for hh in range(heads):
            hf_ref[0, rows_of(i), head_cols[hh]] = h[hh]
            hb_ref[0, rows_of(cb), head_cols[hh]] = h[heads + hh]
        return state

    state = lax.fori_loop(0, ncc, ctx_body, zero)
    lax.fori_loop(0, nc, lat_body, state)


def _mlstm(q, k, kt, v, gt, gm, qc, kc, ktc, vc, gtc, gmc):
    bz, n, _ = q.shape
    ncx = qc.shape[1]
    hps = ML_HEADS_PER_STEP

    def seq(nn):
        return pl.BlockSpec((1, nn, hps * ML_HEAD_DIM), lambda b, h: (b, 0, h))

    def ktr(nn):
        return pl.BlockSpec((1, hps, nn // ML_CHUNK, ML_HEAD_DIM, ML_CHUNK), lambda b, h: (b, h, 0, 0, 0))

    def gtok(nn):
        return pl.BlockSpec((1, hps, nn, 4), lambda b, h: (b, h, 0, 0))

    def gmaj(nn):
        return pl.BlockSpec((1, hps, nn // ML_CHUNK, 4, ML_CHUNK), lambda b, h: (b, h, 0, 0, 0))

    def specs(nn):
        return [seq(nn), seq(nn), ktr(nn), seq(nn), gtok(nn), gmaj(nn)]

    return pl.pallas_call(
        _mlstm_kernel,
        grid=(bz, ML_HEADS // hps),
        in_specs=specs(n) + specs(ncx),
        out_specs=[seq(n)] * 2,
        out_shape=[jax.ShapeDtypeStruct((bz, n, ML_WIDTH), F32)] * 2,
        compiler_params=_params("parallel", "arbitrary"),
        name="mlstm",
    )(q, k, kt, v, gt, gm, qc, kc, ktc, vc, gtc, gmc)


def _gla_chunks(q, k, v, vt, b, masks, st, want_out):
    ll = q.shape[1]
    qf, kf = q.astype(F32), k.astype(F32)
    ref = _by_direction(b, lambda x: x[:, ll // 2:ll // 2 + 1], lambda x: x[:, ll // 2 - 1:ll // 2])
    b_end = _by_direction(b, lambda x: x[:, ll - 1:ll], lambda x: x[:, 0:1])
    o = None
    if want_out:
        att = _bmm_nt((qf * jnp.exp(b - ref)).astype(BF16), (kf * jnp.exp(ref - b)).astype(BF16))
        att = _by_direction(att, lambda x: jnp.where(masks[0], x, 0.0), lambda x: jnp.where(masks[1], x, 0.0))
        o = _bmm(att.astype(BF16), v) + _bmm_nt((qf * jnp.exp(b)).astype(BF16), st.astype(BF16))
    s_loc = _bmm(vt, (kf * jnp.exp(b_end - b)).astype(BF16))
    return o, jnp.exp(b_end) * st + s_loc


def _gla_kernel(q_ref, k_ref, v_ref, vt_ref, bf_ref, bb_ref,
                qc_ref, kc_ref, vc_ref, vtc_ref, bfc_ref, bbc_ref, of_ref, ob_ref):
    ll = GLA_CHUNK
    nc = q_ref.shape[1] // ll
    ncc = qc_ref.shape[1] // ll
    r = lax.broadcasted_iota(jnp.int32, (ll, ll), 0)
    c = lax.broadcasted_iota(jnp.int32, (ll, ll), 1)
    masks = (c <= r, c >= r)

    heads = vt_ref.shape[1]

    def rows_of(ci):
        return pl.ds(pl.multiple_of(ci * ll, ll), ll)

    head_cols = [slice(hh * GLA_DV, (hh + 1) * GLA_DV) for hh in range(heads)]
    key_cols = [slice(hh * GLA_DK, (hh + 1) * GLA_DK) for hh in range(heads)]

    def load(refs, cf, cb):
        qr, kr, vr, vtr, bfr, bbr = refs
        sites = [(hh, ci) for ci in (cf, cb) for hh in range(heads)]
        per_head = lambda r: jnp.stack([r[0, rows_of(ci), key_cols[hh]] for hh, ci in sites])
        return (per_head(qr), per_head(kr),
                jnp.stack([vr[0, rows_of(ci), head_cols[hh]] for hh, ci in sites]),
                jnp.stack([vtr[0, hh, ci] for hh, ci in sites]),
                jnp.stack([bfr[0, rows_of(cf), key_cols[hh]] for hh in range(heads)]
                          + [bbr[0, rows_of(cb), key_cols[hh]] for hh in range(heads)]))

    zero = jnp.zeros((2 * heads, GLA_DV, GLA_DK), F32)
    ctx_refs = (qc_ref, kc_ref, vc_ref, vtc_ref, bfc_ref, bbc_ref)
    lat_refs = (q_ref, k_ref, v_ref, vt_ref, bf_ref, bb_ref)

    def ctx_body(i, st):
        _, st = _gla_chunks(*load(ctx_refs, i, ncc - 1 - i), masks, st, False)
        return st

    def lat_body(i, st):
        cb = nc - 1 - i
        o, st = _gla_chunks(*load(lat_refs, i, cb), masks, st, True)
        for hh in range(heads):
            of_ref[0, rows_of(i), head_cols[hh]] = o[hh]
            ob_ref[0, rows_of(cb), head_cols[hh]] = o[heads + hh]
        return st

    st = lax.fori_loop(0, ncc, ctx_body, zero)
    lax.fori_loop(0, nc, lat_body, st, unroll=2)


def _gla(q, k, v, vt, bf, bb, qc, kc, vc, vtc, bfc, bbc):
    bz, n, _ = q.shape
    ncx = qc.shape[1]
    hps = GLA_HEADS_PER_STEP

    def key(nn):
        return pl.BlockSpec((1, nn, hps * GLA_DK), lambda b, h: (b, 0, h))

    def val(nn):
        return pl.BlockSpec((1, nn, hps * GLA_DV), lambda b, h: (b, 0, h))

    def valt(nn):
        return pl.BlockSpec((1, hps, nn // GLA_CHUNK, GLA_DV, GLA_CHUNK), lambda b, h: (b, h, 0, 0, 0))

    def specs(nn):
        return [key(nn), key(nn), val(nn), valt(nn), key(nn), key(nn)]

    return pl.pallas_call(
        _gla_kernel,
        grid=(bz, GLA_HEADS // hps),
        in_specs=specs(n) + specs(ncx),
        out_specs=[val(n)] * 2,
        out_shape=[jax.ShapeDtypeStruct((bz, n, GLA_WIDTH), F32)] * 2,
        compiler_params=_params("parallel", "arbitrary"),
        name="gla",
    )(q, k, v, vt, bf, bb, qc, kc, vc, vtc, bfc, bbc)


def _head_rms(h, g, width):
    parts = [_rms(h[:, i:i + width]) for i in range(0, h.shape[1], width)]
    return jnp.concatenate(parts, axis=1) * g


def _oddeven_merge(lo, hi, r):
    step = r * 2
    if step < hi - lo:
        yield from _oddeven_merge(lo, hi, step)
        yield from _oddeven_merge(lo + r, hi, step)
        yield from [(i, i + r) for i in range(lo + r, hi - r, step)]
    else:
        yield (lo, lo + r)


def _oddeven_merge_sort(lo, hi):
    if hi - lo >= 1:
        mid = lo + (hi - lo) // 2
        yield from _oddeven_merge_sort(lo, mid)
        yield from _oddeven_merge_sort(mid + 1, hi)
        yield from _oddeven_merge(lo, hi, 1)


def _exchange(v, i, j):
    v[i], v[j] = jnp.maximum(v[i], v[j]), jnp.minimum(v[i], v[j])


def _merge_top16(v, shift):
    other = [pltpu.roll(x, shift, 0) for x in v]
    out = []
    for j in range(PEER_TOPK):
        a = v[j] if j < len(v) else None
        b = other[PEER_TOPK - 1 - j] if PEER_TOPK - 1 - j < len(v) else None
        out.append(a if b is None else b if a is None else jnp.maximum(a, b))
    return out


def _sort_bitonic16(v):
    v = list(v)
    for d in (8, 4, 2, 1):
        for j in range(PEER_TOPK):
            if not j & d:
                _exchange(v, j, j + d)
    return v


def _sixteenth_largest(blocks):
    v = list(blocks)
    assert SUBLANES <= len(v) <= PEER_TOPK
    for i, j in _oddeven_merge_sort(0, PEER_TOPK - 1):
        if j < len(v):
            _exchange(v, i, j)
    v = _sort_bitonic16(_merge_top16(v, 4))
    v = _sort_bitonic16(_merge_top16(v, 2))
    v = _merge_top16(v, 1)
    return functools.reduce(jnp.minimum, v)[0:1]


def _sorted_top16(tiles):
    v = list(tiles)
    n = len(v)
    for i, j in _oddeven_merge_sort(0, n - 1):
        _exchange(v, i, j)
    for shift in (4, 2, 1):
        v = _sort_bitonic16(_merge_top16(v, shift))
    sub = lax.broadcasted_iota(jnp.int32, v[0].shape, 0)
    halves = []
    for lo in (0, SUBLANES):
        rows = v[lo]
        for r in range(1, SUBLANES):
            rows = jnp.where(sub == r, v[lo + r], rows)
        halves.append(rows)
    repeated = jnp.zeros_like(v[0])
    for r in range(n - 1):
        repeated = jnp.where(v[r] == v[r + 1], 1.0, repeated)
    return jnp.concatenate(halves, axis=0), repeated[0:1]


def _prefix_count(x, thr, test):
    r = [thr[i:i + 1] for i in range(PEER_TOPK)]
    sel = jnp.where
    t8 = test(x, r[7])
    t4 = test(x, sel(t8, r[11], r[3]))
    t2 = test(x, sel(t8, sel(t4, r[13], r[9]), sel(t4, r[5], r[1])))
    t1 = test(x, sel(t8, sel(t4, sel(t2, r[14], r[12]), sel(t2, r[10], r[8])),
                     sel(t4, sel(t2, r[6], r[4]), sel(t2, r[2], r[0]))))
    count = sel(t8, 8.0, 0.0) + sel(t4, 4.0, 0.0) + sel(t2, 2.0, 0.0) + sel(t1, 1.0, 0.0)
    return sel(test(x, r[15]), float(PEER_TOPK), count)


def _count(mask):
    return jnp.sum(jnp.where(mask, 1.0, 0.0), axis=0, keepdims=True)


def _top16(s):
    nk, tl = s.shape
    slot = lax.broadcasted_iota(jnp.int32, (PEER_TOPK, tl), 0)
    key_id = lax.broadcasted_iota(jnp.int32, (nk, tl), 0).astype(F32)

    def body(r, carry):
        s, rank, vals = carry
        m = jnp.max(s, axis=0, keepdims=True)
        first = jnp.min(jnp.where(s == m, key_id, float(nk)), axis=0, keepdims=True)
        sel = key_id == first
        rank = jnp.where(sel, jnp.asarray(r).astype(F32), rank)
        s = jnp.where(sel, NEG_INF, s)
        vals = jnp.where(slot == r, m, vals)
        return s, rank, vals

    init = (s, jnp.full((nk, tl), float(PEER_TOPK), F32), jnp.zeros((PEER_TOPK, tl), F32))
    _, rank, vals = lax.fori_loop(0, PEER_TOPK, body, init)
    return vals, rank


def _pair_candidates(v1, v2):
    blocks = [v1[0:1] + v2[0:8], v1[0:1] + v2[8:16]]
    blocks += [v1[a:a + 1] + v2[0:8] for a in range(1, 8)]
    blocks += [v1[8:16] + v2[0:1]]
    return jnp.concatenate(blocks, axis=0)


def _blocks(cand):
    return [cand[i:i + SUBLANES] for i in range(0, cand.shape[0], SUBLANES)]


def _pick16(cand):
    i = lax.broadcasted_iota(jnp.int32, cand.shape, 0)
    blk, rr = i // 8, i % 8
    flat = jnp.where(blk == 0, rr, jnp.where(blk == 1, 8 + rr,
                     jnp.where(blk <= 8, (blk - 1) * 16 + rr, (8 + rr) * 16))).astype(F32)

    def body(_, carry):
        cand, picked = carry
        m = jnp.max(cand, axis=0, keepdims=True)
        first = jnp.min(jnp.where(cand == m, flat, 1e9), axis=0, keepdims=True)
        sel = flat == first
        return jnp.where(sel, NEG_INF, cand), jnp.where(sel, 1.0, picked)

    _, picked = lax.fori_loop(0, PEER_TOPK, body, (cand, jnp.zeros_like(cand)))
    return picked


def _first_key_thresholds(picked, v1):
    inf = jnp.inf
    lo = jnp.where(picked[0:8] > 0.0, v1[0:1], inf)
    for a in range(1, 8):
        lo = jnp.minimum(lo, jnp.where(picked[8 * (a + 1):8 * (a + 2)] > 0.0, v1[a:a + 1], inf))
    tail = jnp.min(jnp.where(picked[72:80] > 0.0, v1[8:16], inf), axis=0, keepdims=True)
    row = lax.broadcasted_iota(jnp.int32, lo.shape, 0)
    lo = jnp.minimum(lo, jnp.where(row == 0, tail, inf))
    hi = jnp.where(picked[8:16] > 0.0, v1[0:1], inf)
    return jnp.concatenate([lo, hi], axis=0)


def _pair_counts(cand, picked):
    z = jnp.sum(picked * jnp.exp(cand - cand[0:1]), axis=0, keepdims=True)
    n_rows = [jnp.sum(picked[0:16], axis=0, keepdims=True)]
    n_rows += [jnp.sum(picked[8 * (a + 1):8 * (a + 2)], axis=0, keepdims=True) for a in range(1, 8)]
    return jnp.concatenate(n_rows + [picked[72:80]], axis=0), z


def _mix_kernel(x_ref, mhf_ref, mhb_ref, ghf_ref, ghb_ref, mo_ref, gr_ref, mlg_ref, glag_ref,
                wout_ref, g1_ref, sh2_ref, sc2_ref, n2g_ref, wq_ref, keys_ref,
                x1_ref, h2t_ref, e1_ref, l_ref, e2_ref, rk_ref, q_scr, s_scr):
    ml = _head_rms(mhf_ref[0] + mhb_ref[0], mlg_ref[...], ML_HEAD_DIM)
    ml = _sigmoid(mo_ref[0].astype(F32)) * ml
    gl = _head_rms(ghf_ref[0] + ghb_ref[0], glag_ref[...], GLA_DV)
    gr = gr_ref[0].astype(F32)
    gl = gr * _sigmoid(gr) * gl
    mix = jnp.concatenate([ml, gl], axis=1).astype(BF16)
    x1 = x_ref[0] + g1_ref[0] * _dot(mix, wout_ref[...])
    x1_ref[0] = x1
    h2 = _rms(x1) * n2g_ref[...] * (1.0 + sc2_ref[0]) + sh2_ref[0]
    h2t_ref[...] = h2.T.astype(BF16)
    qall = _dot(h2.astype(BF16), wq_ref[...]).astype(BF16)
    for j in range(2 * PEER_HEADS):
        q_scr[j] = qall[:, j * PEER_HALF:(j + 1) * PEER_HALF]

    tm = x_ref.shape[1]
    topk = float(PEER_TOPK)
    lane_chunks = [slice(i, i + LANES) for i in range(0, tm, LANES)]

    def write_tables(p, lanes, s1, s2, v1, v2, z, lim, rank2):
        e1_ref[p, :, lanes] = jnp.exp(s1 - v1[0:1]) * (0.5 / z)
        l_ref[p, :, lanes] = lim
        e2 = jnp.exp(s2 - v2[0:1]).astype(BF16)
        e2_ref[p, :, :, lanes] = e2.reshape(PEER_NKEYS // 16, 16, LANES)
        rk_ref[p, :, :, lanes] = rank2.astype(BF16).reshape(PEER_NKEYS // 16, 16, LANES)

    def head_body(p, carry):
        for hf in range(2):
            s_scr[hf] = _dot_nt(keys_ref[2 * p + hf], q_scr[2 * p + hf])
        excess = jnp.zeros((1, LANES), F32)
        vals = []
        for hf in range(2):
            for lanes in lane_chunks:
                tiles = [s_scr[hf, i:i + SUBLANES, lanes] for i in range(0, PEER_NKEYS, SUBLANES)]
                top, repeated = _sorted_top16(tiles)
                vals.append(top)
                excess = jnp.maximum(excess, repeated)
        v1s, v2s = vals[:len(lane_chunks)], vals[len(lane_chunks):]
        cands = [_pair_candidates(v1, v2) for v1, v2 in zip(v1s, v2s)]
        taus = [_sixteenth_largest(_blocks(c)) for c in cands]
        for lanes, v1, v2, cand, tau in zip(lane_chunks, v1s, v2s, cands, taus):
            s1, s2 = s_scr[0, :, lanes], s_scr[1, :, lanes]
            picked = jnp.where(cand >= tau, 1.0, 0.0)
            _, z = _pair_counts(cand, picked)
            theta = _first_key_thresholds(picked, v1)
            lim = _prefix_count(s1, theta, lambda x, t: x >= t)
            rank2 = _prefix_count(s2, v2, lambda x, t: x < t)
            write_tables(p, lanes, s1, s2, v1, v2, z, lim, rank2)
            n_marked = (_count(s1 >= v1[PEER_TOPK - 1:PEER_TOPK]) + _count(rank2 < topk)
                        + jnp.sum(picked, axis=0, keepdims=True))
            excess = jnp.maximum(excess, n_marked - 3.0 * topk)

        @pl.when(jnp.max(excess) > 0.0)
        def _():
            for lanes in lane_chunks:
                s1, s2 = s_scr[0, :, lanes], s_scr[1, :, lanes]
                v1, rank1 = _top16(s1)
                v2, rank2 = _top16(s2)
                cand = _pair_candidates(v1, v2)
                n, z = _pair_counts(cand, _pick16(cand))
                lim = jnp.zeros_like(rank1)
                for a in range(PEER_TOPK):
                    lim = jnp.where(rank1 == float(a), n[a:a + 1], lim)
                write_tables(p, lanes, s1, s2, v1, v2, z, lim, rank2)

        return carry

    lax.fori_loop(0, PEER_HEADS, head_body, 0)


def _mix(x, mhf, mhb, ghf, ghb, mo, gr, mlg, glag, wout, g1, sh2, sc2, n2g, wq, keys):
    bz, n, d = x.shape
    tm = min(MIX_TM, n)
    assert n % tm == 0 and tm % LANES == 0
    nt = n // tm
    t_all = bz * n

    def tok(width):
        return pl.BlockSpec((1, tm, width), lambda b, i: (b, i, 0))

    def full(a):
        return pl.BlockSpec(a.shape, lambda b, i, nd=a.ndim: (0,) * nd)

    vec = pl.BlockSpec((1, 1, d), lambda b, i: (b, 0, 0))
    tab1 = pl.BlockSpec((PEER_HEADS, PEER_NKEYS, tm), lambda b, i: (0, 0, b * nt + i))
    tab2 = pl.BlockSpec((PEER_HEADS, PEER_NKEYS // 16, 16, tm), lambda b, i: (0, 0, 0, b * nt + i))
    tab1_shape = jax.ShapeDtypeStruct((PEER_HEADS, PEER_NKEYS, t_all), F32)
    tab2_shape = jax.ShapeDtypeStruct((PEER_HEADS, PEER_NKEYS // 16, 16, t_all), BF16)
    return pl.pallas_call(
        _mix_kernel,
        grid=(bz, nt),
        in_specs=[tok(d)] + [tok(ML_WIDTH)] * 6 + [full(mlg), full(glag), full(wout), vec, vec, vec,
                                                    full(n2g), full(wq), full(keys)],
        out_specs=[tok(d), pl.BlockSpec((d, tm), lambda b, i: (0, b * nt + i)), tab1, tab1, tab2, tab2],
        out_shape=[jax.ShapeDtypeStruct((bz, n, d), F32), jax.ShapeDtypeStruct((d, t_all), BF16),
                   tab1_shape, tab1_shape, tab2_shape, tab2_shape],
        scratch_shapes=[pltpu.VMEM((2 * PEER_HEADS, tm, PEER_HALF), BF16),
                        pltpu.VMEM((2, PEER_NKEYS, tm), F32)],
        compiler_params=_params("parallel", "arbitrary"),
        name="mix",
    )(x, mhf, mhb, ghf, ghb, mo, gr, mlg, glag, wout, g1, sh2, sc2, n2g, wq, keys)


def _peer_kernel(h2t_ref, u_ref, vt_ref, e1_ref, l_ref, e2_ref, rk_ref, x1_ref, g2_ref, nfg_ref,
                 out_ref, acc_ref):
    j = pl.program_id(2)
    te, tm = u_ref.shape[0], h2t_ref.shape[1]
    rows_per_step = te // PEER_NKEYS
    tiles = PEER_NKEYS // 16

    @pl.when(j == 0)
    def _():
        acc_ref[...] = jnp.zeros_like(acc_ref)

    act = _dot(u_ref[...], h2t_ref[...])
    act = act.astype(BF16)
    act = (act * (1.0 + lax.erf(act * 2.0 ** -0.5))).reshape(rows_per_step, tiles, 16, tm)
    parts = []
    for i in range(rows_per_step):
        w = None
        for p in range(PEER_HEADS):
            e1 = jnp.broadcast_to(e1_ref[p, i:i + 1, :], (16, tm)).astype(BF16)[None]
            lim = jnp.broadcast_to(l_ref[p, i:i + 1, :], (16, tm)).astype(BF16)[None]
            term = jnp.where(rk_ref[p] < lim, e2_ref[p] * e1, jnp.zeros((), BF16))
            w = term if w is None else w + term
        parts.append(w * act[i])
    gated = jnp.concatenate(parts, axis=0).reshape(te, tm)
    acc_ref[...] += _dot(vt_ref[...], gated)

    @pl.when(j == pl.num_programs(2) - 1)
    def _():
        xf = x1_ref[0] + g2_ref[0] * acc_ref[...].T
        out_ref[0] = _rms(xf) * nfg_ref[...]


def _peer(h2t, u, vt, e1, lim, e2, rk, x1, g2, nfg):
    bz, n, d = x1.shape
    n_exp = u.shape[0]
    tm = min(PEER_TM, n)
    te = PEER_TE
    assert n % tm == 0 and n_exp % te == 0 and te % PEER_NKEYS == 0
    nt = n // tm
    tab1 = pl.BlockSpec((PEER_HEADS, te // PEER_NKEYS, tm), lambda b, i, j: (0, j, b * nt + i))
    tab2 = pl.BlockSpec((PEER_HEADS, PEER_NKEYS // 16, 16, tm), lambda b, i, j: (0, 0, 0, b * nt + i))
    tok = pl.BlockSpec((1, tm, d), lambda b, i, j: (b, i, 0))
    return pl.pallas_call(
        _peer_kernel,
        grid=(bz, nt, n_exp // te),
        in_specs=[pl.BlockSpec((d, tm), lambda b, i, j: (0, b * nt + i)),
                  pl.BlockSpec((te, d), lambda b, i, j: (j, 0)),
                  pl.BlockSpec((d, te), lambda b, i, j: (0, j)),
                  tab1, tab1, tab2, tab2, tok,
                  pl.BlockSpec((1, 1, d), lambda b, i, j: (b, 0, 0)),
                  pl.BlockSpec((1, d), lambda b, i, j: (0, 0))],
        out_specs=tok,
        out_shape=jax.ShapeDtypeStruct((bz, n, d), F32),
        scratch_shapes=[pltpu.VMEM((d, tm), F32)],
        compiler_params=_params("parallel", "parallel", "arbitrary"),
        name="peer",
    )(h2t, u, vt, e1, lim, e2, rk, x1, g2, nfg)


def _inproj_weights(w_in, conv_w, gate_b, lr_w2, alpha_b):
    widths = (ML_WIDTH, ML_WIDTH, ML_WIDTH, ML_WIDTH, N_GATES,
              GLA_KEY_WIDTH, GLA_KEY_WIDTH, GLA_WIDTH, GLA_WIDTH, 2 * GLA_RANK)
    offs = [0]
    for w in widths:
        offs.append(offs[-1] + w)
    wb = w_in.astype(BF16)
    col = lambda a, b: wb[:, offs[a]:offs[b]]
    pad_lanes = lambda a: jnp.pad(a, ((0, 0), (0, LANES - a.shape[1])))
    w2 = jnp.zeros((LANES, 2 * GLA_KEY_WIDTH), F32)
    w2 = w2.at[:GLA_RANK, :GLA_KEY_WIDTH].set(lr_w2[0])
    w2 = w2.at[GLA_RANK:2 * GLA_RANK, GLA_KEY_WIDTH:].set(lr_w2[1])
    return (col(0, 2), col(2, 3), col(3, 4), pad_lanes(col(4, 5)), col(4, 5).T,
            col(5, 7), col(7, 8), col(8, 9), pad_lanes(col(9, 10)),
            conv_w, pad_lanes(gate_b[None, :]), gate_b[:, None],
            w2.astype(BF16), alpha_b.reshape(1, 2 * GLA_KEY_WIDTH))


def _per_head_gates(gtok, gmaj):
    bz, n, _ = gtok.shape
    gt = gtok.reshape(bz, n, 2, 2, ML_HEADS).transpose(0, 4, 1, 2, 3).reshape(bz, ML_HEADS, n, 4)
    gm = gmaj.reshape(bz, 2, 2, ML_HEADS, n // ML_CHUNK, ML_CHUNK)
    gm = gm.transpose(0, 3, 4, 1, 2, 5).reshape(bz, ML_HEADS, n // ML_CHUNK, 4, ML_CHUNK)
    return gt, gm


def _token_mix_inputs(x, sh, sc, ng, wts, row_len):
    mq, mk, mv, mo, gtok, gmaj, gq, gk, gv, gr, bf, bb, mkt, gvt = _inproj(x, sh, sc, ng, wts, row_len)
    gt, gm = _per_head_gates(gtok, gmaj)
    return (mq, mk, mkt, mv, gt, gm), (gq, gk, gv, gvt, bf, bb), mo, gr


def kernel(x, c, ctx, c_ctx, w_mod, b_mod, norm1_g, w_in, ml_conv_w, ml_gate_b, ml_norm_g,
           gla_lr_w2, gla_alpha_b, gla_norm_g, w_out, norm2_g, peer_wq, peer_keys, peer_u,
           peer_v, norm_f_g):
    assert w_mod.shape[0] == 1, "single trunk layer"
    bz, n, d = x.shape
    c_all = jnp.concatenate([c, c_ctx[None, :]], axis=0)
    c_all = jnp.pad(c_all, ((0, (-c_all.shape[0]) % SUBLANES), (0, 0)))
    mod = _modulation(c_all, w_mod[0], b_mod[0][None, :])
    sh1, sc1, g1, sh2, sc2, g2 = [m[:, None, :] for m in jnp.split(mod[:bz], 6, axis=1)]
    mod_c = mod[bz]
    csh1, csc1 = mod_c[None, None, :d], mod_c[None, None, d:2 * d]

    ng1 = norm1_g[0][None, :]
    wts = _inproj_weights(w_in[0], ml_conv_w[0], ml_gate_b[0], gla_lr_w2[0], gla_alpha_b[0])
    ml_lat, gla_lat, mo, gr = _token_mix_inputs(x, sh1, sc1, ng1, wts, GRID_W)
    ml_ctx, gla_ctx, _, _ = _token_mix_inputs(ctx, csh1, csc1, ng1, wts, ctx.shape[1])
    mhf, mhb = _mlstm(*ml_lat, *ml_ctx)
    ghf, ghb = _gla(*gla_lat, *gla_ctx)

    keys = peer_keys[0].reshape(2 * PEER_HEADS, PEER_NKEYS, PEER_HALF).astype(BF16)
    x1, h2t, e1, lim, e2, rk = _mix(
        x, mhf, mhb, ghf, ghb, mo, gr, ml_norm_g[0][None, :], gla_norm_g[0][None, :],
        w_out[0].astype(BF16), g1, sh2, sc2, norm2_g[0][None, :], peer_wq[0].astype(BF16), keys)
    return _peer(h2t, peer_u[0].astype(BF16), peer_v[0].astype(BF16).T, e1, lim, e2, rk,
                 x1, g2, norm_f_g[None, :])
```

```python
import functools

import jax
import jax.numpy as jnp
from jax import lax
from jax.experimental import pallas as pl
from jax.experimental.pallas import tpu as pltpu

F32 = jnp.float32
BF16 = jnp.bfloat16

EPS = 1e-6
D_MODEL = 1024
GRID_W = 64

ML_HEADS = 4
ML_HEAD_DIM = 128
ML_WIDTH = ML_HEADS * ML_HEAD_DIM
ML_CHUNK = 128
GLA_HEADS = 4
GLA_DK = 64
GLA_DV = 128
GLA_KEY_WIDTH = GLA_HEADS * GLA_DK
GLA_WIDTH = GLA_HEADS * GLA_DV
GLA_RANK = 16
GLA_TAU = 16.0
GLA_CHUNK = 64
N_GATES = 4 * ML_HEADS

PEER_HEADS = 8
PEER_NKEYS = 128
PEER_TOPK = 16
PEER_HALF = 128

LANES = 128
SUBLANES = 8
BF16_ROWS = 2 * SUBLANES
VMEM_LIMIT = 56 * 1024 * 1024

INPROJ_TM = 1024
CUMSUM_ROWS = 256
ML_HEADS_PER_STEP = 4
GLA_HEADS_PER_STEP = 4
MIX_TM = 512
PEER_TM = 512
PEER_TE = 2048

NEG_INF = float("-inf")


def _params(*sem):
    return pltpu.CompilerParams(dimension_semantics=sem, vmem_limit_bytes=VMEM_LIMIT)


def _dot(a, b):
    return jnp.dot(a, b, preferred_element_type=F32)


def _dot_nt(a, b):
    return lax.dot_general(a, b, (((1,), (1,)), ((), ())), preferred_element_type=F32)


def _split2(x):
    hi = x.astype(BF16)
    return hi, (x - hi.astype(F32)).astype(BF16)


def _dot_exact_rhs(a01, x):
    hi, lo = _split2(x)
    return _dot(a01, hi) + _dot(a01, lo)


def _dot_exact_lhs(x, a01):
    hi, lo = _split2(x)
    return _dot(hi, a01) + _dot(lo, a01)


def _block_rows(mask, x):
    r = mask.shape[0]
    return jnp.concatenate([_dot_exact_rhs(mask, x[i:i + r]) for i in range(0, x.shape[0], r)], axis=0)


def _block_cols(x, mask):
    r = mask.shape[0]
    return jnp.concatenate([_dot_exact_lhs(x[:, i:i + r], mask) for i in range(0, x.shape[1], r)], axis=1)


def _sigmoid(x):
    return 1.0 / (1.0 + jnp.exp(-x))


def _log_sigmoid(x):
    return jnp.minimum(x, 0.0) - jnp.log(1.0 + jnp.exp(-jnp.abs(x)))


def _rms(x):
    return x * lax.rsqrt(jnp.mean(x * x, axis=-1, keepdims=True) + EPS)


def _mod_kernel(c_ref, w_ref, b_ref, o_ref):
    cond = c_ref[...]
    cond = cond * _sigmoid(cond)
    ch, cl = cond.astype(BF16), (cond - cond.astype(BF16).astype(F32)).astype(BF16)
    w = w_ref[...]
    wh = w.astype(BF16)
    wl = (w - wh.astype(F32)).astype(BF16)
    o_ref[...] = _dot(ch, wh) + _dot(ch, wl) + _dot(cl, wh) + b_ref[...]


def _modulation(c_all, w_mod, b_mod):
    rows, d = c_all.shape
    n_out = w_mod.shape[1]
    tn = 512
    return pl.pallas_call(
        _mod_kernel,
        grid=(n_out // tn,),
        in_specs=[pl.BlockSpec((rows, d), lambda j: (0, 0)),
                  pl.BlockSpec((d, tn), lambda j: (0, j)),
                  pl.BlockSpec((1, tn), lambda j: (0, j))],
        out_specs=pl.BlockSpec((rows, tn), lambda j: (0, j)),
        out_shape=jax.ShapeDtypeStruct((rows, n_out), F32),
        compiler_params=_params("arbitrary"),
        name="mod",
    )(c_all, w_mod, b_mod)


def _chunk_masks(tm, chunk):
    r = lax.broadcasted_iota(jnp.int32, (tm, tm), 0)
    c = lax.broadcasted_iota(jnp.int32, (tm, tm), 1)
    same = (r // chunk) == (c // chunk)
    prefix = jnp.where(same & (c <= r), 1.0, 0.0).astype(BF16)
    suffix = jnp.where(same & (c >= r), 1.0, 0.0).astype(BF16)
    return prefix, suffix


def _inproj_kernel(x_ref, sh_ref, sc_ref, ng_ref, wqk_ref, wv_ref, wo_ref, wg_ref, wgt_ref,
                   wgqk_ref, wgv_ref, wgr_ref, wlr_ref, conv_ref, gb_ref, gbc_ref, w2_ref, ab_ref,
                   mq_ref, mk_ref, mv_ref, mo_ref, gtok_ref, gmaj_ref,
                   gq_ref, gk_ref, gv_ref, gr_ref, bf_ref, bb_ref, mkt_ref, gvt_ref, *, row_len):
    tm = x_ref.shape[1]
    x = x_ref[0]
    h = _rms(x) * ng_ref[...] * (1.0 + sc_ref[0]) + sh_ref[0]
    hb = h.astype(BF16)

    u = _dot(hb, wqk_ref[...])
    pos = lax.broadcasted_iota(jnp.int32, (tm, 1), 0) % row_len
    up = jnp.where(pos == 0, 0.0, pltpu.roll(u, 1, 0))
    dn = jnp.where(pos == row_len - 1, 0.0, pltpu.roll(u, tm - 1, 0))
    cw = conv_ref[...]
    y = cw[0:1] * up + cw[1:2] * u + cw[2:3] * dn
    y = y * _sigmoid(y)
    mq_ref[0] = y[:, :ML_WIDTH].astype(BF16)
    mk = y[:, ML_WIDTH:] * ML_HEAD_DIM ** -0.5
    mk_ref[0] = mk.astype(BF16)
    mkt = mk.T.astype(BF16)
    for hh in range(ML_HEADS):
        for ci in range(tm // ML_CHUNK):
            mkt_ref[0, hh, ci] = mkt[hh * ML_HEAD_DIM:(hh + 1) * ML_HEAD_DIM, ci * ML_CHUNK:(ci + 1) * ML_CHUNK]
    mv_ref[0] = _dot(hb, wv_ref[...]).astype(BF16)
    mo_ref[0] = _dot(hb, wo_ref[...]).astype(BF16)

    pre128, suf128 = _chunk_masks(min(tm, CUMSUM_ROWS), ML_CHUNK)
    g = _dot(hb, wg_ref[...]) + gb_ref[...]
    col = lax.broadcasted_iota(jnp.int32, (1, LANES), 1)
    is_f = ((col % 8) >= 4) & (col < N_GATES)
    is_bwd = col >= 8
    lf = jnp.where(is_f, _log_sigmoid(g), 0.0)
    cum = jnp.where(is_bwd, _block_rows(suf128, lf), _block_rows(pre128, lf))
    gtok_ref[0] = jnp.where(is_f, cum, g)[:, :N_GATES]

    gt = _dot_nt(wgt_ref[...], hb) + gbc_ref[...]
    row = lax.broadcasted_iota(jnp.int32, (N_GATES, 1), 0)
    is_f_r = (row % 8) >= 4
    lft = jnp.where(is_f_r, _log_sigmoid(gt), 0.0)
    cum_t = jnp.where(row >= 8, _block_cols(lft, pre128), _block_cols(lft, suf128))
    gmaj_ref[0] = jnp.where(is_f_r, cum_t, gt)

    gqk = _dot(hb, wgqk_ref[...])
    gq_ref[0] = (gqk[:, :GLA_KEY_WIDTH] * GLA_DK ** -0.5).astype(BF16)
    gk_ref[0] = gqk[:, GLA_KEY_WIDTH:].astype(BF16)
    gv = _dot(hb, wgv_ref[...])
    gv_ref[0] = gv.astype(BF16)
    gvt = gv.T.astype(BF16)
    for hh in range(GLA_HEADS):
        for ci in range(tm // GLA_CHUNK):
            gvt_ref[0, hh, ci] = gvt[hh * GLA_DV:(hh + 1) * GLA_DV, ci * GLA_CHUNK:(ci + 1) * GLA_CHUNK]
    gr_ref[0] = _dot(hb, wgr_ref[...]).astype(BF16)
    glr = _dot(hb, wlr_ref[...]).astype(BF16)
    alpha = _dot(glr, w2_ref[...]) + ab_ref[...]
    la = _log_sigmoid(alpha) * (1.0 / GLA_TAU)
    pre64, suf64 = _chunk_masks(min(tm, CUMSUM_ROWS), GLA_CHUNK)
    bf_ref[0] = _block_rows(pre64, la[:, :GLA_KEY_WIDTH])
    bb_ref[0] = _block_rows(suf64, la[:, GLA_KEY_WIDTH:])


def _inproj(x, sh, sc, ng, wts, row_len):
    bz, n, d = x.shape
    tm = min(INPROJ_TM, n)
    assert n % tm == 0 and tm % row_len == 0 and tm % ML_CHUNK == 0
    grid = (bz, n // tm)
    per_b = (lambda b, i: (b, 0, 0)) if sh.shape[0] == bz else (lambda b, i: (0, 0, 0))

    def full(a):
        return pl.BlockSpec(a.shape, lambda b, i, nd=a.ndim: (0,) * nd)

    def tok(width):
        return pl.BlockSpec((1, tm, width), lambda b, i: (b, i, 0))

    def out(width, dtype):
        return jax.ShapeDtypeStruct((bz, n, width), dtype)

    in_specs = [tok(d), pl.BlockSpec((1, 1, d), per_b), pl.BlockSpec((1, 1, d), per_b), full(ng)]
    in_specs += [full(w) for w in wts]
    out_specs = [tok(ML_WIDTH)] * 4 + [tok(N_GATES), pl.BlockSpec((1, N_GATES, tm), lambda b, i: (b, 0, i))]
    out_specs += [tok(GLA_KEY_WIDTH)] * 2 + [tok(GLA_WIDTH)] * 2 + [tok(GLA_KEY_WIDTH)] * 2
    out_shape = [out(ML_WIDTH, BF16)] * 4 + [out(N_GATES, F32),
                                              jax.ShapeDtypeStruct((bz, N_GATES, n), F32)]
    out_shape += ([out(GLA_KEY_WIDTH, BF16)] * 2 + [out(GLA_WIDTH, BF16)] * 2
                  + [out(GLA_KEY_WIDTH, F32)] * 2)
    for heads, dim, chunk in ((ML_HEADS, ML_HEAD_DIM, ML_CHUNK), (GLA_HEADS, GLA_DV, GLA_CHUNK)):
        out_specs.append(pl.BlockSpec((1, heads, tm // chunk, dim, chunk), lambda b, i: (b, 0, i, 0, 0)))
        out_shape.append(jax.ShapeDtypeStruct((bz, heads, n // chunk, dim, chunk), BF16))
    return pl.pallas_call(
        functools.partial(_inproj_kernel, row_len=row_len),
        grid=grid, in_specs=in_specs, out_specs=out_specs, out_shape=out_shape,
        compiler_params=_params("parallel", "arbitrary"),
        name="inproj",
    )(x, sh, sc, ng, *wts)


def _bmm(a, b):
    return lax.dot_general(a, b, (((2,), (1,)), ((0,), (0,))), preferred_element_type=F32)


def _bmm_nt(a, b):
    return lax.dot_general(a, b, (((2,), (2,)), ((0,), (0,))), preferred_element_type=F32)


def _by_direction(x, fwd, bwd):
    half = x.shape[0] // 2
    return jnp.concatenate([fwd(x[:half]), bwd(x[half:])], axis=0)


def _ml_chunks(q, k, kt, v, gt, gm, masks, state, want_out):
    ct, nv, m = state
    ll = q.shape[1]
    ig_col = _by_direction(gt, lambda g: g[:, :, 0:1], lambda g: g[:, :, 2:3])
    b_col = _by_direction(gt, lambda g: g[:, :, 1:2], lambda g: g[:, :, 3:4])
    ig_row = _by_direction(gm, lambda g: g[:, 0:1, :], lambda g: g[:, 2:3, :])
    b_row = _by_direction(gm, lambda g: g[:, 1:2, :], lambda g: g[:, 3:4, :])
    b_end = _by_direction(b_row, lambda b: b[:, :, ll - 1:ll], lambda b: b[:, :, 0:1])
    h = None
    if want_out:
        d_log = _by_direction(b_col - b_row + ig_row,
                              lambda x: jnp.where(masks[0], x, NEG_INF),
                              lambda x: jnp.where(masks[1], x, NEG_INF))
        inter_log = b_col + m
        m_t = jnp.maximum(inter_log, jnp.max(d_log, axis=2, keepdims=True))
        scores = _bmm_nt(q, k) * jnp.exp(d_log - m_t)
        inter = jnp.exp(inter_log - m_t)
        qf = q.astype(F32)
        num = _bmm(scores.astype(BF16), v) + inter * _bmm(q, ct.astype(BF16))
        den = jnp.sum(scores, axis=2, keepdims=True) + inter * jnp.sum(qf * nv, axis=2, keepdims=True)
        h = num / jnp.maximum(jnp.abs(den), jnp.exp(-m_t))
    m_loc = jnp.max(b_end - b_row + ig_row, axis=2, keepdims=True)
    w_col = jnp.exp(b_end - b_col + ig_col - m_loc)
    c_loc = _bmm(kt, (w_col * v.astype(F32)).astype(BF16))
    n_loc = jnp.sum(w_col * k.astype(F32), axis=1, keepdims=True)
    m_new = jnp.maximum(b_end + m, m_loc)
    a = jnp.exp(b_end + m - m_new)
    bb = jnp.exp(m_loc - m_new)
    return h, (a * ct + bb * c_loc, a * nv + bb * n_loc, m_new)


def _mlstm_kernel(q_ref, k_ref, kt_ref, v_ref, gt_ref, gm_ref,
                  qc_ref, kc_ref, ktc_ref, vc_ref, gtc_ref, gmc_ref, hf_ref, hb_ref):
    ll = ML_CHUNK
    nc = q_ref.shape[1] // ll
    ncc = qc_ref.shape[1] // ll
    r = lax.broadcasted_iota(jnp.int32, (ll, ll), 0)
    c = lax.broadcasted_iota(jnp.int32, (ll, ll), 1)
    masks = (c <= r, c >= r)

    heads = gt_ref.shape[1]
    head_cols = [slice(hh * ML_HEAD_DIM, (hh + 1) * ML_HEAD_DIM) for hh in range(heads)]

    def rows_of(ci):
        return pl.ds(pl.multiple_of(ci * ll, ll), ll)

    def load(refs, cf, cb):
        qr, kr, ktr, vr, gtr, gmr = refs
        sites = [(hh, ci) for ci in (cf, cb) for hh in range(heads)]
        seq = lambda r: jnp.stack([r[0, rows_of(ci), head_cols[hh]] for hh, ci in sites])
        return (seq(qr), seq(kr), jnp.stack([ktr[0, hh, ci] for hh, ci in sites]), seq(vr),
                jnp.stack([gtr[0, hh, rows_of(ci), :] for hh, ci in sites]),
                jnp.stack([gmr[0, hh, ci] for hh, ci in sites]))

    chains = 2 * heads
    zero = (jnp.zeros((chains, ML_HEAD_DIM, ML_HEAD_DIM), F32), jnp.zeros((chains, 1, ML_HEAD_DIM), F32),
            jnp.zeros((chains, 1, 1), F32))
    ctx_refs = (qc_ref, kc_ref, ktc_ref, vc_ref, gtc_ref, gmc_ref)
    lat_refs = (q_ref, k_ref, kt_ref, v_ref, gt_ref, gm_ref)

    def ctx_body(i, state):
        _, state = _ml_chunks(*load(ctx_refs, i, ncc - 1 - i), masks, state, False)
        return state

    def lat_body(i, state):
        cb = nc - 1 - i
        h, state = _ml_chunks(*load(lat_refs, i, cb), masks, state, True)
        for hh in range(heads):
            hf_ref[0, rows_of(i), head_cols[hh]] = h[hh]
            hb_ref[0, rows_of(cb), head_cols[hh]] = h[heads + hh]
        return state

    state = lax.fori_loop(0, ncc, ctx_body, zero)
    lax.fori_loop(0, nc, lat_body, state)


def _mlstm(q, k, kt, v, gt, gm, qc, kc, ktc, vc, gtc, gmc):
    bz, n, _ = q.shape
    ncx = qc.shape[1]
    hps = ML_HEADS_PER_STEP

    def seq(nn):
        return pl.BlockSpec((1, nn, hps * ML_HEAD_DIM), lambda b, h: (b, 0, h))

    def ktr(nn):
        return pl.BlockSpec((1, hps, nn // ML_CHUNK, ML_HEAD_DIM, ML_CHUNK), lambda b, h: (b, h, 0, 0, 0))

    def gtok(nn):
        return pl.BlockSpec((1, hps, nn, 4), lambda b, h: (b, h, 0, 0))

    def gmaj(nn):
        return pl.BlockSpec((1, hps, nn // ML_CHUNK, 4, ML_CHUNK), lambda b, h: (b, h, 0, 0, 0))

    def specs(nn):
        return [seq(nn), seq(nn), ktr(nn), seq(nn), gtok(nn), gmaj(nn)]

    return pl.pallas_call(
        _mlstm_kernel,
        grid=(bz, ML_HEADS // hps),
        in_specs=specs(n) + specs(ncx),
        out_specs=[seq(n)] * 2,
        out_shape=[jax.ShapeDtypeStruct((bz, n, ML_WIDTH), F32)] * 2,
        compiler_params=_params("parallel", "arbitrary"),
        name="mlstm",
    )(q, k, kt, v, gt, gm, qc, kc, ktc, vc, gtc, gmc)


def _gla_chunks(q, k, v, vt, b, masks, st, want_out):
    ll = q.shape[1]
    qf, kf = q.astype(F32), k.astype(F32)
    ref = _by_direction(b, lambda x: x[:, ll // 2:ll // 2 + 1], lambda x: x[:, ll // 2 - 1:ll // 2])
    b_end = _by_direction(b, lambda x: x[:, ll - 1:ll], lambda x: x[:, 0:1])
    o = None
    if want_out:
        att = _bmm_nt((qf * jnp.exp(b - ref)).astype(BF16), (kf * jnp.exp(ref - b)).astype(BF16))
        att = _by_direction(att, lambda x: jnp.where(masks[0], x, 0.0), lambda x: jnp.where(masks[1], x, 0.0))
        o = _bmm(att.astype(BF16), v) + _bmm_nt((qf * jnp.exp(b)).astype(BF16), st.astype(BF16))
    s_loc = _bmm(vt, (kf * jnp.exp(b_end - b)).astype(BF16))
    return o, jnp.exp(b_end) * st + s_loc


def _gla_kernel(q_ref, k_ref, v_ref, vt_ref, bf_ref, bb_ref,
                qc_ref, kc_ref, vc_ref, vtc_ref, bfc_ref, bbc_ref, of_ref, ob_ref):
    ll = GLA_CHUNK
    nc = q_ref.shape[1] // ll
    ncc = qc_ref.shape[1] // ll
    r = lax.broadcasted_iota(jnp.int32, (ll, ll), 0)
    c = lax.broadcasted_iota(jnp.int32, (ll, ll), 1)
    masks = (c <= r, c >= r)

    heads = vt_ref.shape[1]

    def rows_of(ci):
        return pl.ds(pl.multiple_of(ci * ll, ll), ll)

    head_cols = [slice(hh * GLA_DV, (hh + 1) * GLA_DV) for hh in range(heads)]
    key_cols = [slice(hh * GLA_DK, (hh + 1) * GLA_DK) for hh in range(heads)]

    def load(refs, cf, cb):
        qr, kr, vr, vtr, bfr, bbr = refs
        sites = [(hh, ci) for ci in (cf, cb) for hh in range(heads)]
        per_head = lambda r: jnp.stack([r[0, rows_of(ci), key_cols[hh]] for hh, ci in sites])
        return (per_head(qr), per_head(kr),
                jnp.stack([vr[0, rows_of(ci), head_cols[hh]] for hh, ci in sites]),
                jnp.stack([vtr[0, hh, ci] for hh, ci in sites]),
                jnp.stack([bfr[0, rows_of(cf), key_cols[hh]] for hh in range(heads)]
                          + [bbr[0, rows_of(cb), key_cols[hh]] for hh in range(heads)]))

    zero = jnp.zeros((2 * heads, GLA_DV, GLA_DK), F32)
    ctx_refs = (qc_ref, kc_ref, vc_ref, vtc_ref, bfc_ref, bbc_ref)
    lat_refs = (q_ref, k_ref, v_ref, vt_ref, bf_ref, bb_ref)

    def ctx_body(i, st):
        _, st = _gla_chunks(*load(ctx_refs, i, ncc - 1 - i), masks, st, False)
        return st

    def lat_body(i, st):
        cb = nc - 1 - i
        o, st = _gla_chunks(*load(lat_refs, i, cb), masks, st, True)
        for hh in range(heads):
            of_ref[0, rows_of(i), head_cols[hh]] = o[hh]
            ob_ref[0, rows_of(cb), head_cols[hh]] = o[heads + hh]
        return st

    st = lax.fori_loop(0, ncc, ctx_body, zero)
    lax.fori_loop(0, nc, lat_body, st, unroll=2)


def _gla(q, k, v, vt, bf, bb, qc, kc, vc, vtc, bfc, bbc):
    bz, n, _ = q.shape
    ncx = qc.shape[1]
    hps = GLA_HEADS_PER_STEP

    def key(nn):
        return pl.BlockSpec((1, nn, hps * GLA_DK), lambda b, h: (b, 0, h))

    def val(nn):
        return pl.BlockSpec((1, nn, hps * GLA_DV), lambda b, h: (b, 0, h))

    def valt(nn):
        return pl.BlockSpec((1, hps, nn // GLA_CHUNK, GLA_DV, GLA_CHUNK), lambda b, h: (b, h, 0, 0, 0))

    def specs(nn):
        return [key(nn), key(nn), val(nn), valt(nn), key(nn), key(nn)]

    return pl.pallas_call(
        _gla_kernel,
        grid=(bz, GLA_HEADS // hps),
        in_specs=specs(n) + specs(ncx),
        out_specs=[val(n)] * 2,
        out_shape=[jax.ShapeDtypeStruct((bz, n, GLA_WIDTH), F32)] * 2,
        compiler_params=_params("parallel", "arbitrary"),
        name="gla",
    )(q, k, v, vt, bf, bb, qc, kc, vc, vtc, bfc, bbc)


def _head_rms(h, g, width):
    parts = [_rms(h[:, i:i + width]) for i in range(0, h.shape[1], width)]
    return jnp.concatenate(parts, axis=1) * g


def _oddeven_merge(lo, hi, r):
    step = r * 2
    if step < hi - lo:
        yield from _oddeven_merge(lo, hi, step)
        yield from _oddeven_merge(lo + r, hi, step)
        yield from [(i, i + r) for i in range(lo + r, hi - r, step)]
    else:
        yield (lo, lo + r)


def _oddeven_merge_sort(lo, hi):
    if hi - lo >= 1:
        mid = lo + (hi - lo) // 2
        yield from _oddeven_merge_sort(lo, mid)
        yield from _oddeven_merge_sort(mid + 1, hi)
        yield from _oddeven_merge(lo, hi, 1)


def _exchange(v, i, j):
    v[i], v[j] = jnp.maximum(v[i], v[j]), jnp.minimum(v[i], v[j])


def _merge_top16(v, shift):
    other = [pltpu.roll(x, shift, 0) for x in v]
    out = []
    for j in range(PEER_TOPK):
        a = v[j] if j < len(v) else None
        b = other[PEER_TOPK - 1 - j] if PEER_TOPK - 1 - j < len(v) else None
        out.append(a if b is None else b if a is None else jnp.maximum(a, b))
    return out


def _sort_bitonic16(v):
    v = list(v)
    for d in (8, 4, 2, 1):
        for j in range(PEER_TOPK):
            if not j & d:
                _exchange(v, j, j + d)
    return v


def _sixteenth_largest(blocks):
    v = list(blocks)
    assert SUBLANES <= len(v) <= PEER_TOPK
    for i, j in _oddeven_merge_sort(0, PEER_TOPK - 1):
        if j < len(v):
            _exchange(v, i, j)
    v = _sort_bitonic16(_merge_top16(v, 4))
    v = _sort_bitonic16(_merge_top16(v, 2))
    v = _merge_top16(v, 1)
    return functools.reduce(jnp.minimum, v)[0:1]


def _sorted_top16(tiles):
    v = list(tiles)
    n = len(v)
    for i, j in _oddeven_merge_sort(0, n - 1):
        _exchange(v, i, j)
    for shift in (4, 2, 1):
        v = _sort_bitonic16(_merge_top16(v, shift))
    sub = lax.broadcasted_iota(jnp.int32, v[0].shape, 0)
    halves = []
    for lo in (0, SUBLANES):
        rows = v[lo]
        for r in range(1, SUBLANES):
            rows = jnp.where(sub == r, v[lo + r], rows)
        halves.append(rows)
    repeated = jnp.zeros_like(v[0])
    for r in range(n - 1):
        repeated = jnp.where(v[r] == v[r + 1], 1.0, repeated)
    return jnp.concatenate(halves, axis=0), repeated[0:1]


def _prefix_count(x, thr, test):
    r = [thr[i:i + 1] for i in range(PEER_TOPK)]
    sel = jnp.where
    t8 = test(x, r[7])
    t4 = test(x, sel(t8, r[11], r[3]))
    t2 = test(x, sel(t8, sel(t4, r[13], r[9]), sel(t4, r[5], r[1])))
    t1 = test(x, sel(t8, sel(t4, sel(t2, r[14], r[12]), sel(t2, r[10], r[8])),
                     sel(t4, sel(t2, r[6], r[4]), sel(t2, r[2], r[0]))))
    count = sel(t8, 8.0, 0.0) + sel(t4, 4.0, 0.0) + sel(t2, 2.0, 0.0) + sel(t1, 1.0, 0.0)
    return sel(test(x, r[15]), float(PEER_TOPK), count)


def _count(mask):
    return jnp.sum(jnp.where(mask, 1.0, 0.0), axis=0, keepdims=True)


def _top16(s):
    nk, tl = s.shape
    slot = lax.broadcasted_iota(jnp.int32, (PEER_TOPK, tl), 0)
    key_id = lax.broadcasted_iota(jnp.int32, (nk, tl), 0).astype(F32)

    def body(r, carry):
        s, rank, vals = carry
        m = jnp.max(s, axis=0, keepdims=True)
        first = jnp.min(jnp.where(s == m, key_id, float(nk)), axis=0, keepdims=True)
        sel = key_id == first
        rank = jnp.where(sel, jnp.asarray(r).astype(F32), rank)
        s = jnp.where(sel, NEG_INF, s)
        vals = jnp.where(slot == r, m, vals)
        return s, rank, vals

    init = (s, jnp.full((nk, tl), float(PEER_TOPK), F32), jnp.zeros((PEER_TOPK, tl), F32))
    _, rank, vals = lax.fori_loop(0, PEER_TOPK, body, init)
    return vals, rank


def _pair_candidates(v1, v2):
    blocks = [v1[0:1] + v2[0:8], v1[0:1] + v2[8:16]]
    blocks += [v1[a:a + 1] + v2[0:8] for a in range(1, 8)]
    blocks += [v1[8:16] + v2[0:1]]
    return jnp.concatenate(blocks, axis=0)


def _blocks(cand):
    return [cand[i:i + SUBLANES] for i in range(0, cand.shape[0], SUBLANES)]


def _pick16(cand):
    i = lax.broadcasted_iota(jnp.int32, cand.shape, 0)
    blk, rr = i // 8, i % 8
    flat = jnp.where(blk == 0, rr, jnp.where(blk == 1, 8 + rr,
                     jnp.where(blk <= 8, (blk - 1) * 16 + rr, (8 + rr) * 16))).astype(F32)

    def body(_, carry):
        cand, picked = carry
        m = jnp.max(cand, axis=0, keepdims=True)
        first = jnp.min(jnp.where(cand == m, flat, 1e9), axis=0, keepdims=True)
        sel = flat == first
        return jnp.where(sel, NEG_INF, cand), jnp.where(sel, 1.0, picked)

    _, picked = lax.fori_loop(0, PEER_TOPK, body, (cand, jnp.zeros_like(cand)))
    return picked


def _first_key_thresholds(picked, v1):
    inf = jnp.inf
    lo = jnp.where(picked[0:8] > 0.0, v1[0:1], inf)
    for a in range(1, 8):
        lo = jnp.minimum(lo, jnp.where(picked[8 * (a + 1):8 * (a + 2)] > 0.0, v1[a:a + 1], inf))
    tail = jnp.min(jnp.where(picked[72:80] > 0.0, v1[8:16], inf), axis=0, keepdims=True)
    row = lax.broadcasted_iota(jnp.int32, lo.shape, 0)
    lo = jnp.minimum(lo, jnp.where(row == 0, tail, inf))
    hi = jnp.where(picked[8:16] > 0.0, v1[0:1], inf)
    return jnp.concatenate([lo, hi], axis=0)


def _pair_counts(cand, picked):
    z = jnp.sum(picked * jnp.exp(cand - cand[0:1]), axis=0, keepdims=True)
    n_rows = [jnp.sum(picked[0:16], axis=0, keepdims=True)]
    n_rows += [jnp.sum(picked[8 * (a + 1):8 * (a + 2)], axis=0, keepdims=True) for a in range(1, 8)]
    return jnp.concatenate(n_rows + [picked[72:80]], axis=0), z


def _mix_kernel(x_ref, mhf_ref, mhb_ref, ghf_ref, ghb_ref, mo_ref, gr_ref, mlg_ref, glag_ref,
                wout_ref, g1_ref, sh2_ref, sc2_ref, n2g_ref, wq_ref, keys_ref,
                x1_ref, h2t_ref, e1_ref, l_ref, e2_ref, rk_ref, q_scr, s_scr):
    ml = _head_rms(mhf_ref[0] + mhb_ref[0], mlg_ref[...], ML_HEAD_DIM)
    ml = _sigmoid(mo_ref[0].astype(F32)) * ml
    gl = _head_rms(ghf_ref[0] + ghb_ref[0], glag_ref[...], GLA_DV)
    gr = gr_ref[0].astype(F32)
    gl = gr * _sigmoid(gr) * gl
    mix = jnp.concatenate([ml, gl], axis=1).astype(BF16)
    x1 = x_ref[0] + g1_ref[0] * _dot(mix, wout_ref[...])
    x1_ref[0] = x1
    h2 = _rms(x1) * n2g_ref[...] * (1.0 + sc2_ref[0]) + sh2_ref[0]
    h2t_ref[...] = h2.T.astype(BF16)
    qall = _dot(h2.astype(BF16), wq_ref[...]).astype(BF16)
    for j in range(2 * PEER_HEADS):
        q_scr[j] = qall[:, j * PEER_HALF:(j + 1) * PEER_HALF]

    tm = x_ref.shape[1]
    topk = float(PEER_TOPK)
    lane_chunks = [slice(i, i + LANES) for i in range(0, tm, LANES)]

    def write_tables(p, lanes, s1, s2, v1, v2, z, lim, rank2):
        e1_ref[p, :, lanes] = jnp.exp(s1 - v1[0:1]) * (0.5 / z)
        l_ref[p, :, lanes] = lim
        e2 = jnp.exp(s2 - v2[0:1]).astype(BF16)
        e2_ref[p, :, :, lanes] = e2.reshape(PEER_NKEYS // BF16_ROWS, BF16_ROWS, LANES)
        rk_ref[p, :, :, lanes] = rank2.astype(BF16).reshape(PEER_NKEYS // BF16_ROWS, BF16_ROWS, LANES)

    def head_body(p, carry):
        for hf in range(2):
            s_scr[hf] = _dot_nt(keys_ref[2 * p + hf], q_scr[2 * p + hf])
        excess = jnp.zeros((1, LANES), F32)
        vals = []
        for hf in range(2):
            for lanes in lane_chunks:
                tiles = [s_scr[hf, i:i + SUBLANES, lanes] for i in range(0, PEER_NKEYS, SUBLANES)]
                top, repeated = _sorted_top16(tiles)
                vals.append(top)
                excess = jnp.maximum(excess, repeated)
        v1s, v2s = vals[:len(lane_chunks)], vals[len(lane_chunks):]
        cands = [_pair_candidates(v1, v2) for v1, v2 in zip(v1s, v2s)]
        taus = [_sixteenth_largest(_blocks(c)) for c in cands]
        for lanes, v1, v2, cand, tau in zip(lane_chunks, v1s, v2s, cands, taus):
            s1, s2 = s_scr[0, :, lanes], s_scr[1, :, lanes]
            picked = jnp.where(cand >= tau, 1.0, 0.0)
            _, z = _pair_counts(cand, picked)
            theta = _first_key_thresholds(picked, v1)
            lim = _prefix_count(s1, theta, lambda x, t: x >= t)
            rank2 = _prefix_count(s2, v2, lambda x, t: x < t)
            write_tables(p, lanes, s1, s2, v1, v2, z, lim, rank2)
            n_marked = (_count(s1 >= v1[PEER_TOPK - 1:PEER_TOPK]) + _count(rank2 < topk)
                        + jnp.sum(picked, axis=0, keepdims=True))
            excess = jnp.maximum(excess, n_marked - 3.0 * topk)

        @pl.when(jnp.max(excess) > 0.0)
        def _():
            for lanes in lane_chunks:
                s1, s2 = s_scr[0, :, lanes], s_scr[1, :, lanes]
                v1, rank1 = _top16(s1)
                v2, rank2 = _top16(s2)
                cand = _pair_candidates(v1, v2)
                n, z = _pair_counts(cand, _pick16(cand))
                lim = jnp.zeros_like(rank1)
                for a in range(PEER_TOPK):
                    lim = jnp.where(rank1 == float(a), n[a:a + 1], lim)
                write_tables(p, lanes, s1, s2, v1, v2, z, lim, rank2)

        return carry

    lax.fori_loop(0, PEER_HEADS, head_body, 0)


def _mix(x, mhf, mhb, ghf, ghb, mo, gr, mlg, glag, wout, g1, sh2, sc2, n2g, wq, keys):
    bz, n, d = x.shape
    tm = min(MIX_TM, n)
    assert n % tm == 0 and tm % LANES == 0
    nt = n // tm
    t_all = bz * n

    def tok(width):
        return pl.BlockSpec((1, tm, width), lambda b, i: (b, i, 0))

    def full(a):
        return pl.BlockSpec(a.shape, lambda b, i, nd=a.ndim: (0,) * nd)

    vec = pl.BlockSpec((1, 1, d), lambda b, i: (b, 0, 0))
    tab1 = pl.BlockSpec((PEER_HEADS, PEER_NKEYS, tm), lambda b, i: (0, 0, b * nt + i))
    tab2 = pl.BlockSpec((PEER_HEADS, PEER_NKEYS // BF16_ROWS, BF16_ROWS, tm),
                        lambda b, i: (0, 0, 0, b * nt + i))
    tab1_shape = jax.ShapeDtypeStruct((PEER_HEADS, PEER_NKEYS, t_all), F32)
    tab2_shape = jax.ShapeDtypeStruct((PEER_HEADS, PEER_NKEYS // BF16_ROWS, BF16_ROWS, t_all), BF16)
    return pl.pallas_call(
        _mix_kernel,
        grid=(bz, nt),
        in_specs=[tok(d)] + [tok(ML_WIDTH)] * 6 + [full(mlg), full(glag), full(wout), vec, vec, vec,
                                                    full(n2g), full(wq), full(keys)],
        out_specs=[tok(d), pl.BlockSpec((d, tm), lambda b, i: (0, b * nt + i)), tab1, tab1, tab2, tab2],
        out_shape=[jax.ShapeDtypeStruct((bz, n, d), F32), jax.ShapeDtypeStruct((d, t_all), BF16),
                   tab1_shape, tab1_shape, tab2_shape, tab2_shape],
        scratch_shapes=[pltpu.VMEM((2 * PEER_HEADS, tm, PEER_HALF), BF16),
                        pltpu.VMEM((2, PEER_NKEYS, tm), F32)],
        compiler_params=_params("parallel", "arbitrary"),
        name="mix",
    )(x, mhf, mhb, ghf, ghb, mo, gr, mlg, glag, wout, g1, sh2, sc2, n2g, wq, keys)


def _peer_kernel(h2t_ref, u_ref, vt_ref, e1_ref, l_ref, e2_ref, rk_ref, x1_ref, g2_ref, nfg_ref,
                 out_ref, acc_ref):
    j = pl.program_id(2)
    te, tm = u_ref.shape[0], h2t_ref.shape[1]
    rows_per_step = te // PEER_NKEYS
    tiles = PEER_NKEYS // BF16_ROWS

    @pl.when(j == 0)
    def _():
        acc_ref[...] = jnp.zeros_like(acc_ref)

    act = _dot(u_ref[...], h2t_ref[...])
    act = act.astype(BF16)
    act = (act * (1.0 + lax.erf(act * 2.0 ** -0.5))).reshape(rows_per_step, tiles, BF16_ROWS, tm)
    parts = []
    for i in range(rows_per_step):
        w = None
        for p in range(PEER_HEADS):
            e1 = jnp.broadcast_to(e1_ref[p, i:i + 1, :], (BF16_ROWS, tm)).astype(BF16)[None]
            lim = jnp.broadcast_to(l_ref[p, i:i + 1, :], (BF16_ROWS, tm)).astype(BF16)[None]
            term = jnp.where(rk_ref[p] < lim, e2_ref[p] * e1, jnp.zeros((), BF16))
            w = term if w is None else w + term
        parts.append(w * act[i])
    gated = jnp.concatenate(parts, axis=0).reshape(te, tm)
    acc_ref[...] += _dot(vt_ref[...], gated)

    @pl.when(j == pl.num_programs(2) - 1)
    def _():
        xf = x1_ref[0] + g2_ref[0] * acc_ref[...].T
        out_ref[0] = _rms(xf) * nfg_ref[...]


def _peer(h2t, u, vt, e1, lim, e2, rk, x1, g2, nfg):
    bz, n, d = x1.shape
    n_exp = u.shape[0]
    tm = min(PEER_TM, n)
    te = PEER_TE
    assert n % tm == 0 and n_exp % te == 0 and te % PEER_NKEYS == 0
    nt = n // tm
    tab1 = pl.BlockSpec((PEER_HEADS, te // PEER_NKEYS, tm), lambda b, i, j: (0, j, b * nt + i))
    tab2 = pl.BlockSpec((PEER_HEADS, PEER_NKEYS // BF16_ROWS, BF16_ROWS, tm),
                        lambda b, i, j: (0, 0, 0, b * nt + i))
    tok = pl.BlockSpec((1, tm, d), lambda b, i, j: (b, i, 0))
    return pl.pallas_call(
        _peer_kernel,
        grid=(bz, nt, n_exp // te),
        in_specs=[pl.BlockSpec((d, tm), lambda b, i, j: (0, b * nt + i)),
                  pl.BlockSpec((te, d), lambda b, i, j: (j, 0)),
                  pl.BlockSpec((d, te), lambda b, i, j: (0, j)),
                  tab1, tab1, tab2, tab2, tok,
                  pl.BlockSpec((1, 1, d), lambda b, i, j: (b, 0, 0)),
                  pl.BlockSpec((1, d), lambda b, i, j: (0, 0))],
        out_specs=tok,
        out_shape=jax.ShapeDtypeStruct((bz, n, d), F32),
        scratch_shapes=[pltpu.VMEM((d, tm), F32)],
        compiler_params=_params("parallel", "parallel", "arbitrary"),
        name="peer",
    )(h2t, u, vt, e1, lim, e2, rk, x1, g2, nfg)


def _inproj_weights(w_in, conv_w, gate_b, lr_w2, alpha_b):
    widths = (ML_WIDTH, ML_WIDTH, ML_WIDTH, ML_WIDTH, N_GATES,
              GLA_KEY_WIDTH, GLA_KEY_WIDTH, GLA_WIDTH, GLA_WIDTH, 2 * GLA_RANK)
    offs = [0]
    for w in widths:
        offs.append(offs[-1] + w)
    wb = w_in.astype(BF16)
    col = lambda a, b: wb[:, offs[a]:offs[b]]
    pad_lanes = lambda a: jnp.pad(a, ((0, 0), (0, LANES - a.shape[1])))
    w2 = jnp.zeros((LANES, 2 * GLA_KEY_WIDTH), F32)
    w2 = w2.at[:GLA_RANK, :GLA_KEY_WIDTH].set(lr_w2[0])
    w2 = w2.at[GLA_RANK:2 * GLA_RANK, GLA_KEY_WIDTH:].set(lr_w2[1])
    return (col(0, 2), col(2, 3), col(3, 4), pad_lanes(col(4, 5)), col(4, 5).T,
            col(5, 7), col(7, 8), col(8, 9), pad_lanes(col(9, 10)),
            conv_w, pad_lanes(gate_b[None, :]), gate_b[:, None],
            w2.astype(BF16), alpha_b.reshape(1, 2 * GLA_KEY_WIDTH))


def _per_head_gates(gtok, gmaj):
    bz, n, _ = gtok.shape
    gt = gtok.reshape(bz, n, 2, 2, ML_HEADS).transpose(0, 4, 1, 2, 3).reshape(bz, ML_HEADS, n, 4)
    gm = gmaj.reshape(bz, 2, 2, ML_HEADS, n // ML_CHUNK, ML_CHUNK)
    gm = gm.transpose(0, 3, 4, 1, 2, 5).reshape(bz, ML_HEADS, n // ML_CHUNK, 4, ML_CHUNK)
    return gt, gm


def _token_mix_inputs(x, sh, sc, ng, wts, row_len):
    mq, mk, mv, mo, gtok, gmaj, gq, gk, gv, gr, bf, bb, mkt, gvt = _inproj(x, sh, sc, ng, wts, row_len)
    gt, gm = _per_head_gates(gtok, gmaj)
    return (mq, mk, mkt, mv, gt, gm), (gq, gk, gv, gvt, bf, bb), mo, gr


def kernel(x, c, ctx, c_ctx, w_mod, b_mod, norm1_g, w_in, ml_conv_w, ml_gate_b, ml_norm_g,
           gla_lr_w2, gla_alpha_b, gla_norm_g, w_out, norm2_g, peer_wq, peer_keys, peer_u,
           peer_v, norm_f_g):
    assert w_mod.shape[0] == 1, "single trunk layer"
    bz, n, d = x.shape
    c_all = jnp.concatenate([c, c_ctx[None, :]], axis=0)
    c_all = jnp.pad(c_all, ((0, (-c_all.shape[0]) % SUBLANES), (0, 0)))
    mod = _modulation(c_all, w_mod[0], b_mod[0][None, :])
    sh1, sc1, g1, sh2, sc2, g2 = [m[:, None, :] for m in jnp.split(mod[:bz], 6, axis=1)]
    mod_c = mod[bz]
    csh1, csc1 = mod_c[None, None, :d], mod_c[None, None, d:2 * d]

    ng1 = norm1_g[0][None, :]
    wts = _inproj_weights(w_in[0], ml_conv_w[0], ml_gate_b[0], gla_lr_w2[0], gla_alpha_b[0])
    ml_lat, gla_lat, mo, gr = _token_mix_inputs(x, sh1, sc1, ng1, wts, GRID_W)
    ml_ctx, gla_ctx, _, _ = _token_mix_inputs(ctx, csh1, csc1, ng1, wts, ctx.shape[1])
    mhf, mhb = _mlstm(*ml_lat, *ml_ctx)
    ghf, ghb = _gla(*gla_lat, *gla_ctx)

    keys = peer_keys[0].reshape(2 * PEER_HEADS, PEER_NKEYS, PEER_HALF).astype(BF16)
    x1, h2t, e1, lim, e2, rk = _mix(
        x, mhf, mhb, ghf, ghb, mo, gr, ml_norm_g[0][None, :], gla_norm_g[0][None, :],
        w_out[0].astype(BF16), g1, sh2, sc2, norm2_g[0][None, :], peer_wq[0].astype(BF16), keys)
    return _peer(h2t, peer_u[0].astype(BF16), peer_v[0].astype(BF16).T, e1, lim, e2, rk,
                 x1, g2, norm_f_g[None, :])
```

```python
import functools

import jax
import jax.numpy as jnp
from jax import lax
from jax.experimental import pallas as pl
from jax.experimental.pallas import tpu as pltpu

F32 = jnp.float32
BF16 = jnp.bfloat16

EPS = 1e-6
D_MODEL = 1024
GRID_W = 64

ML_HEADS = 4
ML_HEAD_DIM = 128
ML_WIDTH = ML_HEADS * ML_HEAD_DIM
ML_CHUNK = 128
GLA_HEADS = 4
GLA_DK = 64
GLA_DV = 128
GLA_KEY_WIDTH = GLA_HEADS * GLA_DK
GLA_WIDTH = GLA_HEADS * GLA_DV
GLA_RANK = 16
GLA_TAU = 16.0
GLA_CHUNK = 64
N_GATES = 4 * ML_HEADS

PEER_HEADS = 8
PEER_NKEYS = 128
PEER_TOPK = 16
PEER_HALF = 128

LANES = 128
SUBLANES = 8
BF16_ROWS = 2 * SUBLANES
VMEM_LIMIT = 56 * 1024 * 1024

INPROJ_TM = 1024
CUMSUM_ROWS = 256
ML_HEADS_PER_STEP = 4
GLA_HEADS_PER_STEP = 4
MIX_TM = 512
PEER_TM = 512
PEER_TE = 2048

NEG_INF = float("-inf")


def _params(*sem):
    return pltpu.CompilerParams(dimension_semantics=sem, vmem_limit_bytes=VMEM_LIMIT)


def _dot(a, b):
    return jnp.dot(a, b, preferred_element_type=F32)


def _dot_nt(a, b):
    return lax.dot_general(a, b, (((1,), (1,)), ((), ())), preferred_element_type=F32)


def _split2(x):
    hi = x.astype(BF16)
    return hi, (x - hi.astype(F32)).astype(BF16)


def _dot_exact_rhs(a01, x):
    hi, lo = _split2(x)
    return _dot(a01, hi) + _dot(a01, lo)


def _dot_exact_lhs(x, a01):
    hi, lo = _split2(x)
    return _dot(hi, a01) + _dot(lo, a01)


def _block_rows(mask, x):
    r = mask.shape[0]
    return jnp.concatenate([_dot_exact_rhs(mask, x[i:i + r]) for i in range(0, x.shape[0], r)], axis=0)


def _block_cols(x, mask):
    r = mask.shape[0]
    return jnp.concatenate([_dot_exact_lhs(x[:, i:i + r], mask) for i in range(0, x.shape[1], r)], axis=1)


def _sigmoid(x):
    return 1.0 / (1.0 + jnp.exp(-x))


def _log_sigmoid(x):
    return jnp.minimum(x, 0.0) - jnp.log(1.0 + jnp.exp(-jnp.abs(x)))


def _rms(x):
    return x * lax.rsqrt(jnp.mean(x * x, axis=-1, keepdims=True) + EPS)


def _mod_kernel(c_ref, w_ref, b_ref, o_ref):
    cond = c_ref[...]
    cond = cond * _sigmoid(cond)
    ch, cl = cond.astype(BF16), (cond - cond.astype(BF16).astype(F32)).astype(BF16)
    w = w_ref[...]
    wh = w.astype(BF16)
    wl = (w - wh.astype(F32)).astype(BF16)
    o_ref[...] = _dot(ch, wh) + _dot(ch, wl) + _dot(cl, wh) + b_ref[...]


def _modulation(c_all, w_mod, b_mod):
    rows, d = c_all.shape
    n_out = w_mod.shape[1]
    tn = 512
    return pl.pallas_call(
        _mod_kernel,
        grid=(n_out // tn,),
        in_specs=[pl.BlockSpec((rows, d), lambda j: (0, 0)),
                  pl.BlockSpec((d, tn), lambda j: (0, j)),
                  pl.BlockSpec((1, tn), lambda j: (0, j))],
        out_specs=pl.BlockSpec((rows, tn), lambda j: (0, j)),
        out_shape=jax.ShapeDtypeStruct((rows, n_out), F32),
        compiler_params=_params("arbitrary"),
        name="mod",
    )(c_all, w_mod, b_mod)


def _chunk_masks(tm, chunk):
    r = lax.broadcasted_iota(jnp.int32, (tm, tm), 0)
    c = lax.broadcasted_iota(jnp.int32, (tm, tm), 1)
    same = (r // chunk) == (c // chunk)
    prefix = jnp.where(same & (c <= r), 1.0, 0.0).astype(BF16)
    suffix = jnp.where(same & (c >= r), 1.0, 0.0).astype(BF16)
    return prefix, suffix


def _inproj_kernel(x_ref, sh_ref, sc_ref, ng_ref, wqk_ref, wv_ref, wo_ref, wg_ref, wgt_ref,
                   wgqk_ref, wgv_ref, wgr_ref, wlr_ref, conv_ref, gb_ref, gbc_ref, w2_ref, ab_ref,
                   *out_refs, row_len, with_queries):
    names = [n for n in _INPROJ_OUTS if with_queries or n not in _INPROJ_QUERY_OUTS]
    (mk_ref, mv_ref, gtok_ref, gmaj_ref, gk_ref, gv_ref, bf_ref, bb_ref, mkt_ref, gvt_ref) = [
        r for n, r in zip(names, out_refs) if n not in _INPROJ_QUERY_OUTS]
    tm = x_ref.shape[1]
    x = x_ref[0]
    h = _rms(x) * ng_ref[...] * (1.0 + sc_ref[0]) + sh_ref[0]
    hb = h.astype(BF16)

    k_cols = slice(0 if with_queries else ML_WIDTH, 2 * ML_WIDTH)
    u = _dot(hb, wqk_ref[:, k_cols])
    pos = lax.broadcasted_iota(jnp.int32, (tm, 1), 0) % row_len
    up = jnp.where(pos == 0, 0.0, pltpu.roll(u, 1, 0))
    dn = jnp.where(pos == row_len - 1, 0.0, pltpu.roll(u, tm - 1, 0))
    cw = conv_ref[:, k_cols]
    y = cw[0:1] * up + cw[1:2] * u + cw[2:3] * dn
    y = y * _sigmoid(y)
    if with_queries:
        out_refs[names.index("mq")][0] = y[:, :ML_WIDTH].astype(BF16)
    mk = y[:, -ML_WIDTH:] * ML_HEAD_DIM ** -0.5
    mk_ref[0] = mk.astype(BF16)
    mkt = mk.T.astype(BF16)
    for hh in range(ML_HEADS):
        for ci in range(tm // ML_CHUNK):
            mkt_ref[0, hh, ci] = mkt[hh * ML_HEAD_DIM:(hh + 1) * ML_HEAD_DIM, ci * ML_CHUNK:(ci + 1) * ML_CHUNK]
    mv_ref[0] = _dot(hb, wv_ref[...]).astype(BF16)
    if with_queries:
        out_refs[names.index("mo")][0] = _dot(hb, wo_ref[...]).astype(BF16)

    pre128, suf128 = _chunk_masks(min(tm, CUMSUM_ROWS), ML_CHUNK)
    g = _dot(hb, wg_ref[...]) + gb_ref[...]
    col = lax.broadcasted_iota(jnp.int32, (1, LANES), 1)
    is_f = ((col % 8) >= 4) & (col < N_GATES)
    is_bwd = col >= 8
    lf = jnp.where(is_f, _log_sigmoid(g), 0.0)
    cum = jnp.where(is_bwd, _block_rows(suf128, lf), _block_rows(pre128, lf))
    gtok_ref[0] = jnp.where(is_f, cum, g)[:, :N_GATES]

    gt = _dot_nt(wgt_ref[...], hb) + gbc_ref[...]
    row = lax.broadcasted_iota(jnp.int32, (N_GATES, 1), 0)
    is_f_r = (row % 8) >= 4
    lft = jnp.where(is_f_r, _log_sigmoid(gt), 0.0)
    cum_t = jnp.where(row >= 8, _block_cols(lft, pre128), _block_cols(lft, suf128))
    gmaj_ref[0] = jnp.where(is_f_r, cum_t, gt)

    gqk = _dot(hb, wgqk_ref[:, slice(0 if with_queries else GLA_KEY_WIDTH, 2 * GLA_KEY_WIDTH)])
    if with_queries:
        out_refs[names.index("gq")][0] = (gqk[:, :GLA_KEY_WIDTH] * GLA_DK ** -0.5).astype(BF16)
    gk_ref[0] = gqk[:, -GLA_KEY_WIDTH:].astype(BF16)
    gv = _dot(hb, wgv_ref[...])
    gv_ref[0] = gv.astype(BF16)
    gvt = gv.T.astype(BF16)
    for hh in range(GLA_HEADS):
        for ci in range(tm // GLA_CHUNK):
            gvt_ref[0, hh, ci] = gvt[hh * GLA_DV:(hh + 1) * GLA_DV, ci * GLA_CHUNK:(ci + 1) * GLA_CHUNK]
    if with_queries:
        out_refs[names.index("gr")][0] = _dot(hb, wgr_ref[...]).astype(BF16)
    glr = _dot(hb, wlr_ref[...]).astype(BF16)
    alpha = _dot(glr, w2_ref[...]) + ab_ref[...]
    la = _log_sigmoid(alpha) * (1.0 / GLA_TAU)
    pre64, suf64 = _chunk_masks(min(tm, CUMSUM_ROWS), GLA_CHUNK)
    bf_ref[0] = _block_rows(pre64, la[:, :GLA_KEY_WIDTH])
    bb_ref[0] = _block_rows(suf64, la[:, GLA_KEY_WIDTH:])


_INPROJ_OUTS = ("mq", "mk", "mv", "mo", "gtok", "gmaj", "gq", "gk", "gv", "gr", "bf", "bb", "mkt", "gvt")
_INPROJ_QUERY_OUTS = ("mq", "mo", "gq", "gr")


def _inproj(x, sh, sc, ng, wts, row_len, with_queries):
    bz, n, d = x.shape
    tm = min(INPROJ_TM, n)
    assert n % tm == 0 and tm % row_len == 0 and tm % ML_CHUNK == 0
    grid = (bz, n // tm)
    per_b = (lambda b, i: (b, 0, 0)) if sh.shape[0] == bz else (lambda b, i: (0, 0, 0))

    def full(a):
        return pl.BlockSpec(a.shape, lambda b, i, nd=a.ndim: (0,) * nd)

    def tok(width):
        return pl.BlockSpec((1, tm, width), lambda b, i: (b, i, 0))

    def out(width, dtype):
        return jax.ShapeDtypeStruct((bz, n, width), dtype)

    in_specs = [tok(d), pl.BlockSpec((1, 1, d), per_b), pl.BlockSpec((1, 1, d), per_b), full(ng)]
    in_specs += [full(w) for w in wts]
    out_specs = [tok(ML_WIDTH)] * 4 + [tok(N_GATES), pl.BlockSpec((1, N_GATES, tm), lambda b, i: (b, 0, i))]
    out_specs += [tok(GLA_KEY_WIDTH)] * 2 + [tok(GLA_WIDTH)] * 2 + [tok(GLA_KEY_WIDTH)] * 2
    out_shape = [out(ML_WIDTH, BF16)] * 4 + [out(N_GATES, F32),
                                              jax.ShapeDtypeStruct((bz, N_GATES, n), F32)]
    out_shape += ([out(GLA_KEY_WIDTH, BF16)] * 2 + [out(GLA_WIDTH, BF16)] * 2
                  + [out(GLA_KEY_WIDTH, F32)] * 2)
    for heads, dim, chunk in ((ML_HEADS, ML_HEAD_DIM, ML_CHUNK), (GLA_HEADS, GLA_DV, GLA_CHUNK)):
        out_specs.append(pl.BlockSpec((1, heads, tm // chunk, dim, chunk), lambda b, i: (b, 0, i, 0, 0)))
        out_shape.append(jax.ShapeDtypeStruct((bz, heads, n // chunk, dim, chunk), BF16))
    keep = [i for i, nm in enumerate(_INPROJ_OUTS) if with_queries or nm not in _INPROJ_QUERY_OUTS]
    outs = pl.pallas_call(
        functools.partial(_inproj_kernel, row_len=row_len, with_queries=with_queries),
        grid=grid, in_specs=in_specs, out_specs=[out_specs[i] for i in keep],
        out_shape=[out_shape[i] for i in keep],
        compiler_params=_params("parallel", "arbitrary"),
        name="inproj",
    )(x, sh, sc, ng, *wts)
    return {_INPROJ_OUTS[i]: o for i, o in zip(keep, outs)}


def _bmm(a, b):
    return lax.dot_general(a, b, (((2,), (1,)), ((0,), (0,))), preferred_element_type=F32)


def _bmm_nt(a, b):
    return lax.dot_general(a, b, (((2,), (2,)), ((0,), (0,))), preferred_element_type=F32)


def _by_direction(x, fwd, bwd):
    half = x.shape[0] // 2
    return jnp.concatenate([fwd(x[:half]), bwd(x[half:])], axis=0)


def _ml_chunks(q, k, kt, v, gt, gm, masks, state, want_out):
    ct, nv, m = state
    ll = q.shape[1]
    ig_col = _by_direction(gt, lambda g: g[:, :, 0:1], lambda g: g[:, :, 2:3])
    b_col = _by_direction(gt, lambda g: g[:, :, 1:2], lambda g: g[:, :, 3:4])
    ig_row = _by_direction(gm, lambda g: g[:, 0:1, :], lambda g: g[:, 2:3, :])
    b_row = _by_direction(gm, lambda g: g[:, 1:2, :], lambda g: g[:, 3:4, :])
    b_end = _by_direction(b_row, lambda b: b[:, :, ll - 1:ll], lambda b: b[:, :, 0:1])
    h = None
    if want_out:
        d_log = _by_direction(b_col - b_row + ig_row,
                              lambda x: jnp.where(masks[0], x, NEG_INF),
                              lambda x: jnp.where(masks[1], x, NEG_INF))
        inter_log = b_col + m
        m_t = jnp.maximum(inter_log, jnp.max(d_log, axis=2, keepdims=True))
        scores = _bmm_nt(q, k) * jnp.exp(d_log - m_t)
        inter = jnp.exp(inter_log - m_t)
        qf = q.astype(F32)
        num = _bmm(scores.astype(BF16), v) + inter * _bmm(q, ct.astype(BF16))
        den = jnp.sum(scores, axis=2, keepdims=True) + inter * jnp.sum(qf * nv, axis=2, keepdims=True)
        h = num / jnp.maximum(jnp.abs(den), jnp.exp(-m_t))
    m_loc = jnp.max(b_end - b_row + ig_row, axis=2, keepdims=True)
    w_col = jnp.exp(b_end - b_col + ig_col - m_loc)
    c_loc = _bmm(kt, (w_col * v.astype(F32)).astype(BF16))
    n_loc = jnp.sum(w_col * k.astype(F32), axis=1, keepdims=True)
    m_new = jnp.maximum(b_end + m, m_loc)
    a = jnp.exp(b_end + m - m_new)
    bb = jnp.exp(m_loc - m_new)
    return h, (a * ct + bb * c_loc, a * nv + bb * n_loc, m_new)


def _mlstm_kernel(q_ref, k_ref, kt_ref, v_ref, gt_ref, gm_ref,
                  qc_ref, kc_ref, ktc_ref, vc_ref, gtc_ref, gmc_ref, hf_ref, hb_ref):
    ll = ML_CHUNK
    nc = q_ref.shape[1] // ll
    ncc = qc_ref.shape[1] // ll
    r = lax.broadcasted_iota(jnp.int32, (ll, ll), 0)
    c = lax.broadcasted_iota(jnp.int32, (ll, ll), 1)
    masks = (c <= r, c >= r)

    heads = gt_ref.shape[1]
    head_cols = [slice(hh * ML_HEAD_DIM, (hh + 1) * ML_HEAD_DIM) for hh in range(heads)]

    def rows_of(ci):
        return pl.ds(pl.multiple_of(ci * ll, ll), ll)

    def load(refs, cf, cb):
        qr, kr, ktr, vr, gtr, gmr = refs
        sites = [(hh, ci) for ci in (cf, cb) for hh in range(heads)]
        seq = lambda r: jnp.stack([r[0, rows_of(ci), head_cols[hh]] for hh, ci in sites])
        return (seq(qr), seq(kr), jnp.stack([ktr[0, hh, ci] for hh, ci in sites]), seq(vr),
                jnp.stack([gtr[0, hh, rows_of(ci), :] for hh, ci in sites]),
                jnp.stack([gmr[0, hh, ci] for hh, ci in sites]))

    chains = 2 * heads
    zero = (jnp.zeros((chains, ML_HEAD_DIM, ML_HEAD_DIM), F32), jnp.zeros((chains, 1, ML_HEAD_DIM), F32),
            jnp.zeros((chains, 1, 1), F32))
    ctx_refs = (qc_ref, kc_ref, ktc_ref, vc_ref, gtc_ref, gmc_ref)
    lat_refs = (q_ref, k_ref, kt_ref, v_ref, gt_ref, gm_ref)

    def ctx_body(i, state):
        _, state = _ml_chunks(*load(ctx_refs, i, ncc - 1 - i), masks, state, False)
        return state

    def lat_body(i, state):
        cb = nc - 1 - i
        h, state = _ml_chunks(*load(lat_refs, i, cb), masks, state, True)
        for hh in range(heads):
            hf_ref[0, rows_of(i), head_cols[hh]] = h[hh]
            hb_ref[0, rows_of(cb), head_cols[hh]] = h[heads + hh]
        return state

    state = lax.fori_loop(0, ncc, ctx_body, zero)
    lax.fori_loop(0, nc, lat_body, state)


def _mlstm(q, k, kt, v, gt, gm, qc, kc, ktc, vc, gtc, gmc):
    bz, n, _ = q.shape
    ncx = qc.shape[1]
    hps = ML_HEADS_PER_STEP

    def seq(nn):
        return pl.BlockSpec((1, nn, hps * ML_HEAD_DIM), lambda b, h: (b, 0, h))

    def ktr(nn):
        return pl.BlockSpec((1, hps, nn // ML_CHUNK, ML_HEAD_DIM, ML_CHUNK), lambda b, h: (b, h, 0, 0, 0))

    def gtok(nn):
        return pl.BlockSpec((1, hps, nn, 4), lambda b, h: (b, h, 0, 0))

    def gmaj(nn):
        return pl.BlockSpec((1, hps, nn // ML_CHUNK, 4, ML_CHUNK), lambda b, h: (b, h, 0, 0, 0))

    def specs(nn):
        return [seq(nn), seq(nn), ktr(nn), seq(nn), gtok(nn), gmaj(nn)]

    return pl.pallas_call(
        _mlstm_kernel,
        grid=(bz, ML_HEADS // hps),
        in_specs=specs(n) + specs(ncx),
        out_specs=[seq(n)] * 2,
        out_shape=[jax.ShapeDtypeStruct((bz, n, ML_WIDTH), F32)] * 2,
        compiler_params=_params("parallel", "arbitrary"),
        name="mlstm",
    )(q, k, kt, v, gt, gm, qc, kc, ktc, vc, gtc, gmc)


def _gla_chunks(q, k, v, vt, b, masks, st, want_out):
    ll = q.shape[1]
    qf, kf = q.astype(F32), k.astype(F32)
    ref = _by_direction(b, lambda x: x[:, ll // 2:ll // 2 + 1], lambda x: x[:, ll // 2 - 1:ll // 2])
    b_end = _by_direction(b, lambda x: x[:, ll - 1:ll], lambda x: x[:, 0:1])
    o = None
    if want_out:
        att = _bmm_nt((qf * jnp.exp(b - ref)).astype(BF16), (kf * jnp.exp(ref - b)).astype(BF16))
        att = _by_direction(att, lambda x: jnp.where(masks[0], x, 0.0), lambda x: jnp.where(masks[1], x, 0.0))
        o = _bmm(att.astype(BF16), v) + _bmm_nt((qf * jnp.exp(b)).astype(BF16), st.astype(BF16))
    s_loc = _bmm(vt, (kf * jnp.exp(b_end - b)).astype(BF16))
    return o, jnp.exp(b_end) * st + s_loc


def _gla_kernel(q_ref, k_ref, v_ref, vt_ref, bf_ref, bb_ref,
                qc_ref, kc_ref, vc_ref, vtc_ref, bfc_ref, bbc_ref, of_ref, ob_ref):
    ll = GLA_CHUNK
    nc = q_ref.shape[1] // ll
    ncc = qc_ref.shape[1] // ll
    r = lax.broadcasted_iota(jnp.int32, (ll, ll), 0)
    c = lax.broadcasted_iota(jnp.int32, (ll, ll), 1)
    masks = (c <= r, c >= r)

    heads = vt_ref.shape[1]

    def rows_of(ci):
        return pl.ds(pl.multiple_of(ci * ll, ll), ll)

    head_cols = [slice(hh * GLA_DV, (hh + 1) * GLA_DV) for hh in range(heads)]
    key_cols = [slice(hh * GLA_DK, (hh + 1) * GLA_DK) for hh in range(heads)]

    def load(refs, cf, cb):
        qr, kr, vr, vtr, bfr, bbr = refs
        sites = [(hh, ci) for ci in (cf, cb) for hh in range(heads)]
        per_head = lambda r: jnp.stack([r[0, rows_of(ci), key_cols[hh]] for hh, ci in sites])
        return (per_head(qr), per_head(kr),
                jnp.stack([vr[0, rows_of(ci), head_cols[hh]] for hh, ci in sites]),
                jnp.stack([vtr[0, hh, ci] for hh, ci in sites]),
                jnp.stack([bfr[0, rows_of(cf), key_cols[hh]] for hh in range(heads)]
                          + [bbr[0, rows_of(cb), key_cols[hh]] for hh in range(heads)]))

    zero = jnp.zeros((2 * heads, GLA_DV, GLA_DK), F32)
    ctx_refs = (qc_ref, kc_ref, vc_ref, vtc_ref, bfc_ref, bbc_ref)
    lat_refs = (q_ref, k_ref, v_ref, vt_ref, bf_ref, bb_ref)

    def ctx_body(i, st):
        _, st = _gla_chunks(*load(ctx_refs, i, ncc - 1 - i), masks, st, False)
        return st

    def lat_body(i, st):
        cb = nc - 1 - i
        o, st = _gla_chunks(*load(lat_refs, i, cb), masks, st, True)
        for hh in range(heads):
            of_ref[0, rows_of(i), head_cols[hh]] = o[hh]
            ob_ref[0, rows_of(cb), head_cols[hh]] = o[heads + hh]
        return st

    st = lax.fori_loop(0, ncc, ctx_body, zero)
    lax.fori_loop(0, nc, lat_body, st, unroll=2)


def _gla(q, k, v, vt, bf, bb, qc, kc, vc, vtc, bfc, bbc):
    bz, n, _ = q.shape
    ncx = qc.shape[1]
    hps = GLA_HEADS_PER_STEP

    def key(nn):
        return pl.BlockSpec((1, nn, hps * GLA_DK), lambda b, h: (b, 0, h))

    def val(nn):
        return pl.BlockSpec((1, nn, hps * GLA_DV), lambda b, h: (b, 0, h))

    def valt(nn):
        return pl.BlockSpec((1, hps, nn // GLA_CHUNK, GLA_DV, GLA_CHUNK), lambda b, h: (b, h, 0, 0, 0))

    def specs(nn):
        return [key(nn), key(nn), val(nn), valt(nn), key(nn), key(nn)]

    return pl.pallas_call(
        _gla_kernel,
        grid=(bz, GLA_HEADS // hps),
        in_specs=specs(n) + specs(ncx),
        out_specs=[val(n)] * 2,
        out_shape=[jax.ShapeDtypeStruct((bz, n, GLA_WIDTH), F32)] * 2,
        compiler_params=_params("parallel", "arbitrary"),
        name="gla",
    )(q, k, v, vt, bf, bb, qc, kc, vc, vtc, bfc, bbc)


def _head_rms(h, g, width):
    parts = [_rms(h[:, i:i + width]) for i in range(0, h.shape[1], width)]
    return jnp.concatenate(parts, axis=1) * g


def _oddeven_merge(lo, hi, r):
    step = r * 2
    if step < hi - lo:
        yield from _oddeven_merge(lo, hi, step)
        yield from _oddeven_merge(lo + r, hi, step)
        yield from [(i, i + r) for i in range(lo + r, hi - r, step)]
    else:
        yield (lo, lo + r)


def _oddeven_merge_sort(lo, hi):
    if hi - lo >= 1:
        mid = lo + (hi - lo) // 2
        yield from _oddeven_merge_sort(lo, mid)
        yield from _oddeven_merge_sort(mid + 1, hi)
        yield from _oddeven_merge(lo, hi, 1)


def _exchange(v, i, j):
    v[i], v[j] = jnp.maximum(v[i], v[j]), jnp.minimum(v[i], v[j])


def _merge_top16(v, shift):
    other = [pltpu.roll(x, shift, 0) for x in v]
    out = []
    for j in range(PEER_TOPK):
        a = v[j] if j < len(v) else None
        b = other[PEER_TOPK - 1 - j] if PEER_TOPK - 1 - j < len(v) else None
        out.append(a if b is None else b if a is None else jnp.maximum(a, b))
    return out


def _sort_bitonic16(v):
    v = list(v)
    for d in (8, 4, 2, 1):
        for j in range(PEER_TOPK):
            if not j & d:
                _exchange(v, j, j + d)
    return v


def _sixteenth_largest(blocks):
    v = list(blocks)
    assert SUBLANES <= len(v) <= PEER_TOPK
    for i, j in _oddeven_merge_sort(0, PEER_TOPK - 1):
        if j < len(v):
            _exchange(v, i, j)
    v = _sort_bitonic16(_merge_top16(v, 4))
    v = _sort_bitonic16(_merge_top16(v, 2))
    v = _merge_top16(v, 1)
    return functools.reduce(jnp.minimum, v)[0:1]


def _sorted_top16(tiles):
    v = list(tiles)
    n = len(v)
    for i, j in _oddeven_merge_sort(0, n - 1):
        _exchange(v, i, j)
    for shift in (4, 2, 1):
        v = _sort_bitonic16(_merge_top16(v, shift))
    sub = lax.broadcasted_iota(jnp.int32, v[0].shape, 0)
    halves = []
    for lo in (0, SUBLANES):
        rows = v[lo]
        for r in range(1, SUBLANES):
            rows = jnp.where(sub == r, v[lo + r], rows)
        halves.append(rows)
    repeated = jnp.zeros_like(v[0])
    for r in range(n - 1):
        repeated = jnp.where(v[r] == v[r + 1], 1.0, repeated)
    return jnp.concatenate(halves, axis=0), repeated[0:1]


def _prefix_count(x, thr, test):
    r = [thr[i:i + 1] for i in range(PEER_TOPK)]
    sel = jnp.where
    t8 = test(x, r[7])
    t4 = test(x, sel(t8, r[11], r[3]))
    t2 = test(x, sel(t8, sel(t4, r[13], r[9]), sel(t4, r[5], r[1])))
    t1 = test(x, sel(t8, sel(t4, sel(t2, r[14], r[12]), sel(t2, r[10], r[8])),
                     sel(t4, sel(t2, r[6], r[4]), sel(t2, r[2], r[0]))))
    count = sel(t8, 8.0, 0.0) + sel(t4, 4.0, 0.0) + sel(t2, 2.0, 0.0) + sel(t1, 1.0, 0.0)
    return sel(test(x, r[15]), float(PEER_TOPK), count)


def _count(mask):
    return jnp.sum(jnp.where(mask, 1.0, 0.0), axis=0, keepdims=True)


def _top16(s):
    nk, tl = s.shape
    slot = lax.broadcasted_iota(jnp.int32, (PEER_TOPK, tl), 0)
    key_id = lax.broadcasted_iota(jnp.int32, (nk, tl), 0).astype(F32)

    def body(r, carry):
        s, rank, vals = carry
        m = jnp.max(s, axis=0, keepdims=True)
        first = jnp.min(jnp.where(s == m, key_id, float(nk)), axis=0, keepdims=True)
        sel = key_id == first
        rank = jnp.where(sel, jnp.asarray(r).astype(F32), rank)
        s = jnp.where(sel, NEG_INF, s)
        vals = jnp.where(slot == r, m, vals)
        return s, rank, vals

    init = (s, jnp.full((nk, tl), float(PEER_TOPK), F32), jnp.zeros((PEER_TOPK, tl), F32))
    _, rank, vals = lax.fori_loop(0, PEER_TOPK, body, init)
    return vals, rank


def _pair_candidates(v1, v2):
    blocks = [v1[0:1] + v2[0:8], v1[0:1] + v2[8:16]]
    blocks += [v1[a:a + 1] + v2[0:8] for a in range(1, 8)]
    blocks += [v1[8:16] + v2[0:1]]
    return jnp.concatenate(blocks, axis=0)


def _blocks(cand):
    return [cand[i:i + SUBLANES] for i in range(0, cand.shape[0], SUBLANES)]


def _pick16(cand):
    i = lax.broadcasted_iota(jnp.int32, cand.shape, 0)
    blk, rr = i // 8, i % 8
    flat = jnp.where(blk == 0, rr, jnp.where(blk == 1, 8 + rr,
                     jnp.where(blk <= 8, (blk - 1) * 16 + rr, (8 + rr) * 16))).astype(F32)

    def body(_, carry):
        cand, picked = carry
        m = jnp.max(cand, axis=0, keepdims=True)
        first = jnp.min(jnp.where(cand == m, flat, 1e9), axis=0, keepdims=True)
        sel = flat == first
        return jnp.where(sel, NEG_INF, cand), jnp.where(sel, 1.0, picked)

    _, picked = lax.fori_loop(0, PEER_TOPK, body, (cand, jnp.zeros_like(cand)))
    return picked


def _first_key_thresholds(picked, v1):
    inf = jnp.inf
    lo = jnp.where(picked[0:8] > 0.0, v1[0:1], inf)
    for a in range(1, 8):
        lo = jnp.minimum(lo, jnp.where(picked[8 * (a + 1):8 * (a + 2)] > 0.0, v1[a:a + 1], inf))
    tail = jnp.min(jnp.where(picked[72:80] > 0.0, v1[8:16], inf), axis=0, keepdims=True)
    row = lax.broadcasted_iota(jnp.int32, lo.shape, 0)
    lo = jnp.minimum(lo, jnp.where(row == 0, tail, inf))
    hi = jnp.where(picked[8:16] > 0.0, v1[0:1], inf)
    return jnp.concatenate([lo, hi], axis=0)


def _pair_counts(cand, picked):
    z = jnp.sum(picked * jnp.exp(cand - cand[0:1]), axis=0, keepdims=True)
    n_rows = [jnp.sum(picked[0:16], axis=0, keepdims=True)]
    n_rows += [jnp.sum(picked[8 * (a + 1):8 * (a + 2)], axis=0, keepdims=True) for a in range(1, 8)]
    return jnp.concatenate(n_rows + [picked[72:80]], axis=0), z


def _mix_kernel(x_ref, mhf_ref, mhb_ref, ghf_ref, ghb_ref, mo_ref, gr_ref, mlg_ref, glag_ref,
                wout_ref, g1_ref, sh2_ref, sc2_ref, n2g_ref, wq_ref, keys_ref,
                x1_ref, h2t_ref, e1_ref, l_ref, e2_ref, rk_ref, q_scr, s_scr):
    ml = _head_rms(mhf_ref[0] + mhb_ref[0], mlg_ref[...], ML_HEAD_DIM)
    ml = _sigmoid(mo_ref[0].astype(F32)) * ml
    gl = _head_rms(ghf_ref[0] + ghb_ref[0], glag_ref[...], GLA_DV)
    gr = gr_ref[0].astype(F32)
    gl = gr * _sigmoid(gr) * gl
    mix = jnp.concatenate([ml, gl], axis=1).astype(BF16)
    x1 = x_ref[0] + g1_ref[0] * _dot(mix, wout_ref[...])
    x1_ref[0] = x1
    h2 = _rms(x1) * n2g_ref[...] * (1.0 + sc2_ref[0]) + sh2_ref[0]
    h2t_ref[...] = h2.T.astype(BF16)
    qall = _dot(h2.astype(BF16), wq_ref[...]).astype(BF16)
    for j in range(2 * PEER_HEADS):
        q_scr[j] = qall[:, j * PEER_HALF:(j + 1) * PEER_HALF]

    tm = x_ref.shape[1]
    topk = float(PEER_TOPK)
    lane_chunks = [slice(i, i + LANES) for i in range(0, tm, LANES)]

    def write_tables(p, lanes, s1, s2, v1, v2, z, lim, rank2):
        e1_ref[p, :, lanes] = jnp.exp(s1 - v1[0:1]) * (0.5 / z)
        l_ref[p, :, lanes] = lim
        e2 = jnp.exp(s2 - v2[0:1]).astype(BF16)
        e2_ref[p, :, :, lanes] = e2.reshape(PEER_NKEYS // BF16_ROWS, BF16_ROWS, LANES)
        rk_ref[p, :, :, lanes] = rank2.astype(BF16).reshape(PEER_NKEYS // BF16_ROWS, BF16_ROWS, LANES)

    def head_body(p, carry):
        for hf in range(2):
            s_scr[hf] = _dot_nt(keys_ref[2 * p + hf], q_scr[2 * p + hf])
        excess = jnp.zeros((1, LANES), F32)
        vals = []
        for hf in range(2):
            for lanes in lane_chunks:
                tiles = [s_scr[hf, i:i + SUBLANES, lanes] for i in range(0, PEER_NKEYS, SUBLANES)]
                top, repeated = _sorted_top16(tiles)
                vals.append(top)
                excess = jnp.maximum(excess, repeated)
        v1s, v2s = vals[:len(lane_chunks)], vals[len(lane_chunks):]
        cands = [_pair_candidates(v1, v2) for v1, v2 in zip(v1s, v2s)]
        taus = [_sixteenth_largest(_blocks(c)) for c in cands]
        for lanes, v1, v2, cand, tau in zip(lane_chunks, v1s, v2s, cands, taus):
            s1, s2 = s_scr[0, :, lanes], s_scr[1, :, lanes]
            picked = jnp.where(cand >= tau, 1.0, 0.0)
            _, z = _pair_counts(cand, picked)
            theta = _first_key_thresholds(picked, v1)
            lim = _prefix_count(s1, theta, lambda x, t: x >= t)
            rank2 = _prefix_count(s2, v2, lambda x, t: x < t)
            write_tables(p, lanes, s1, s2, v1, v2, z, lim, rank2)
            n_marked = (_count(s1 >= v1[PEER_TOPK - 1:PEER_TOPK]) + _count(rank2 < topk)
                        + jnp.sum(picked, axis=0, keepdims=True))
            excess = jnp.maximum(excess, n_marked - 3.0 * topk)

        @pl.when(jnp.max(excess) > 0.0)
        def _():
            for lanes in lane_chunks:
                s1, s2 = s_scr[0, :, lanes], s_scr[1, :, lanes]
                v1, rank1 = _top16(s1)
                v2, rank2 = _top16(s2)
                cand = _pair_candidates(v1, v2)
                n, z = _pair_counts(cand, _pick16(cand))
                lim = jnp.zeros_like(rank1)
                for a in range(PEER_TOPK):
                    lim = jnp.where(rank1 == float(a), n[a:a + 1], lim)
                write_tables(p, lanes, s1, s2, v1, v2, z, lim, rank2)

        return carry

    lax.fori_loop(0, PEER_HEADS, head_body, 0)


def _mix(x, mhf, mhb, ghf, ghb, mo, gr, mlg, glag, wout, g1, sh2, sc2, n2g, wq, keys):
    bz, n, d = x.shape
    tm = min(MIX_TM, n)
    assert n % tm == 0 and tm % LANES == 0
    nt = n // tm
    t_all = bz * n

    def tok(width):
        return pl.BlockSpec((1, tm, width), lambda b, i: (b, i, 0))

    def full(a):
        return pl.BlockSpec(a.shape, lambda b, i, nd=a.ndim: (0,) * nd)

    vec = pl.BlockSpec((1, 1, d), lambda b, i: (b, 0, 0))
    tab1 = pl.BlockSpec((PEER_HEADS, PEER_NKEYS, tm), lambda b, i: (0, 0, b * nt + i))
    tab2 = pl.BlockSpec((PEER_HEADS, PEER_NKEYS // BF16_ROWS, BF16_ROWS, tm),
                        lambda b, i: (0, 0, 0, b * nt + i))
    tab1_shape = jax.ShapeDtypeStruct((PEER_HEADS, PEER_NKEYS, t_all), F32)
    tab2_shape = jax.ShapeDtypeStruct((PEER_HEADS, PEER_NKEYS // BF16_ROWS, BF16_ROWS, t_all), BF16)
    return pl.pallas_call(
        _mix_kernel,
        grid=(bz, nt),
        in_specs=[tok(d)] + [tok(ML_WIDTH)] * 6 + [full(mlg), full(glag), full(wout), vec, vec, vec,
                                                    full(n2g), full(wq), full(keys)],
        out_specs=[tok(d), pl.BlockSpec((d, tm), lambda b, i: (0, b * nt + i)), tab1, tab1, tab2, tab2],
        out_shape=[jax.ShapeDtypeStruct((bz, n, d), F32), jax.ShapeDtypeStruct((d, t_all), BF16),
                   tab1_shape, tab1_shape, tab2_shape, tab2_shape],
        scratch_shapes=[pltpu.VMEM((2 * PEER_HEADS, tm, PEER_HALF), BF16),
                        pltpu.VMEM((2, PEER_NKEYS, tm), F32)],
        compiler_params=_params("parallel", "arbitrary"),
        name="mix",
    )(x, mhf, mhb, ghf, ghb, mo, gr, mlg, glag, wout, g1, sh2, sc2, n2g, wq, keys)


def _peer_kernel(h2t_ref, u_ref, vt_ref, e1_ref, l_ref, e2_ref, rk_ref, x1_ref, g2_ref, nfg_ref,
                 out_ref, acc_ref):
    j = pl.program_id(2)
    te, tm = u_ref.shape[0], h2t_ref.shape[1]
    rows_per_step = te // PEER_NKEYS
    tiles = PEER_NKEYS // BF16_ROWS

    @pl.when(j == 0)
    def _():
        acc_ref[...] = jnp.zeros_like(acc_ref)

    act = _dot(u_ref[...], h2t_ref[...])
    act = act.astype(BF16)
    act = (act * (1.0 + lax.erf(act * 2.0 ** -0.5))).reshape(rows_per_step, tiles, BF16_ROWS, tm)
    parts = []
    for i in range(rows_per_step):
        w = None
        for p in range(PEER_HEADS):
            e1 = jnp.broadcast_to(e1_ref[p, i:i + 1, :], (BF16_ROWS, tm)).astype(BF16)[None]
            lim = jnp.broadcast_to(l_ref[p, i:i + 1, :], (BF16_ROWS, tm)).astype(BF16)[None]
            term = jnp.where(rk_ref[p] < lim, e2_ref[p] * e1, jnp.zeros((), BF16))
            w = term if w is None else w + term
        parts.append(w * act[i])
    gated = jnp.concatenate(parts, axis=0).reshape(te, tm)
    acc_ref[...] += _dot(vt_ref[...], gated)

    @pl.when(j == pl.num_programs(2) - 1)
    def _():
        xf = x1_ref[0] + g2_ref[0] * acc_ref[...].T
        out_ref[0] = _rms(xf) * nfg_ref[...]


def _peer(h2t, u, vt, e1, lim, e2, rk, x1, g2, nfg):
    bz, n, d = x1.shape
    n_exp = u.shape[0]
    tm = min(PEER_TM, n)
    te = PEER_TE
    assert n % tm == 0 and n_exp % te == 0 and te % PEER_NKEYS == 0
    nt = n // tm
    tab1 = pl.BlockSpec((PEER_HEADS, te // PEER_NKEYS, tm), lambda b, i, j: (0, j, b * nt + i))
    tab2 = pl.BlockSpec((PEER_HEADS, PEER_NKEYS // BF16_ROWS, BF16_ROWS, tm),
                        lambda b, i, j: (0, 0, 0, b * nt + i))
    tok = pl.BlockSpec((1, tm, d), lambda b, i, j: (b, i, 0))
    return pl.pallas_call(
        _peer_kernel,
        grid=(bz, nt, n_exp // te),
        in_specs=[pl.BlockSpec((d, tm), lambda b, i, j: (0, b * nt + i)),
                  pl.BlockSpec((te, d), lambda b, i, j: (j, 0)),
                  pl.BlockSpec((d, te), lambda b, i, j: (0, j)),
                  tab1, tab1, tab2, tab2, tok,
                  pl.BlockSpec((1, 1, d), lambda b, i, j: (b, 0, 0)),
                  pl.BlockSpec((1, d), lambda b, i, j: (0, 0))],
        out_specs=tok,
        out_shape=jax.ShapeDtypeStruct((bz, n, d), F32),
        scratch_shapes=[pltpu.VMEM((d, tm), F32)],
        compiler_params=_params("parallel", "parallel", "arbitrary"),
        name="peer",
    )(h2t, u, vt, e1, lim, e2, rk, x1, g2, nfg)


def _inproj_weights(w_in, conv_w, gate_b, lr_w2, alpha_b):
    widths = (ML_WIDTH, ML_WIDTH, ML_WIDTH, ML_WIDTH, N_GATES,
              GLA_KEY_WIDTH, GLA_KEY_WIDTH, GLA_WIDTH, GLA_WIDTH, 2 * GLA_RANK)
    offs = [0]
    for w in widths:
        offs.append(offs[-1] + w)
    wb = w_in.astype(BF16)
    col = lambda a, b: wb[:, offs[a]:offs[b]]
    pad_lanes = lambda a: jnp.pad(a, ((0, 0), (0, LANES - a.shape[1])))
    w2 = jnp.zeros((LANES, 2 * GLA_KEY_WIDTH), F32)
    w2 = w2.at[:GLA_RANK, :GLA_KEY_WIDTH].set(lr_w2[0])
    w2 = w2.at[GLA_RANK:2 * GLA_RANK, GLA_KEY_WIDTH:].set(lr_w2[1])
    return (col(0, 2), col(2, 3), col(3, 4), pad_lanes(col(4, 5)), col(4, 5).T,
            col(5, 7), col(7, 8), col(8, 9), pad_lanes(col(9, 10)),
            conv_w, pad_lanes(gate_b[None, :]), gate_b[:, None],
            w2.astype(BF16), alpha_b.reshape(1, 2 * GLA_KEY_WIDTH))


def _per_head_gates(gtok, gmaj):
    bz, n, _ = gtok.shape
    gt = gtok.reshape(bz, n, 2, 2, ML_HEADS).transpose(0, 4, 1, 2, 3).reshape(bz, ML_HEADS, n, 4)
    gm = gmaj.reshape(bz, 2, 2, ML_HEADS, n // ML_CHUNK, ML_CHUNK)
    gm = gm.transpose(0, 3, 4, 1, 2, 5).reshape(bz, ML_HEADS, n // ML_CHUNK, 4, ML_CHUNK)
    return gt, gm


def _token_mix_inputs(x, sh, sc, ng, wts, row_len, with_queries):
    o = _inproj(x, sh, sc, ng, wts, row_len, with_queries)
    gt, gm = _per_head_gates(o["gtok"], o["gmaj"])
    mq, gq = (o["mq"], o["gq"]) if with_queries else (o["mk"], o["gk"])
    return ((mq, o["mk"], o["mkt"], o["mv"], gt, gm), (gq, o["gk"], o["gv"], o["gvt"], o["bf"], o["bb"]),
            o.get("mo"), o.get("gr"))


def kernel(x, c, ctx, c_ctx, w_mod, b_mod, norm1_g, w_in, ml_conv_w, ml_gate_b, ml_norm_g,
           gla_lr_w2, gla_alpha_b, gla_norm_g, w_out, norm2_g, peer_wq, peer_keys, peer_u,
           peer_v, norm_f_g):
    assert w_mod.shape[0] == 1, "single trunk layer"
    bz, n, d = x.shape
    c_all = jnp.concatenate([c, c_ctx[None, :]], axis=0)
    c_all = jnp.pad(c_all, ((0, (-c_all.shape[0]) % SUBLANES), (0, 0)))
    mod = _modulation(c_all, w_mod[0], b_mod[0][None, :])
    sh1, sc1, g1, sh2, sc2, g2 = [m[:, None, :] for m in jnp.split(mod[:bz], 6, axis=1)]
    mod_c = mod[bz]
    csh1, csc1 = mod_c[None, None, :d], mod_c[None, None, d:2 * d]

    ng1 = norm1_g[0][None, :]
    wts = _inproj_weights(w_in[0], ml_conv_w[0], ml_gate_b[0], gla_lr_w2[0], gla_alpha_b[0])
    ml_lat, gla_lat, mo, gr = _token_mix_inputs(x, sh1, sc1, ng1, wts, GRID_W, True)
    ml_ctx, gla_ctx, _, _ = _token_mix_inputs(ctx, csh1, csc1, ng1, wts, ctx.shape[1], False)
    mhf, mhb = _mlstm(*ml_lat, *ml_ctx)
    ghf, ghb = _gla(*gla_lat, *gla_ctx)

    keys = peer_keys[0].reshape(2 * PEER_HEADS, PEER_NKEYS, PEER_HALF).astype(BF16)
    x1, h2t, e1, lim, e2, rk = _mix(
        x, mhf, mhb, ghf, ghb, mo, gr, ml_norm_g[0][None, :], gla_norm_g[0][None, :],
        w_out[0].astype(BF16), g1, sh2, sc2, norm2_g[0][None, :], peer_wq[0].astype(BF16), keys)
    return _peer(h2t, peer_u[0].astype(BF16), peer_v[0].astype(BF16).T, e1, lim, e2, rk,
                 x1, g2, norm_f_g[None, :])
```
